```python
import math
import jax, jax.numpy as jnp
from jax import lax
import numpy as np

D_MODEL = 1024
BATCH = 8
SEQ = 2048
DEPTH = 1

D_A = D_MODEL
SGU_CHUNK = 128
SGU_GROUPS = 8
RWKV_HEAD = 64
D_B = D_MODEL
N_HEADS_B = D_B // RWKV_HEAD

def _lora_dim(factor, power):
    return max(32, int(round(factor * D_MODEL ** power / 32)) * 32)

LORA_W = _lora_dim(1.8, 0.5)
LORA_A = _lora_dim(1.8, 0.5)
LORA_G = _lora_dim(0.6, 0.8)
C_B = 3 * D_B + LORA_W + LORA_A + LORA_G
P_TOTAL = 2 * D_A + C_B + 2 * D_MODEL
D_FF = 4 * D_MODEL
NORM_EPS = 1e-6
LN_EPS = 1e-5
GN_EPS = 64e-5

kernel_name = "hybrid_gmlp_rwkv7_gated_block"


def _rms_norm(x, g):
    xf = x.astype(jnp.float32)
    y = xf * lax.rsqrt(jnp.mean(xf * xf, axis=-1, keepdims=True) + NORM_EPS)
    return (y * g.astype(jnp.float32)).astype(x.dtype)


def _layer_norm(x, g, b):
    xf = x.astype(jnp.float32)
    mu = jnp.mean(xf, axis=-1, keepdims=True)
    var = jnp.mean(jnp.square(xf - mu), axis=-1, keepdims=True)
    y = (xf - mu) * lax.rsqrt(var + LN_EPS)
    return (y * g.astype(jnp.float32) + b.astype(jnp.float32)).astype(x.dtype)


def _token_shift(p):
    return jnp.pad(p, ((0, 0), (1, 0), (0, 0)))[:, :-1, :]


def _sgu_branch(p_sgu, ln_w, ln_b, sgu_w, sgu_b, w_proj_a):
    B, T, _ = p_sgu.shape
    z = jax.nn.gelu(p_sgu, approximate=False)
    u, v = jnp.split(z, 2, axis=-1)
    v = _layer_norm(v, ln_w, ln_b)
    n_chunks = T // SGU_CHUNK
    dg = D_A // SGU_GROUPS
    v = v.reshape(B, n_chunks, SGU_CHUNK, SGU_GROUPS, dg)
    mask = jnp.tril(jnp.ones((SGU_CHUNK, SGU_CHUNK), dtype=sgu_w.dtype))
    ws = sgu_w * mask[None]
    sv = jnp.einsum('gij,bcjgd->bcigd', ws, v)
    sv = sv + jnp.swapaxes(sgu_b, 0, 1)[None, None, :, :, None]
    s = u * sv.reshape(B, T, D_A)
    return s @ w_proj_a


def _rwkv7_scan(r, w, k, v, kk, a):
    B, T, H, N = r.shape
    xs = tuple(jnp.moveaxis(t, 1, 0) for t in (r, w, k, v, kk, a))

    def step(S, inp):
        r_t, w_t, k_t, v_t, kk_t, a_t = inp
        sa = jnp.einsum('bhvk,bhk->bhv', S, -kk_t)
        S = (S * w_t[:, :, None, :]
             + sa[..., None] * (kk_t * a_t)[:, :, None, :]
             + v_t[..., None] * k_t[:, :, None, :])
        o = jnp.einsum('bhvk,bhk->bhv', S, r_t)
        return S, o

    S0 = jnp.zeros((B, H, N, N), dtype=jnp.float32)
    _, o = lax.scan(step, S0, xs)
    return jnp.moveaxis(o, 0, 1)


def _rwkv7_branch(p_rwkv, shift_b, w_lora_w, w0, a_lora_w, a0, g_lora_w,
                  k_k, k_a, r_k, ln_x_w, ln_x_b, w_proj_b):
    B, T, _ = p_rwkv.shape
    f32 = jnp.float32
    p = p_rwkv.astype(f32)
    sb = shift_b.astype(f32)
    q = p * sb[0] + _token_shift(p) * sb[1]
    cuts = np.cumsum([D_B, D_B, D_B, LORA_W, LORA_A]).tolist()
    r, k, v, xw, xa, xg = jnp.split(q, cuts, axis=-1)
    w = -jax.nn.softplus(-(w0.astype(f32) + jnp.tanh(xw) @ w_lora_w.astype(f32))) - 0.5
    decay = jnp.exp(-jnp.exp(w))
    aa = jax.nn.sigmoid(a0.astype(f32) + xa @ a_lora_w.astype(f32))
    g = jax.nn.sigmoid(xg) @ g_lora_w.astype(f32)
    kk = (k * k_k.astype(f32)).reshape(B, T, N_HEADS_B, RWKV_HEAD)
    kk = kk / jnp.maximum(jnp.linalg.norm(kk, axis=-1, keepdims=True), 1e-12)
    k = k * (1.0 + (aa - 1.0) * k_a.astype(f32))
    hs = lambda t: t.reshape(B, T, N_HEADS_B, RWKV_HEAD)
    rh, kh, vh = hs(r), hs(k), hs(v)
    o = _rwkv7_scan(rh, hs(decay), kh, vh, kk, hs(aa))
    mu = jnp.mean(o, axis=-1, keepdims=True)
    var = jnp.mean(jnp.square(o - mu), axis=-1, keepdims=True)
    o = ((o - mu) * lax.rsqrt(var + GN_EPS)).reshape(B, T, D_B)
    o = o * ln_x_w.astype(f32) + ln_x_b.astype(f32)
    r_k_h = r_k.astype(f32).reshape(N_HEADS_B, RWKV_HEAD)
    bonus = jnp.sum(rh * kh * r_k_h, axis=-1, keepdims=True) * vh
    o = (o + bonus.reshape(B, T, D_B)) * g
    return (o @ w_proj_b.astype(f32)).astype(p_rwkv.dtype)


def setup_inputs(seed: int = 0) -> dict:
    key = jax.random.key(seed)
    ks = jax.random.split(key, 32)
    L = DEPTH
    f32 = jnp.float32

    def nrm(k, shape, scale):
        return jax.random.normal(k, shape, f32) * scale

    mu = jax.random.uniform(ks[8], (L, C_B), f32)
    return {
        "x": nrm(ks[0], (BATCH, SEQ, D_MODEL), 1.0),
        "g_mix": 1.0 + nrm(ks[1], (L, D_MODEL), 0.02),
        "w_in": nrm(ks[2], (L, D_MODEL, P_TOTAL), D_MODEL ** -0.5),
        "sgu_ln_w": 1.0 + nrm(ks[3], (L, D_A), 0.02),
        "sgu_ln_b": nrm(ks[4], (L, D_A), 0.02),
        "sgu_w": nrm(ks[5], (L, SGU_GROUPS, SGU_CHUNK, SGU_CHUNK), SGU_CHUNK ** -0.5),
        "sgu_b": 1.0 + nrm(ks[6], (L, SGU_GROUPS, SGU_CHUNK), 0.02),
        "w_proj_a": nrm(ks[7], (L, D_A, D_MODEL), D_A ** -0.5),
        "shift_b": jnp.stack([1.0 - mu, mu], axis=1),
        "w_lora_w": nrm(ks[9], (L, LORA_W, D_B), 0.1 * LORA_W ** -0.5),
        "w0": jax.random.uniform(ks[10], (L, D_B), f32, minval=-4.0, maxval=1.0),
        "a_lora_w": nrm(ks[11], (L, LORA_A, D_B), LORA_A ** -0.5),
        "a0": nrm(ks[12], (L, D_B), 0.1),
        "g_lora_w": nrm(ks[13], (L, LORA_G, D_B), LORA_G ** -0.5),
        "k_k": 0.85 + nrm(ks[14], (L, D_B), 0.02),
        "k_a": 1.0 + nrm(ks[15], (L, D_B), 0.02),
        "r_k": nrm(ks[16], (L, D_B), 0.1),
        "ln_x_w": 1.0 + nrm(ks[17], (L, D_B), 0.02),
        "ln_x_b": nrm(ks[18], (L, D_B), 0.02),
        "w_proj_b": nrm(ks[19], (L, D_B, D_MODEL), D_B ** -0.5),
        "w_out": nrm(ks[20], (L, D_MODEL, D_MODEL), D_MODEL ** -0.5),
        "g_ffn": 1.0 + nrm(ks[21], (L, D_MODEL), 0.02),
        "w_ffn1": nrm(ks[22], (L, D_MODEL, D_FF), D_MODEL ** -0.5),
        "w_ffn2": nrm(ks[23], (L, D_FF, D_MODEL), D_FF ** -0.5),
        "g_final": 1.0 + nrm(ks[24], (D_MODEL,), 0.02),
    }


def reference(x, g_mix, w_in, sgu_ln_w, sgu_ln_b, sgu_w, sgu_b, w_proj_a, shift_b,
              w_lora_w, w0, a_lora_w, a0, g_lora_w, k_k, k_a, r_k, ln_x_w, ln_x_b,
              w_proj_b, w_out, g_ffn, w_ffn1, w_ffn2, g_final):
    h = x
    for l in range(DEPTH):
        a = _rms_norm(h, g_mix[l])
        p = a @ w_in[l]
        p_sgu, p_rwkv, p_gate = jnp.split(p, [2 * D_A, 2 * D_A + C_B], axis=-1)
        y_a = _sgu_branch(p_sgu, sgu_ln_w[l], sgu_ln_b[l], sgu_w[l], sgu_b[l], w_proj_a[l])
        y_b = _rwkv7_branch(p_rwkv, shift_b[l], w_lora_w[l], w0[l], a_lora_w[l], a0[l],
                            g_lora_w[l], k_k[l], k_a[l], r_k[l], ln_x_w[l], ln_x_b[l],
                            w_proj_b[l])
        gate_a, gate_b = jnp.split(p_gate, 2, axis=-1)
        mixed = jax.nn.sigmoid(gate_a) * y_a + jax.nn.sigmoid(gate_b) * y_b
        h = h + mixed @ w_out[l]
        f = _rms_norm(h, g_ffn[l])
        h = h + jnp.square(jax.nn.relu(f @ w_ffn1[l])) @ w_ffn2[l]
    return _rms_norm(h, g_final)
```

```python
import functools
import math

import jax
import jax.numpy as jnp
from jax import lax
from jax.experimental import pallas as pl
from jax.experimental.pallas import tpu as pltpu

F32 = jnp.float32
BF16 = jnp.bfloat16

SGU_CHUNK = 128
SGU_GROUPS = 8
RWKV_HEAD = 64
NORM_EPS = 1e-6
LN_EPS = 1e-5
GN_EPS = 64e-5

LANES = 128
SCAN_CHUNK = 64
SCAN_HEADS = 4
VMEM_LIMIT = 56 * 1024 * 1024


def _cparams(sem):
    return pltpu.CompilerParams(dimension_semantics=sem, vmem_limit_bytes=VMEM_LIMIT)


def _dot(a, b, dims=(((1,), (0,)), ((), ()))):
    return lax.dot_general(a, b, dims, preferred_element_type=F32)


_NT = (((1,), (1,)), ((), ()))
_TN = (((0,), (0,)), ((), ()))


def _split2(x):
    hi = x.astype(BF16)
    lo = (x - hi.astype(F32)).astype(BF16)
    return hi, lo


def _split3(x):
    hi = x.astype(BF16)
    r1 = x - hi.astype(F32)
    mid = r1.astype(BF16)
    lo = (r1 - mid.astype(F32)).astype(BF16)
    return hi, mid, lo


def _dot_hp(a2, b2, dims=(((1,), (0,)), ((), ()))):
    ah, al = a2
    bh, bl = b2
    return _dot(ah, bh, dims) + (_dot(ah, bl, dims) + _dot(al, bh, dims))


def _dot_x2(x, w_exact):
    hi, lo = _split2(x)
    return _dot(hi, w_exact) + _dot(lo, w_exact)


def _head_sum(x, e_ref, et_ref):
    return _dot_x2(_dot_x2(x, e_ref[...]), et_ref[...])


def _rms_rows(x, g):
    ms = jnp.mean(x * x, axis=-1, keepdims=True)
    return (x * lax.rsqrt(ms + NORM_EPS)) * g


def _in_proj_kernel(x_ref, g_ref, w_ref, o_ref, a_scr, *, row_chunk):
    @pl.when(pl.program_id(1) == 0)
    def _():
        def body(i, c):
            rows = pl.ds(pl.multiple_of(i * row_chunk, row_chunk), row_chunk)
            a_scr[rows, :] = _rms_rows(x_ref[rows, :], g_ref[...]).astype(BF16)
            return c
        lax.fori_loop(0, x_ref.shape[0] // row_chunk, body, 0)

    o_ref[...] = _dot(a_scr[...], w_ref[...])


def _in_proj(x2, g, w16, tm, tn):
    m, d = x2.shape
    n = w16.shape[1]
    return pl.pallas_call(
        functools.partial(_in_proj_kernel, row_chunk=256),
        grid=(m // tm, n // tn),
        in_specs=[
            pl.BlockSpec((tm, d), lambda i, j: (i, 0)),
            pl.BlockSpec((1, d), lambda i, j: (0, 0)),
            pl.BlockSpec((d, tn), lambda i, j: (0, j)),
        ],
        out_specs=pl.BlockSpec((tm, tn), lambda i, j: (i, j)),
        out_shape=jax.ShapeDtypeStruct((m, n), F32),
        scratch_shapes=[pltpu.VMEM((tm, d), BF16)],
        compiler_params=_cparams(("parallel", "arbitrary")),
        name="in_proj",
    )(x2, g, w16)


def _gelu(x):
    return 0.5 * x * (1.0 + lax.erf(x * (1.0 / math.sqrt(2.0))))


def _sgu_kernel(u_ref, v_ref, ga_ref, lnw_ref, lnb_ref, ws_ref, bias_ref, wpa_ref, o_ref,
                wm_scr, s_scr):
    ch = SGU_CHUNK
    dg = LANES

    @pl.when(pl.program_id(0) == 0)
    def _():
        row = lax.broadcasted_iota(jnp.int32, (ch, ch), 0)
        col = lax.broadcasted_iota(jnp.int32, (ch, ch), 1)
        for g in range(SGU_GROUPS):
            wm_scr[g] = jnp.where(col <= row, ws_ref[g], 0.0).astype(BF16)

    for c in range(u_ref.shape[0] // ch):
        rows = pl.ds(c * ch, ch)
        zu = _gelu(u_ref[rows, :])
        zv = _gelu(v_ref[rows, :])
        mu = jnp.mean(zv, axis=-1, keepdims=True)
        zc = zv - mu
        var = jnp.mean(zc * zc, axis=-1, keepdims=True)
        vn = ((zc * lax.rsqrt(var + LN_EPS)) * lnw_ref[...] + lnb_ref[...]).astype(BF16)
        for g in range(SGU_GROUPS):
            cols = slice(g * dg, (g + 1) * dg)
            sv = _dot(wm_scr[g], vn[:, cols]) + bias_ref[:, cols]
            s_scr[rows, cols] = (zu[:, cols] * sv).astype(BF16)

    ya = _dot(s_scr[...], wpa_ref[...])
    o_ref[...] = jax.nn.sigmoid(ga_ref[...]) * ya


def _sgu(p_main, ln_w, ln_b, sgu_w, bias_full, wpa16, ts, d):
    m = p_main.shape[0]
    ch = SGU_CHUNK
    return pl.pallas_call(
        _sgu_kernel,
        grid=(m // ts,),
        in_specs=[
            pl.BlockSpec((ts, d), lambda i: (i, 0)),
            pl.BlockSpec((ts, d), lambda i: (i, 1)),
            pl.BlockSpec((ts, d), lambda i: (i, 5)),
            pl.BlockSpec((1, d), lambda i: (0, 0)),
            pl.BlockSpec((1, d), lambda i: (0, 0)),
            pl.BlockSpec((SGU_GROUPS, ch, ch), lambda i: (0, 0, 0)),
            pl.BlockSpec((ch, d), lambda i: (0, 0)),
            pl.BlockSpec((d, d), lambda i: (0, 0)),
        ],
        out_specs=pl.BlockSpec((ts, d), lambda i: (i, 0)),
        out_shape=jax.ShapeDtypeStruct((m, d), F32),
        scratch_shapes=[pltpu.VMEM((SGU_GROUPS, ch, ch), BF16), pltpu.VMEM((ts, d), BF16)],
        compiler_params=_cparams(("arbitrary",)),
        name="sgu",
    )(p_main, p_main, p_main, ln_w, ln_b, sgu_w, bias_full, wpa16)


def _shift_mix(cur_ref, prev_ref, sb0, sb1, first):
    p = cur_ref[...]
    tail = prev_ref[...]
    prev_row = jnp.where(first, 0.0, tail[tail.shape[0] - 1:, :])
    rolled = pltpu.roll(p, 1, 0)
    row = lax.broadcasted_iota(jnp.int32, p.shape, 0)
    shifted = jnp.where(row == 0, prev_row, rolled)
    return p * sb0 + shifted * sb1


def _softplus(z):
    return jnp.maximum(z, 0.0) + jnp.log1p(jnp.exp(-jnp.abs(z)))


def _rwkv_prep_kernel(pr_ref, pk_ref, pv_ref, pl_ref, qr_ref, qk_ref, qv_ref, ql_ref,
                      sb_ref, sbl_ref, ww_ref, wa_ref, wg_ref, vec_ref, e_ref, et_ref,
                      r_out, k_out, v_out, lw_out, kk_out, aa_out, g_out):
    first = pl.program_id(1) == 0
    d = r_out.shape[1]
    r = _shift_mix(pr_ref, qr_ref, sb_ref[0:1, 0:d], sb_ref[1:2, 0:d], first)
    k = _shift_mix(pk_ref, qk_ref, sb_ref[0:1, d:2 * d], sb_ref[1:2, d:2 * d], first)
    v = _shift_mix(pv_ref, qv_ref, sb_ref[0:1, 2 * d:3 * d], sb_ref[1:2, 2 * d:3 * d], first)
    lo = _shift_mix(pl_ref, ql_ref, sbl_ref[0:1, :], sbl_ref[1:2, :], first)
    xw = lo[:, 0:LANES]
    xa = lo[:, LANES:2 * LANES]
    xg = lo[:, 2 * LANES:]
    w0 = vec_ref[0:1, :]
    a0 = vec_ref[1:2, :]
    k_k = vec_ref[2:3, :]
    k_a = vec_ref[3:4, :]

    zw = w0 + _dot_hp(_split2(jnp.tanh(xw)), _split2(ww_ref[...]))
    wlog = -_softplus(-zw) - 0.5
    lw_out[...] = -jnp.exp(wlog)
    aa = jax.nn.sigmoid(a0 + _dot_hp(_split2(xa), _split2(wa_ref[...])))
    g_out[...] = _dot_hp(_split2(jax.nn.sigmoid(xg)), _split2(wg_ref[...]))
    kraw = k * k_k
    ss = _head_sum(kraw * kraw, e_ref, et_ref)
    kk_out[...] = kraw / jnp.maximum(jnp.sqrt(ss), 1e-12)
    k_out[...] = k * (1.0 + (aa - 1.0) * k_a)
    r_out[...] = r
    v_out[...] = v
    aa_out[...] = aa


def _rwkv_prep(p_main, p_lora, sb_main, sb_lora, ww, wa, wg, vecs, e, et, batch, seq, d, tt):
    m = batch * seq
    nt = seq // tt
    tail = 8
    lw = p_lora.shape[1]

    def cur(cb):
        return pl.BlockSpec((tt, d), lambda b, t: (b * nt + t, cb))

    def prev(cb):
        return pl.BlockSpec(
            (tail, d), lambda b, t: (jnp.maximum((b * nt + t) * (tt // tail) - 1, 0), cb))

    full = lambda a: pl.BlockSpec(a.shape, lambda b, t: (0,) * a.ndim)
    out_spec = pl.BlockSpec((tt, d), lambda b, t: (b * nt + t, 0))
    outs = pl.pallas_call(
        _rwkv_prep_kernel,
        grid=(batch, nt),
        in_specs=[
            cur(2), cur(3), cur(4),
            pl.BlockSpec((tt, lw), lambda b, t: (b * nt + t, 0)),
            prev(2), prev(3), prev(4),
            pl.BlockSpec((tail, lw),
                         lambda b, t: (jnp.maximum((b * nt + t) * (tt // tail) - 1, 0), 0)),
            full(sb_main), full(sb_lora), full(ww), full(wa), full(wg), full(vecs),
            full(e), full(et),
        ],
        out_specs=[out_spec] * 7,
        out_shape=[jax.ShapeDtypeStruct((m, d), F32)] * 7,
        compiler_params=_cparams(("parallel", "arbitrary")),
        name="rwkv_prep",
    )(p_main, p_main, p_main, p_lora, p_main, p_main, p_main, p_lora,
      sb_main, sb_lora, ww, wa, wg, vecs, e, et)
    return outs


def _scan_kernel(r_ref, k_ref, v_ref, lw_ref, kk_ref, aa_ref, o_ref, s_scr, *, chunk, heads, hd):
    c_len = chunk
    seq = r_ref.shape[0]
    row = lax.broadcasted_iota(jnp.int32, (c_len, c_len), 0)
    col = lax.broadcasted_iota(jnp.int32, (c_len, c_len), 1)
    incl = col <= row
    strict = col < row
    ltri = jnp.where(incl, 1.0, 0.0).astype(BF16)
    eye = jnp.where(row == col, 1.0, 0.0).astype(F32)
    n_double = int(math.log2(c_len)) - 1
    s_scr[...] = jnp.zeros_like(s_scr)

    def body(c, carry):
        rows = pl.ds(pl.multiple_of(c * c_len, c_len), c_len)
        r = r_ref[rows, :]
        k = k_ref[rows, :]
        v = v_ref[rows, :]
        lw = lw_ref[rows, :]
        kk = kk_ref[rows, :]
        b = kk * aa_ref[rows, :]
        lh, lm, ll = _split3(lw)
        cum = _dot(ltri, lh) + (_dot(ltri, lm) + _dot(ltri, ll))
        cum_last = cum[c_len - 1:c_len, :]
        g_inv = jnp.exp(-cum)
        g_end = jnp.exp(cum_last - cum)
        rq = r * jnp.exp(cum)
        aq = -(kk * jnp.exp(cum - lw))
        kq = k * g_inv
        bq = b * g_inv
        kend = k * g_end
        bend = b * g_end
        g_tot = jnp.exp(cum_last)
        for j in range(heads):
            sl = slice(j * hd, (j + 1) * hd)
            ar = _split2(jnp.concatenate([aq[:, sl], rq[:, sl]], axis=0))
            pb = _dot_hp(ar, _split2(bq[:, sl]), _NT)
            pk = _dot_hp(ar, _split2(kq[:, sl]), _NT)
            a_ab = jnp.where(strict, pb[:c_len], 0.0)
            a_rb = jnp.where(incl, pb[c_len:], 0.0)
            a_ak = jnp.where(strict, pk[:c_len], 0.0)
            a_rk = jnp.where(incl, pk[c_len:], 0.0)
            s0 = s_scr[j]
            ar_s = _dot_hp(ar, _split2(s0), _NT)
            vj = _split2(v[:, sl])
            rhs = ar_s[:c_len] + _dot_hp(_split2(a_ak), vj)
            x = eye + a_ab
            a2 = _split2(a_ab)
            p = _dot_hp(a2, a2)
            for _ in range(n_double - 1):
                xp = _dot_hp(_split2(jnp.concatenate([x, p], axis=0)), _split2(p))
                x = x + xp[:c_len]
                p = xp[c_len:]
            x = x + _dot_hp(_split2(x), _split2(p))
            sa = _dot_hp(_split2(x), _split2(rhs))
            sa2 = _split2(sa)
            o_ref[rows, sl] = ar_s[c_len:] + _dot_hp(_split2(a_rb), sa2) + _dot_hp(_split2(a_rk), vj)
            uv = _split2(jnp.concatenate([sa, v[:, sl]], axis=0))
            bk = _split2(jnp.concatenate([bend[:, sl], kend[:, sl]], axis=0))
            s_scr[j] = s0 * g_tot[:, sl] + _dot_hp(uv, bk, _TN)
        return carry

    lax.fori_loop(0, seq // c_len, body, 0)


def _rwkv_scan(r, k, v, lw, kk, aa, batch, seq, d):
    wb = SCAN_HEADS * RWKV_HEAD
    spec = pl.BlockSpec((None, seq, wb), lambda b, h: (b, 0, h))
    args = [a.reshape(batch, seq, d) for a in (r, k, v, lw, kk, aa)]
    out = pl.pallas_call(
        functools.partial(_scan_kernel, chunk=SCAN_CHUNK, heads=SCAN_HEADS, hd=RWKV_HEAD),
        grid=(batch, d // wb),
        in_specs=[spec] * 6,
        out_specs=spec,
        out_shape=jax.ShapeDtypeStruct((batch, seq, d), F32),
        scratch_shapes=[pltpu.VMEM((SCAN_HEADS, RWKV_HEAD, RWKV_HEAD), F32)],
        compiler_params=_cparams(("parallel", "parallel")),
        name="rwkv_scan",
    )(*args)
    return out.reshape(batch * seq, d)


def _rwkv_post_kernel(o_ref, r_ref, k_ref, v_ref, g_ref, gb_ref, ya_ref, x_ref,
                      vec_ref, e_ref, et_ref, wpb_ref, wout_ref, h_ref, *, hd):
    lnx_w = vec_ref[0:1, :]
    lnx_b = vec_ref[1:2, :]
    r_k = vec_ref[2:3, :]
    o = o_ref[...]
    inv_n = 1.0 / hd
    mu = _head_sum(o, e_ref, et_ref) * inv_n
    oc = o - mu
    var = _head_sum(oc * oc, e_ref, et_ref) * inv_n
    on = (oc * lax.rsqrt(var + GN_EPS)) * lnx_w + lnx_b
    v = v_ref[...]
    bonus = _head_sum((r_ref[...] * k_ref[...]) * r_k, e_ref, et_ref) * v
    yb = _dot(((on + bonus) * g_ref[...]).astype(BF16), wpb_ref[...])
    mixed = ya_ref[...] + jax.nn.sigmoid(gb_ref[...]) * yb
    h_ref[...] = x_ref[...] + _dot(mixed.astype(BF16), wout_ref[...])


def _rwkv_post(o, r, k, v, g, p_main, ya, x2, vecs, e, et, wpb16, wout16, tm, d):
    m = x2.shape[0]
    tile = pl.BlockSpec((tm, d), lambda i: (i, 0))
    full = lambda a: pl.BlockSpec(a.shape, lambda i: (0,) * a.ndim)
    return pl.pallas_call(
        functools.partial(_rwkv_post_kernel, hd=RWKV_HEAD),
        grid=(m // tm,),
        in_specs=[tile, tile, tile, tile, tile,
                  pl.BlockSpec((tm, d), lambda i: (i, 6)),
                  tile, tile, full(vecs), full(e), full(et), full(wpb16), full(wout16)],
        out_specs=tile,
        out_shape=jax.ShapeDtypeStruct((m, d), F32),
        compiler_params=_cparams(("parallel",)),
        name="rwkv_post",
    )(o, r, k, v, g, p_main, ya, x2, vecs, e, et, wpb16, wout16)


def _ffn_kernel(h_ref, g_ref, w1_ref, w2_ref, gf_ref, o_ref, f_scr, acc_scr):
    j = pl.program_id(1)

    @pl.when(j == 0)
    def _():
        f_scr[...] = _rms_rows(h_ref[...], g_ref[...]).astype(BF16)
        acc_scr[...] = jnp.zeros_like(acc_scr)

    t = jnp.maximum(_dot(f_scr[...], w1_ref[...]), 0.0)
    acc_scr[...] += _dot((t * t).astype(BF16), w2_ref[...])

    @pl.when(j == pl.num_programs(1) - 1)
    def _():
        o_ref[...] = _rms_rows(h_ref[...] + acc_scr[...], gf_ref[...])


def _ffn(h, g, w1_16, w2_16, g_final, tm, tf):
    m, d = h.shape
    dff = w1_16.shape[1]
    return pl.pallas_call(
        _ffn_kernel,
        grid=(m // tm, dff // tf),
        in_specs=[
            pl.BlockSpec((tm, d), lambda i, j: (i, 0)),
            pl.BlockSpec((1, d), lambda i, j: (0, 0)),
            pl.BlockSpec((d, tf), lambda i, j: (0, j)),
            pl.BlockSpec((tf, d), lambda i, j: (j, 0)),
            pl.BlockSpec((1, d), lambda i, j: (0, 0)),
        ],
        out_specs=pl.BlockSpec((tm, d), lambda i, j: (i, 0)),
        out_shape=jax.ShapeDtypeStruct((m, d), F32),
        scratch_shapes=[pltpu.VMEM((tm, d), BF16), pltpu.VMEM((tm, d), F32)],
        compiler_params=_cparams(("parallel", "arbitrary")),
        name="ffn",
    )(h, g, w1_16, w2_16, g_final)


def _pad_cols(a, n):
    return jnp.pad(a, ((0, 0), (0, n - a.shape[1])))


def _pad_rows(a, n):
    return jnp.pad(a, ((0, n - a.shape[0]), (0, 0)))


def _layer(x2, batch, seq, g_mix, w_in, sgu_ln_w, sgu_ln_b, sgu_w, sgu_b, w_proj_a, shift_b,
           w_lora_w, w0, a_lora_w, a0, g_lora_w, k_k, k_a, r_k, ln_x_w, ln_x_b, w_proj_b,
           w_out, g_ffn, w_ffn1, w_ffn2, g_out):
    d = x2.shape[1]
    lora_w, lora_a, lora_g = w_lora_w.shape[0], a_lora_w.shape[0], g_lora_w.shape[0]
    c_sgu = 2 * d
    c_rkv = 3 * d
    c_lora = lora_w + lora_a + lora_g
    o_lora = c_sgu + c_rkv
    o_gate = o_lora + c_lora
    pw, pa = LANES, LANES
    pg = -(-lora_g // LANES) * LANES

    w_main = jnp.concatenate([w_in[:, :o_lora], w_in[:, o_gate:]], axis=1).astype(BF16)
    w_lo = jnp.concatenate([
        _pad_cols(w_in[:, o_lora:o_lora + lora_w], pw),
        _pad_cols(w_in[:, o_lora + lora_w:o_lora + lora_w + lora_a], pa),
        _pad_cols(w_in[:, o_lora + lora_w + lora_a:o_gate], pg)], axis=1).astype(BF16)
    sb_rkv = shift_b[:, :c_rkv]
    sb_lo = shift_b[:, c_rkv:]
    sb_lora = jnp.concatenate([
        _pad_cols(sb_lo[:, :lora_w], pw),
        _pad_cols(sb_lo[:, lora_w:lora_w + lora_a], pa),
        _pad_cols(sb_lo[:, lora_w + lora_a:], pg)], axis=1)

    g_mix2 = g_mix.reshape(1, d)
    p_main = _in_proj(x2, g_mix2, w_main, tm=1024, tn=512)
    p_lora = _in_proj(x2, g_mix2, w_lo, tm=1024, tn=w_lo.shape[1])

    bias_full = jnp.repeat(sgu_b.T, d // SGU_GROUPS, axis=1)
    ya = _sgu(p_main, sgu_ln_w.reshape(1, d), sgu_ln_b.reshape(1, d), sgu_w, bias_full,
              w_proj_a.astype(BF16), ts=256, d=d)

    e = (lax.broadcasted_iota(jnp.int32, (d, LANES), 0) // RWKV_HEAD
         == lax.broadcasted_iota(jnp.int32, (d, LANES), 1)).astype(BF16)
    et = e.T
    vec_prep = jnp.stack([w0, a0, k_k, k_a])
    r, k, v, lw, kk, aa, g = _rwkv_prep(
        p_main, p_lora, sb_rkv, sb_lora, _pad_rows(w_lora_w, pw), _pad_rows(a_lora_w, pa),
        _pad_rows(g_lora_w, pg), vec_prep, e, et, batch, seq, d, tt=256)

    o = _rwkv_scan(r, k, v, lw, kk, aa, batch, seq, d)

    vec_post = jnp.stack([ln_x_w, ln_x_b, r_k])
    h1 = _rwkv_post(o, r, k, v, g, p_main, ya, x2, vec_post, e, et, w_proj_b.astype(BF16),
                    w_out.astype(BF16), tm=512, d=d)
    return _ffn(h1, g_ffn.reshape(1, d), w_ffn1.astype(BF16), w_ffn2.astype(BF16),
                g_out.reshape(1, d), tm=1024, tf=512)


def kernel(x, g_mix, w_in, sgu_ln_w, sgu_ln_b, sgu_w, sgu_b, w_proj_a, shift_b, w_lora_w, w0,
           a_lora_w, a0, g_lora_w, k_k, k_a, r_k, ln_x_w, ln_x_b, w_proj_b, w_out, g_ffn,
           w_ffn1, w_ffn2, g_final):
    batch, seq, d = x.shape
    depth = w_in.shape[0]
    assert depth == 1, "the final RMSNorm is fused into the single layer's ffn call"
    h = x.reshape(batch * seq, d)
    l = 0
    h = _layer(h, batch, seq, g_mix[l], w_in[l], sgu_ln_w[l], sgu_ln_b[l], sgu_w[l], sgu_b[l],
               w_proj_a[l], shift_b[l], w_lora_w[l], w0[l], a_lora_w[l], a0[l], g_lora_w[l],
               k_k[l], k_a[l], r_k[l], ln_x_w[l], ln_x_b[l], w_proj_b[l], w_out[l], g_ffn[l],
               w_ffn1[l], w_ffn2[l], g_final)
    return h.reshape(batch, seq, d)
```

```python
import functools
import math

import jax
import jax.numpy as jnp
from jax import lax
from jax.experimental import pallas as pl
from jax.experimental.pallas import tpu as pltpu

F32 = jnp.float32
BF16 = jnp.bfloat16

SGU_CHUNK = 128
SGU_GROUPS = 8
RWKV_HEAD = 64
NORM_EPS = 1e-6
LN_EPS = 1e-5
GN_EPS = 64e-5

LANES = 128
SCAN_CHUNK = 64
SCAN_GROUP_HEADS = 4
VMEM_LIMIT = 56 * 1024 * 1024


def _cparams(sem):
    return pltpu.CompilerParams(dimension_semantics=sem, vmem_limit_bytes=VMEM_LIMIT)


def _dot(a, b, dims=(((1,), (0,)), ((), ()))):
    return lax.dot_general(a, b, dims, preferred_element_type=F32)


_NT = (((1,), (1,)), ((), ()))
_TN = (((0,), (0,)), ((), ()))


def _split2(x):
    hi = x.astype(BF16)
    lo = (x - hi.astype(F32)).astype(BF16)
    return hi, lo


def _split3(x):
    hi = x.astype(BF16)
    r1 = x - hi.astype(F32)
    mid = r1.astype(BF16)
    lo = (r1 - mid.astype(F32)).astype(BF16)
    return hi, mid, lo


def _dot_hp(a2, b2, dims=(((1,), (0,)), ((), ()))):
    ah, al = a2
    bh, bl = b2
    return _dot(ah, bh, dims) + (_dot(ah, bl, dims) + _dot(al, bh, dims))


def _dot_x2(x, w_exact):
    hi, lo = _split2(x)
    return _dot(hi, w_exact) + _dot(lo, w_exact)


def _head_sum(x, e_ref, et_ref):
    return _dot_x2(_dot_x2(x, e_ref[...]), et_ref[...])


def _rms_rows(x, g):
    ms = jnp.mean(x * x, axis=-1, keepdims=True)
    return (x * lax.rsqrt(ms + NORM_EPS)) * g


def _in_proj_kernel(x_ref, g_ref, w_ref, o_ref, a_scr, *, row_chunk):
    @pl.when(pl.program_id(1) == 0)
    def _():
        def body(i, c):
            rows = pl.ds(pl.multiple_of(i * row_chunk, row_chunk), row_chunk)
            a_scr[rows, :] = _rms_rows(x_ref[rows, :], g_ref[...]).astype(BF16)
            return c
        lax.fori_loop(0, x_ref.shape[0] // row_chunk, body, 0)

    o_ref[...] = _dot(a_scr[...], w_ref[...])


def _in_proj(x2, g, w16, tm, tn):
    m, d = x2.shape
    n = w16.shape[1]
    return pl.pallas_call(
        functools.partial(_in_proj_kernel, row_chunk=256),
        grid=(m // tm, n // tn),
        in_specs=[
            pl.BlockSpec((tm, d), lambda i, j: (i, 0)),
            pl.BlockSpec((1, d), lambda i, j: (0, 0)),
            pl.BlockSpec((d, tn), lambda i, j: (0, j)),
        ],
        out_specs=pl.BlockSpec((tm, tn), lambda i, j: (i, j)),
        out_shape=jax.ShapeDtypeStruct((m, n), F32),
        scratch_shapes=[pltpu.VMEM((tm, d), BF16)],
        compiler_params=_cparams(("parallel", "arbitrary")),
        name="in_proj",
    )(x2, g, w16)


def _gelu(x):
    return 0.5 * x * (1.0 + lax.erf(x * (1.0 / math.sqrt(2.0))))


def _sgu_kernel(u_ref, v_ref, ga_ref, lnw_ref, lnb_ref, ws_ref, bias_ref, wpa_ref, o_ref,
                wm_scr, s_scr):
    ch = SGU_CHUNK
    dg = LANES

    @pl.when(pl.program_id(0) == 0)
    def _():
        row = lax.broadcasted_iota(jnp.int32, (ch, ch), 0)
        col = lax.broadcasted_iota(jnp.int32, (ch, ch), 1)
        for g in range(SGU_GROUPS):
            wm_scr[g] = jnp.where(col <= row, ws_ref[g], 0.0).astype(BF16)

    for c in range(u_ref.shape[0] // ch):
        rows = pl.ds(c * ch, ch)
        zu = _gelu(u_ref[rows, :])
        zv = _gelu(v_ref[rows, :])
        mu = jnp.mean(zv, axis=-1, keepdims=True)
        zc = zv - mu
        var = jnp.mean(zc * zc, axis=-1, keepdims=True)
        vn = ((zc * lax.rsqrt(var + LN_EPS)) * lnw_ref[...] + lnb_ref[...]).astype(BF16)
        for g in range(SGU_GROUPS):
            cols = slice(g * dg, (g + 1) * dg)
            sv = _dot(wm_scr[g], vn[:, cols]) + bias_ref[:, cols]
            s_scr[rows, cols] = (zu[:, cols] * sv).astype(BF16)

    ya = _dot(s_scr[...], wpa_ref[...])
    o_ref[...] = jax.nn.sigmoid(ga_ref[...]) * ya


def _sgu(p_main, ln_w, ln_b, sgu_w, bias_full, wpa16, ts, d):
    m = p_main.shape[0]
    ch = SGU_CHUNK
    return pl.pallas_call(
        _sgu_kernel,
        grid=(m // ts,),
        in_specs=[
            pl.BlockSpec((ts, d), lambda i: (i, 0)),
            pl.BlockSpec((ts, d), lambda i: (i, 1)),
            pl.BlockSpec((ts, d), lambda i: (i, 5)),
            pl.BlockSpec((1, d), lambda i: (0, 0)),
            pl.BlockSpec((1, d), lambda i: (0, 0)),
            pl.BlockSpec((SGU_GROUPS, ch, ch), lambda i: (0, 0, 0)),
            pl.BlockSpec((ch, d), lambda i: (0, 0)),
            pl.BlockSpec((d, d), lambda i: (0, 0)),
        ],
        out_specs=pl.BlockSpec((ts, d), lambda i: (i, 0)),
        out_shape=jax.ShapeDtypeStruct((m, d), F32),
        scratch_shapes=[pltpu.VMEM((SGU_GROUPS, ch, ch), BF16), pltpu.VMEM((ts, d), BF16)],
        compiler_params=_cparams(("arbitrary",)),
        name="sgu",
    )(p_main, p_main, p_main, ln_w, ln_b, sgu_w, bias_full, wpa16)


def _shift_mix(cur_ref, prev_ref, sb0, sb1, first):
    p = cur_ref[...]
    tail = prev_ref[...]
    prev_row = jnp.where(first, 0.0, tail[tail.shape[0] - 1:, :])
    rolled = pltpu.roll(p, 1, 0)
    row = lax.broadcasted_iota(jnp.int32, p.shape, 0)
    shifted = jnp.where(row == 0, prev_row, rolled)
    return p * sb0 + shifted * sb1


def _softplus(z):
    return jnp.maximum(z, 0.0) + jnp.log1p(jnp.exp(-jnp.abs(z)))


def _rwkv_prep_kernel(pr_ref, pk_ref, pv_ref, pl_ref, qr_ref, qk_ref, qv_ref, ql_ref,
                      sb_ref, sbl_ref, ww_ref, wa_ref, wg_ref, vec_ref, e_ref, et_ref,
                      r_out, k_out, v_out, lw_out, kk_out, aa_out, g_out):
    first = pl.program_id(1) == 0
    d = r_out.shape[1]
    r = _shift_mix(pr_ref, qr_ref, sb_ref[0:1, 0:d], sb_ref[1:2, 0:d], first)
    k = _shift_mix(pk_ref, qk_ref, sb_ref[0:1, d:2 * d], sb_ref[1:2, d:2 * d], first)
    v = _shift_mix(pv_ref, qv_ref, sb_ref[0:1, 2 * d:3 * d], sb_ref[1:2, 2 * d:3 * d], first)
    lo = _shift_mix(pl_ref, ql_ref, sbl_ref[0:1, :], sbl_ref[1:2, :], first)
    xw = lo[:, 0:LANES]
    xa = lo[:, LANES:2 * LANES]
    xg = lo[:, 2 * LANES:]
    w0 = vec_ref[0:1, :]
    a0 = vec_ref[1:2, :]
    k_k = vec_ref[2:3, :]
    k_a = vec_ref[3:4, :]

    zw = w0 + _dot_hp(_split2(jnp.tanh(xw)), _split2(ww_ref[...]))
    wlog = -_softplus(-zw) - 0.5
    lw_out[...] = -jnp.exp(wlog)
    aa = jax.nn.sigmoid(a0 + _dot_hp(_split2(xa), _split2(wa_ref[...])))
    g_out[...] = _dot_hp(_split2(jax.nn.sigmoid(xg)), _split2(wg_ref[...]))
    kraw = k * k_k
    ss = _head_sum(kraw * kraw, e_ref, et_ref)
    kk_out[...] = kraw / jnp.maximum(jnp.sqrt(ss), 1e-12)
    k_out[...] = k * (1.0 + (aa - 1.0) * k_a)
    r_out[...] = r
    v_out[...] = v
    aa_out[...] = aa


def _rwkv_prep(p_main, p_lora, sb_main, sb_lora, ww, wa, wg, vecs, e, et, batch, seq, d, tt):
    m = batch * seq
    nt = seq // tt
    tail = 8
    lw = p_lora.shape[1]

    def cur(cb):
        return pl.BlockSpec((tt, d), lambda b, t: (b * nt + t, cb))

    def prev(cb):
        return pl.BlockSpec(
            (tail, d), lambda b, t: (jnp.maximum((b * nt + t) * (tt // tail) - 1, 0), cb))

    full = lambda a: pl.BlockSpec(a.shape, lambda b, t: (0,) * a.ndim)
    out_spec = pl.BlockSpec((tt, d), lambda b, t: (b * nt + t, 0))
    outs = pl.pallas_call(
        _rwkv_prep_kernel,
        grid=(batch, nt),
        in_specs=[
            cur(2), cur(3), cur(4),
            pl.BlockSpec((tt, lw), lambda b, t: (b * nt + t, 0)),
            prev(2), prev(3), prev(4),
            pl.BlockSpec((tail, lw),
                         lambda b, t: (jnp.maximum((b * nt + t) * (tt // tail) - 1, 0), 0)),
            full(sb_main), full(sb_lora), full(ww), full(wa), full(wg), full(vecs),
            full(e), full(et),
        ],
        out_specs=[out_spec] * 7,
        out_shape=[jax.ShapeDtypeStruct((m, d), F32)] * 7,
        compiler_params=_cparams(("parallel", "arbitrary")),
        name="rwkv_prep",
    )(p_main, p_main, p_main, p_lora, p_main, p_main, p_main, p_lora,
      sb_main, sb_lora, ww, wa, wg, vecs, e, et)
    return outs


def _scan_kernel(r_ref, k_ref, v_ref, lw_ref, kk_ref, aa_ref, o_ref, s_scr, *, chunk, hd, gw):
    c_len = chunk
    assert c_len == hd
    tblk, d = r_ref.shape
    n_groups = d // gw
    hpg = gw // hd
    n_double = int(math.log2(c_len)) - 1

    rowi = lax.broadcasted_iota(jnp.int32, (c_len, gw), 0)
    sub = lax.broadcasted_iota(jnp.int32, (c_len, gw), 1) % c_len
    strict = sub < rowi
    incl = sub <= rowi
    eye = jnp.where(sub == rowi, 1.0, 0.0).astype(F32)
    bd_mask = (lax.broadcasted_iota(jnp.int32, (gw, gw), 0) // c_len
               == lax.broadcasted_iota(jnp.int32, (gw, gw), 1) // hd)
    ltri = jnp.where(lax.broadcasted_iota(jnp.int32, (c_len, c_len), 1)
                     <= lax.broadcasted_iota(jnp.int32, (c_len, c_len), 0), 1.0, 0.0).astype(BF16)

    @pl.when(pl.program_id(1) == 0)
    def _():
        s_scr[...] = jnp.zeros_like(s_scr)

    def bd(x16):
        t = jnp.concatenate([x16] * hpg, axis=0)
        return jnp.where(bd_mask, t, jnp.zeros_like(t))

    def bd2(x2):
        return bd(x2[0]), bd(x2[1])

    def cat2(a2, b2, axis):
        return (jnp.concatenate([a2[0], b2[0]], axis=axis),
                jnp.concatenate([a2[1], b2[1]], axis=axis))

    groups = range(n_groups)

    def body(c, carry):
        rows = pl.ds(pl.multiple_of(c * c_len, c_len), c_len)

        def load(ref, g):
            return ref[rows, g * gw:(g + 1) * gw]

        ar, bq, kq, v2, bk, g_tot = [], [], [], [], [], []
        for g in groups:
            k = load(k_ref, g)
            lw = load(lw_ref, g)
            kk = load(kk_ref, g)
            b = kk * load(aa_ref, g)
            lh, lm, ll = _split3(lw)
            cum = _dot(ltri, lh) + (_dot(ltri, lm) + _dot(ltri, ll))
            cum_last = cum[c_len - 1:c_len, :]
            g_inv = jnp.exp(-cum)
            g_end = jnp.exp(cum_last - cum)
            aq = -(kk * jnp.exp(cum - lw))
            rq = load(r_ref, g) * jnp.exp(cum)
            ar.append(_split2(jnp.concatenate([aq, rq], axis=0)))
            bq.append(_split2(b * g_inv))
            kq.append(_split2(k * g_inv))
            v2.append(_split2(load(v_ref, g)))
            bk.append(_split2(jnp.concatenate([b * g_end, k * g_end], axis=0)))
            g_tot.append(jnp.exp(cum_last))

        pb = [_dot_hp(ar[g], bd2(bq[g]), _NT) for g in groups]
        pk = [_dot_hp(ar[g], bd2(kq[g]), _NT) for g in groups]
        s0 = [s_scr[g] for g in groups]
        ar_s = [_dot_hp(ar[g], _split2(s0[g]), _NT) for g in groups]
        a_ab = [_split2(jnp.where(strict, pb[g][:c_len], 0.0)) for g in groups]
        a_ak = [_split2(jnp.where(strict, pk[g][:c_len], 0.0)) for g in groups]
        a_r = [_split2(jnp.concatenate([jnp.where(incl, pb[g][c_len:], 0.0),
                                        jnp.where(incl, pk[g][c_len:], 0.0)], axis=1))
               for g in groups]
        v_bd = [bd2(v2[g]) for g in groups]
        rhs = [ar_s[g][:c_len] + _dot_hp(a_ak[g], v_bd[g]) for g in groups]

        p = [_dot_hp(a_ab[g], bd2(a_ab[g])) for g in groups]
        x = [eye + jnp.where(strict, pb[g][:c_len], 0.0) for g in groups]
        for _ in range(n_double - 1):
            xp = [_dot_hp(_split2(jnp.concatenate([x[g], p[g]], axis=0)), bd2(_split2(p[g])))
                  for g in groups]
            x = [x[g] + xp[g][:c_len] for g in groups]
            p = [xp[g][c_len:] for g in groups]
        x = [x[g] + _dot_hp(_split2(x[g]), bd2(_split2(p[g]))) for g in groups]

        sa = [_dot_hp(_split2(x[g]), bd2(_split2(rhs[g]))) for g in groups]
        sa2 = [_split2(sa[g]) for g in groups]
        for g in groups:
            o_ref[rows, g * gw:(g + 1) * gw] = ar_s[g][c_len:] + _dot_hp(
                a_r[g], cat2(bd2(sa2[g]), v_bd[g], 0))
        for g in groups:
            upd = _dot_hp(cat2(sa2[g], v2[g], 0), bk[g], _TN)
            s_scr[g] = jnp.where(bd_mask, s0[g] * g_tot[g] + upd, 0.0)
        return carry

    lax.fori_loop(0, tblk // c_len, body, 0)


def _rwkv_scan(r, k, v, lw, kk, aa, batch, seq, d, tblk):
    gw = SCAN_GROUP_HEADS * RWKV_HEAD
    nt = seq // tblk
    spec = pl.BlockSpec((tblk, d), lambda b, t: (b * nt + t, 0))
    return pl.pallas_call(
        functools.partial(_scan_kernel, chunk=SCAN_CHUNK, hd=RWKV_HEAD, gw=gw),
        grid=(batch, nt),
        in_specs=[spec] * 6,
        out_specs=spec,
        out_shape=jax.ShapeDtypeStruct((batch * seq, d), F32),
        scratch_shapes=[pltpu.VMEM((d // gw, gw, gw), F32)],
        compiler_params=_cparams(("parallel", "arbitrary")),
        name="rwkv_scan",
    )(r, k, v, lw, kk, aa)


def _rwkv_post_kernel(o_ref, r_ref, k_ref, v_ref, g_ref, gb_ref, ya_ref, x_ref,
                      vec_ref, e_ref, et_ref, wpb_ref, wout_ref, h_ref, *, hd):
    lnx_w = vec_ref[0:1, :]
    lnx_b = vec_ref[1:2, :]
    r_k = vec_ref[2:3, :]
    o = o_ref[...]
    inv_n = 1.0 / hd
    mu = _head_sum(o, e_ref, et_ref) * inv_n
    oc = o - mu
    var = _head_sum(oc * oc, e_ref, et_ref) * inv_n
    on = (oc * lax.rsqrt(var + GN_EPS)) * lnx_w + lnx_b
    v = v_ref[...]
    bonus = _head_sum((r_ref[...] * k_ref[...]) * r_k, e_ref, et_ref) * v
    yb = _dot(((on + bonus) * g_ref[...]).astype(BF16), wpb_ref[...])
    mixed = ya_ref[...] + jax.nn.sigmoid(gb_ref[...]) * yb
    h_ref[...] = x_ref[...] + _dot(mixed.astype(BF16), wout_ref[...])


def _rwkv_post(o, r, k, v, g, p_main, ya, x2, vecs, e, et, wpb16, wout16, tm, d):
    m = x2.shape[0]
    tile = pl.BlockSpec((tm, d), lambda i: (i, 0))
    full = lambda a: pl.BlockSpec(a.shape, lambda i: (0,) * a.ndim)
    return pl.pallas_call(
        functools.partial(_rwkv_post_kernel, hd=RWKV_HEAD),
        grid=(m // tm,),
        in_specs=[tile, tile, tile, tile, tile,
                  pl.BlockSpec((tm, d), lambda i: (i, 6)),
                  tile, tile, full(vecs), full(e), full(et), full(wpb16), full(wout16)],
        out_specs=tile,
        out_shape=jax.ShapeDtypeStruct((m, d), F32),
        compiler_params=_cparams(("parallel",)),
        name="rwkv_post",
    )(o, r, k, v, g, p_main, ya, x2, vecs, e, et, wpb16, wout16)


def _ffn_kernel(h_ref, g_ref, w1_ref, w2_ref, gf_ref, o_ref, f_scr, acc_scr):
    j = pl.program_id(1)

    @pl.when(j == 0)
    def _():
        f_scr[...] = _rms_rows(h_ref[...], g_ref[...]).astype(BF16)
        acc_scr[...] = jnp.zeros_like(acc_scr)

    t = jnp.maximum(_dot(f_scr[...], w1_ref[...]), 0.0)
    acc_scr[...] += _dot((t * t).astype(BF16), w2_ref[...])

    @pl.when(j == pl.num_programs(1) - 1)
    def _():
        o_ref[...] = _rms_rows(h_ref[...] + acc_scr[...], gf_ref[...])


def _ffn(h, g, w1_16, w2_16, g_final, tm, tf):
    m, d = h.shape
    dff = w1_16.shape[1]
    return pl.pallas_call(
        _ffn_kernel,
        grid=(m // tm, dff // tf),
        in_specs=[
            pl.BlockSpec((tm, d), lambda i, j: (i, 0)),
            pl.BlockSpec((1, d), lambda i, j: (0, 0)),
            pl.BlockSpec((d, tf), lambda i, j: (0, j)),
            pl.BlockSpec((tf, d), lambda i, j: (j, 0)),
            pl.BlockSpec((1, d), lambda i, j: (0, 0)),
        ],
        out_specs=pl.BlockSpec((tm, d), lambda i, j: (i, 0)),
        out_shape=jax.ShapeDtypeStruct((m, d), F32),
        scratch_shapes=[pltpu.VMEM((tm, d), BF16), pltpu.VMEM((tm, d), F32)],
        compiler_params=_cparams(("parallel", "arbitrary")),
        name="ffn",
    )(h, g, w1_16, w2_16, g_final)


def _pad_cols(a, n):
    return jnp.pad(a, ((0, 0), (0, n - a.shape[1])))


def _pad_rows(a, n):
    return jnp.pad(a, ((0, n - a.shape[0]), (0, 0)))


def _layer(x2, batch, seq, g_mix, w_in, sgu_ln_w, sgu_ln_b, sgu_w, sgu_b, w_proj_a, shift_b,
           w_lora_w, w0, a_lora_w, a0, g_lora_w, k_k, k_a, r_k, ln_x_w, ln_x_b, w_proj_b,
           w_out, g_ffn, w_ffn1, w_ffn2, g_out):
    d = x2.shape[1]
    lora_w, lora_a, lora_g = w_lora_w.shape[0], a_lora_w.shape[0], g_lora_w.shape[0]
    c_sgu = 2 * d
    c_rkv = 3 * d
    c_lora = lora_w + lora_a + lora_g
    o_lora = c_sgu + c_rkv
    o_gate = o_lora + c_lora
    pw, pa = LANES, LANES
    pg = -(-lora_g // LANES) * LANES

    w_main = jnp.concatenate([w_in[:, :o_lora], w_in[:, o_gate:]], axis=1).astype(BF16)
    w_lo = jnp.concatenate([
        _pad_cols(w_in[:, o_lora:o_lora + lora_w], pw),
        _pad_cols(w_in[:, o_lora + lora_w:o_lora + lora_w + lora_a], pa),
        _pad_cols(w_in[:, o_lora + lora_w + lora_a:o_gate], pg)], axis=1).astype(BF16)
    sb_rkv = shift_b[:, :c_rkv]
    sb_lo = shift_b[:, c_rkv:]
    sb_lora = jnp.concatenate([
        _pad_cols(sb_lo[:, :lora_w], pw),
        _pad_cols(sb_lo[:, lora_w:lora_w + lora_a], pa),
        _pad_cols(sb_lo[:, lora_w + lora_a:], pg)], axis=1)

    g_mix2 = g_mix.reshape(1, d)
    p_main = _in_proj(x2, g_mix2, w_main, tm=1024, tn=512)
    p_lora = _in_proj(x2, g_mix2, w_lo, tm=1024, tn=w_lo.shape[1])

    bias_full = jnp.repeat(sgu_b.T, d // SGU_GROUPS, axis=1)
    ya = _sgu(p_main, sgu_ln_w.reshape(1, d), sgu_ln_b.reshape(1, d), sgu_w, bias_full,
              w_proj_a.astype(BF16), ts=256, d=d)

    e = (lax.broadcasted_iota(jnp.int32, (d, LANES), 0) // RWKV_HEAD
         == lax.broadcasted_iota(jnp.int32, (d, LANES), 1)).astype(BF16)
    et = e.T
    vec_prep = jnp.stack([w0, a0, k_k, k_a])
    r, k, v, lw, kk, aa, g = _rwkv_prep(
        p_main, p_lora, sb_rkv, sb_lora, _pad_rows(w_lora_w, pw), _pad_rows(a_lora_w, pa),
        _pad_rows(g_lora_w, pg), vec_prep, e, et, batch, seq, d, tt=256)

    o = _rwkv_scan(r, k, v, lw, kk, aa, batch, seq, d, tblk=512)

    vec_post = jnp.stack([ln_x_w, ln_x_b, r_k])
    h1 = _rwkv_post(o, r, k, v, g, p_main, ya, x2, vec_post, e, et, w_proj_b.astype(BF16),
                    w_out.astype(BF16), tm=512, d=d)
    return _ffn(h1, g_ffn.reshape(1, d), w_ffn1.astype(BF16), w_ffn2.astype(BF16),
                g_out.reshape(1, d), tm=1024, tf=512)


def kernel(x, g_mix, w_in, sgu_ln_w, sgu_ln_b, sgu_w, sgu_b, w_proj_a, shift_b, w_lora_w, w0,
           a_lora_w, a0, g_lora_w, k_k, k_a, r_k, ln_x_w, ln_x_b, w_proj_b, w_out, g_ffn,
           w_ffn1, w_ffn2, g_final):
    batch, seq, d = x.shape
    depth = w_in.shape[0]
    assert depth == 1, "the final RMSNorm is fused into the single layer's ffn call"
    h = x.reshape(batch * seq, d)
    l = 0
    h = _layer(h, batch, seq, g_mix[l], w_in[l], sgu_ln_w[l], sgu_ln_b[l], sgu_w[l], sgu_b[l],
               w_proj_a[l], shift_b[l], w_lora_w[l], w0[l], a_lora_w[l], a0[l], g_lora_w[l],
               k_k[l], k_a[l], r_k[l], ln_x_w[l], ln_x_b[l], w_proj_b[l], w_out[l], g_ffn[l],
               w_ffn1[l], w_ffn2[l], g_final)
    return h.reshape(batch, seq, d)
```

```python
import functools
import math

import jax
import jax.numpy as jnp
from jax import lax
from jax.experimental import pallas as pl
from jax.experimental.pallas import tpu as pltpu

F32 = jnp.float32
BF16 = jnp.bfloat16

SGU_CHUNK = 128
SGU_GROUPS = 8
RWKV_HEAD = 64
NORM_EPS = 1e-6
LN_EPS = 1e-5
GN_EPS = 64e-5

LANES = 128
SCAN_CHUNK = 64
SCAN_GROUP_HEADS = 4
VMEM_LIMIT = 56 * 1024 * 1024
SCAN_HILO = dict(ar=False, bq=False, kq=False, v=False, bk=False, s0=False, a_ab=False,
                 a_ak=False, a_r=False, x=False, p=False, rhs=False, sa=False)


def _cparams(sem):
    return pltpu.CompilerParams(dimension_semantics=sem, vmem_limit_bytes=VMEM_LIMIT)


def _dot(a, b, dims=(((1,), (0,)), ((), ()))):
    return lax.dot_general(a, b, dims, preferred_element_type=F32)


_NT = (((1,), (1,)), ((), ()))
_TN = (((0,), (0,)), ((), ()))


def _split2(x):
    hi = x.astype(BF16)
    lo = (x - hi.astype(F32)).astype(BF16)
    return hi, lo


def _split3(x):
    hi = x.astype(BF16)
    r1 = x - hi.astype(F32)
    mid = r1.astype(BF16)
    lo = (r1 - mid.astype(F32)).astype(BF16)
    return hi, mid, lo


def _split1(x):
    return x.astype(BF16), None


def _dot_hp(a2, b2, dims=(((1,), (0,)), ((), ()))):
    ah, al = a2
    bh, bl = b2
    out = _dot(ah, bh, dims)
    if bl is not None:
        out = out + _dot(ah, bl, dims)
    if al is not None:
        out = out + _dot(al, bh, dims)
    return out


def _dot_x2(x, w_exact):
    hi, lo = _split2(x)
    return _dot(hi, w_exact) + _dot(lo, w_exact)


def _head_sum(x, e_ref, et_ref):
    return _dot_x2(_dot_x2(x, e_ref[...]), et_ref[...])


def _rms_rows(x, g):
    ms = jnp.mean(x * x, axis=-1, keepdims=True)
    return (x * lax.rsqrt(ms + NORM_EPS)) * g


def _in_proj_kernel(x_ref, g_ref, w_ref, o_ref, a_scr, *, row_chunk):
    @pl.when(pl.program_id(1) == 0)
    def _():
        def body(i, c):
            rows = pl.ds(pl.multiple_of(i * row_chunk, row_chunk), row_chunk)
            a_scr[rows, :] = _rms_rows(x_ref[rows, :], g_ref[...]).astype(BF16)
            return c
        lax.fori_loop(0, x_ref.shape[0] // row_chunk, body, 0)

    o_ref[...] = _dot(a_scr[...], w_ref[...])


def _in_proj(x2, g, w16, tm, tn):
    m, d = x2.shape
    n = w16.shape[1]
    return pl.pallas_call(
        functools.partial(_in_proj_kernel, row_chunk=256),
        grid=(m // tm, n // tn),
        in_specs=[
            pl.BlockSpec((tm, d), lambda i, j: (i, 0)),
            pl.BlockSpec((1, d), lambda i, j: (0, 0)),
            pl.BlockSpec((d, tn), lambda i, j: (0, j)),
        ],
        out_specs=pl.BlockSpec((tm, tn), lambda i, j: (i, j)),
        out_shape=jax.ShapeDtypeStruct((m, n), F32),
        scratch_shapes=[pltpu.VMEM((tm, d), BF16)],
        compiler_params=_cparams(("parallel", "arbitrary")),
        name="in_proj",
    )(x2, g, w16)


def _gelu(x):
    return 0.5 * x * (1.0 + lax.erf(x * (1.0 / math.sqrt(2.0))))


def _sgu_kernel(u_ref, v_ref, ga_ref, lnw_ref, lnb_ref, ws_ref, bias_ref, wpa_ref, o_ref,
                wm_scr, s_scr):
    ch = SGU_CHUNK
    dg = LANES

    @pl.when(pl.program_id(0) == 0)
    def _():
        row = lax.broadcasted_iota(jnp.int32, (ch, ch), 0)
        col = lax.broadcasted_iota(jnp.int32, (ch, ch), 1)
        for g in range(SGU_GROUPS):
            wm_scr[g] = jnp.where(col <= row, ws_ref[g], 0.0).astype(BF16)

    for c in range(u_ref.shape[0] // ch):
        rows = pl.ds(c * ch, ch)
        zu = _gelu(u_ref[rows, :])
        zv = _gelu(v_ref[rows, :])
        mu = jnp.mean(zv, axis=-1, keepdims=True)
        zc = zv - mu
        var = jnp.mean(zc * zc, axis=-1, keepdims=True)
        vn = ((zc * lax.rsqrt(var + LN_EPS)) * lnw_ref[...] + lnb_ref[...]).astype(BF16)
        for g in range(SGU_GROUPS):
            cols = slice(g * dg, (g + 1) * dg)
            sv = _dot(wm_scr[g], vn[:, cols]) + bias_ref[:, cols]
            s_scr[rows, cols] = (zu[:, cols] * sv).astype(BF16)

    ya = _dot(s_scr[...], wpa_ref[...])
    o_ref[...] = jax.nn.sigmoid(ga_ref[...]) * ya


def _sgu(p_main, ln_w, ln_b, sgu_w, bias_full, wpa16, ts, d):
    m = p_main.shape[0]
    ch = SGU_CHUNK
    return pl.pallas_call(
        _sgu_kernel,
        grid=(m // ts,),
        in_specs=[
            pl.BlockSpec((ts, d), lambda i: (i, 0)),
            pl.BlockSpec((ts, d), lambda i: (i, 1)),
            pl.BlockSpec((ts, d), lambda i: (i, 5)),
            pl.BlockSpec((1, d), lambda i: (0, 0)),
            pl.BlockSpec((1, d), lambda i: (0, 0)),
            pl.BlockSpec((SGU_GROUPS, ch, ch), lambda i: (0, 0, 0)),
            pl.BlockSpec((ch, d), lambda i: (0, 0)),
            pl.BlockSpec((d, d), lambda i: (0, 0)),
        ],
        out_specs=pl.BlockSpec((ts, d), lambda i: (i, 0)),
        out_shape=jax.ShapeDtypeStruct((m, d), F32),
        scratch_shapes=[pltpu.VMEM((SGU_GROUPS, ch, ch), BF16), pltpu.VMEM((ts, d), BF16)],
        compiler_params=_cparams(("arbitrary",)),
        name="sgu",
    )(p_main, p_main, p_main, ln_w, ln_b, sgu_w, bias_full, wpa16)


def _shift_mix(cur_ref, prev_ref, sb0, sb1, first):
    p = cur_ref[...]
    tail = prev_ref[...]
    prev_row = jnp.where(first, 0.0, tail[tail.shape[0] - 1:, :])
    rolled = pltpu.roll(p, 1, 0)
    row = lax.broadcasted_iota(jnp.int32, p.shape, 0)
    shifted = jnp.where(row == 0, prev_row, rolled)
    return p * sb0 + shifted * sb1


def _softplus(z):
    return jnp.maximum(z, 0.0) + jnp.log1p(jnp.exp(-jnp.abs(z)))


def _rwkv_prep_kernel(pr_ref, pk_ref, pv_ref, pl_ref, qr_ref, qk_ref, qv_ref, ql_ref,
                      sb_ref, sbl_ref, ww_ref, wa_ref, wg_ref, vec_ref, e_ref, et_ref,
                      r_out, k_out, v_out, lw_out, kk_out, aa_out, g_out):
    first = pl.program_id(1) == 0
    d = r_out.shape[1]
    r = _shift_mix(pr_ref, qr_ref, sb_ref[0:1, 0:d], sb_ref[1:2, 0:d], first)
    k = _shift_mix(pk_ref, qk_ref, sb_ref[0:1, d:2 * d], sb_ref[1:2, d:2 * d], first)
    v = _shift_mix(pv_ref, qv_ref, sb_ref[0:1, 2 * d:3 * d], sb_ref[1:2, 2 * d:3 * d], first)
    lo = _shift_mix(pl_ref, ql_ref, sbl_ref[0:1, :], sbl_ref[1:2, :], first)
    xw = lo[:, 0:LANES]
    xa = lo[:, LANES:2 * LANES]
    xg = lo[:, 2 * LANES:]
    w0 = vec_ref[0:1, :]
    a0 = vec_ref[1:2, :]
    k_k = vec_ref[2:3, :]
    k_a = vec_ref[3:4, :]

    zw = w0 + _dot_hp(_split2(jnp.tanh(xw)), _split2(ww_ref[...]))
    wlog = -_softplus(-zw) - 0.5
    lw_out[...] = -jnp.exp(wlog)
    aa = jax.nn.sigmoid(a0 + _dot_hp(_split2(xa), _split2(wa_ref[...])))
    g_out[...] = _dot_hp(_split2(jax.nn.sigmoid(xg)), _split2(wg_ref[...]))
    kraw = k * k_k
    ss = _head_sum(kraw * kraw, e_ref, et_ref)
    kk_out[...] = kraw / jnp.maximum(jnp.sqrt(ss), 1e-12)
    k_out[...] = k * (1.0 + (aa - 1.0) * k_a)
    r_out[...] = r
    v_out[...] = v
    aa_out[...] = aa


def _rwkv_prep(p_main, p_lora, sb_main, sb_lora, ww, wa, wg, vecs, e, et, batch, seq, d, tt):
    m = batch * seq
    nt = seq // tt
    tail = 8
    lw = p_lora.shape[1]

    def cur(cb):
        return pl.BlockSpec((tt, d), lambda b, t: (b * nt + t, cb))

    def prev(cb):
        return pl.BlockSpec(
            (tail, d), lambda b, t: (jnp.maximum((b * nt + t) * (tt // tail) - 1, 0), cb))

    full = lambda a: pl.BlockSpec(a.shape, lambda b, t: (0,) * a.ndim)
    out_spec = pl.BlockSpec((tt, d), lambda b, t: (b * nt + t, 0))
    outs = pl.pallas_call(
        _rwkv_prep_kernel,
        grid=(batch, nt),
        in_specs=[
            cur(2), cur(3), cur(4),
            pl.BlockSpec((tt, lw), lambda b, t: (b * nt + t, 0)),
            prev(2), prev(3), prev(4),
            pl.BlockSpec((tail, lw),
                         lambda b, t: (jnp.maximum((b * nt + t) * (tt // tail) - 1, 0), 0)),
            full(sb_main), full(sb_lora), full(ww), full(wa), full(wg), full(vecs),
            full(e), full(et),
        ],
        out_specs=[out_spec] * 7,
        out_shape=[jax.ShapeDtypeStruct((m, d), F32)] * 7,
        compiler_params=_cparams(("parallel", "arbitrary")),
        name="rwkv_prep",
    )(p_main, p_main, p_main, p_lora, p_main, p_main, p_main, p_lora,
      sb_main, sb_lora, ww, wa, wg, vecs, e, et)
    return outs


def _scan_kernel(r_ref, k_ref, v_ref, lw_ref, kk_ref, aa_ref, o_ref, s_scr, *, chunk, hd, gw):
    c_len = chunk
    assert c_len == hd
    tblk, d = r_ref.shape
    n_groups = d // gw
    hpg = gw // hd
    n_double = int(math.log2(c_len)) - 1

    rowi = lax.broadcasted_iota(jnp.int32, (c_len, gw), 0)
    sub = lax.broadcasted_iota(jnp.int32, (c_len, gw), 1) % c_len
    strict = sub < rowi
    incl = sub <= rowi
    eye = jnp.where(sub == rowi, 1.0, 0.0).astype(F32)
    bd_mask = (lax.broadcasted_iota(jnp.int32, (gw, gw), 0) // c_len
               == lax.broadcasted_iota(jnp.int32, (gw, gw), 1) // hd)
    ltri = jnp.where(lax.broadcasted_iota(jnp.int32, (c_len, c_len), 1)
                     <= lax.broadcasted_iota(jnp.int32, (c_len, c_len), 0), 1.0, 0.0).astype(BF16)

    @pl.when(pl.program_id(1) == 0)
    def _():
        s_scr[...] = jnp.zeros_like(s_scr)

    def bd(x16):
        t = jnp.concatenate([x16] * hpg, axis=0)
        return jnp.where(bd_mask, t, jnp.zeros_like(t))

    def bd2(x2):
        return bd(x2[0]), None if x2[1] is None else bd(x2[1])

    def cat2(a2, b2, axis):
        hi = jnp.concatenate([a2[0], b2[0]], axis=axis)
        if a2[1] is None and b2[1] is None:
            return hi, None
        lo = [jnp.zeros_like(t[0]) if t[1] is None else t[1] for t in (a2, b2)]
        return hi, jnp.concatenate(lo, axis=axis)

    def sp(name, x):
        return (_split2 if SCAN_HILO[name] else _split1)(x)

    groups = range(n_groups)

    def body(c, carry):
        rows = pl.ds(pl.multiple_of(c * c_len, c_len), c_len)

        def load(ref, g):
            return ref[rows, g * gw:(g + 1) * gw]

        ar, bq, kq, v2, bk, g_tot = [], [], [], [], [], []
        for g in groups:
            k = load(k_ref, g)
            lw = load(lw_ref, g)
            kk = load(kk_ref, g)
            b = kk * load(aa_ref, g)
            lh, lm, ll = _split3(lw)
            cum = _dot(ltri, lh) + (_dot(ltri, lm) + _dot(ltri, ll))
            cum_last = cum[c_len - 1:c_len, :]
            g_inv = jnp.exp(-cum)
            g_end = jnp.exp(cum_last - cum)
            aq = -(kk * jnp.exp(cum - lw))
            rq = load(r_ref, g) * jnp.exp(cum)
            ar.append(sp("ar", jnp.concatenate([aq, rq], axis=0)))
            bq.append(sp("bq", b * g_inv))
            kq.append(sp("kq", k * g_inv))
            v2.append(sp("v", load(v_ref, g)))
            bk.append(sp("bk", jnp.concatenate([b * g_end, k * g_end], axis=0)))
            g_tot.append(jnp.exp(cum_last))

        pb = [_dot_hp(ar[g], bd2(bq[g]), _NT) for g in groups]
        pk = [_dot_hp(ar[g], bd2(kq[g]), _NT) for g in groups]
        s0 = [s_scr[g] for g in groups]
        ar_s = [_dot_hp(ar[g], sp("s0", s0[g]), _NT) for g in groups]
        a_ab = [sp("a_ab", jnp.where(strict, pb[g][:c_len], 0.0)) for g in groups]
        a_ak = [sp("a_ak", jnp.where(strict, pk[g][:c_len], 0.0)) for g in groups]
        a_r = [sp("a_r", jnp.concatenate([jnp.where(incl, pb[g][c_len:], 0.0),
                                          jnp.where(incl, pk[g][c_len:], 0.0)], axis=1))
               for g in groups]
        v_bd = [bd2(v2[g]) for g in groups]
        rhs = [ar_s[g][:c_len] + _dot_hp(a_ak[g], v_bd[g]) for g in groups]

        p = [_dot_hp(a_ab[g], bd2(a_ab[g])) for g in groups]
        x = [eye + jnp.where(strict, pb[g][:c_len], 0.0) for g in groups]
        for _ in range(n_double - 1):
            xp = [_dot_hp(sp("x", jnp.concatenate([x[g], p[g]], axis=0)), bd2(sp("p", p[g])))
                  for g in groups]
            x = [x[g] + xp[g][:c_len] for g in groups]
            p = [xp[g][c_len:] for g in groups]
        x = [x[g] + _dot_hp(sp("x", x[g]), bd2(sp("p", p[g]))) for g in groups]

        sa = [_dot_hp(sp("x", x[g]), bd2(sp("rhs", rhs[g]))) for g in groups]
        sa2 = [sp("sa", sa[g]) for g in groups]
        for g in groups:
            o_ref[rows, g * gw:(g + 1) * gw] = ar_s[g][c_len:] + _dot_hp(
                a_r[g], cat2(bd2(sa2[g]), v_bd[g], 0))
        for g in groups:
            upd = _dot_hp(cat2(sa2[g], v2[g], 0), bk[g], _TN)
            s_scr[g] = jnp.where(bd_mask, s0[g] * g_tot[g] + upd, 0.0)
        return carry

    lax.fori_loop(0, tblk // c_len, body, 0)


def _rwkv_scan(r, k, v, lw, kk, aa, batch, seq, d, tblk):
    gw = SCAN_GROUP_HEADS * RWKV_HEAD
    nt = seq // tblk
    spec = pl.BlockSpec((tblk, d), lambda b, t: (b * nt + t, 0))
    return pl.pallas_call(
        functools.partial(_scan_kernel, chunk=SCAN_CHUNK, hd=RWKV_HEAD, gw=gw),
        grid=(batch, nt),
        in_specs=[spec] * 6,
        out_specs=spec,
        out_shape=jax.ShapeDtypeStruct((batch * seq, d), F32),
        scratch_shapes=[pltpu.VMEM((d // gw, gw, gw), F32)],
        compiler_params=_cparams(("parallel", "arbitrary")),
        name="rwkv_scan",
    )(r, k, v, lw, kk, aa)


def _rwkv_post_kernel(o_ref, r_ref, k_ref, v_ref, g_ref, gb_ref, ya_ref, x_ref,
                      vec_ref, e_ref, et_ref, wpb_ref, wout_ref, h_ref, *, hd):
    lnx_w = vec_ref[0:1, :]
    lnx_b = vec_ref[1:2, :]
    r_k = vec_ref[2:3, :]
    o = o_ref[...]
    inv_n = 1.0 / hd
    mu = _head_sum(o, e_ref, et_ref) * inv_n
    oc = o - mu
    var = _head_sum(oc * oc, e_ref, et_ref) * inv_n
    on = (oc * lax.rsqrt(var + GN_EPS)) * lnx_w + lnx_b
    v = v_ref[...]
    bonus = _head_sum((r_ref[...] * k_ref[...]) * r_k, e_ref, et_ref) * v
    yb = _dot(((on + bonus) * g_ref[...]).astype(BF16), wpb_ref[...])
    mixed = ya_ref[...] + jax.nn.sigmoid(gb_ref[...]) * yb
    h_ref[...] = x_ref[...] + _dot(mixed.astype(BF16), wout_ref[...])


def _rwkv_post(o, r, k, v, g, p_main, ya, x2, vecs, e, et, wpb16, wout16, tm, d):
    m = x2.shape[0]
    tile = pl.BlockSpec((tm, d), lambda i: (i, 0))
    full = lambda a: pl.BlockSpec(a.shape, lambda i: (0,) * a.ndim)
    return pl.pallas_call(
        functools.partial(_rwkv_post_kernel, hd=RWKV_HEAD),
        grid=(m // tm,),
        in_specs=[tile, tile, tile, tile, tile,
                  pl.BlockSpec((tm, d), lambda i: (i, 6)),
                  tile, tile, full(vecs), full(e), full(et), full(wpb16), full(wout16)],
        out_specs=tile,
        out_shape=jax.ShapeDtypeStruct((m, d), F32),
        compiler_params=_cparams(("parallel",)),
        name="rwkv_post",
    )(o, r, k, v, g, p_main, ya, x2, vecs, e, et, wpb16, wout16)


def _ffn_kernel(h_ref, g_ref, w1_ref, w2_ref, gf_ref, o_ref, f_scr, acc_scr):
    j = pl.program_id(1)

    @pl.when(j == 0)
    def _():
        f_scr[...] = _rms_rows(h_ref[...], g_ref[...]).astype(BF16)
        acc_scr[...] = jnp.zeros_like(acc_scr)

    t = jnp.maximum(_dot(f_scr[...], w1_ref[...]), 0.0)
    acc_scr[...] += _dot((t * t).astype(BF16), w2_ref[...])

    @pl.when(j == pl.num_programs(1) - 1)
    def _():
        o_ref[...] = _rms_rows(h_ref[...] + acc_scr[...], gf_ref[...])


def _ffn(h, g, w1_16, w2_16, g_final, tm, tf):
    m, d = h.shape
    dff = w1_16.shape[1]
    return pl.pallas_call(
        _ffn_kernel,
        grid=(m // tm, dff // tf),
        in_specs=[
            pl.BlockSpec((tm, d), lambda i, j: (i, 0)),
            pl.BlockSpec((1, d), lambda i, j: (0, 0)),
            pl.BlockSpec((d, tf), lambda i, j: (0, j)),
            pl.BlockSpec((tf, d), lambda i, j: (j, 0)),
            pl.BlockSpec((1, d), lambda i, j: (0, 0)),
        ],
        out_specs=pl.BlockSpec((tm, d), lambda i, j: (i, 0)),
        out_shape=jax.ShapeDtypeStruct((m, d), F32),
        scratch_shapes=[pltpu.VMEM((tm, d), BF16), pltpu.VMEM((tm, d), F32)],
        compiler_params=_cparams(("parallel", "arbitrary")),
        name="ffn",
    )(h, g, w1_16, w2_16, g_final)


def _pad_cols(a, n):
    return jnp.pad(a, ((0, 0), (0, n - a.shape[1])))


def _pad_rows(a, n):
    return jnp.pad(a, ((0, n - a.shape[0]), (0, 0)))


def _layer(x2, batch, seq, g_mix, w_in, sgu_ln_w, sgu_ln_b, sgu_w, sgu_b, w_proj_a, shift_b,
           w_lora_w, w0, a_lora_w, a0, g_lora_w, k_k, k_a, r_k, ln_x_w, ln_x_b, w_proj_b,
           w_out, g_ffn, w_ffn1, w_ffn2, g_out):
    d = x2.shape[1]
    lora_w, lora_a, lora_g = w_lora_w.shape[0], a_lora_w.shape[0], g_lora_w.shape[0]
    c_sgu = 2 * d
    c_rkv = 3 * d
    c_lora = lora_w + lora_a + lora_g
    o_lora = c_sgu + c_rkv
    o_gate = o_lora + c_lora
    pw, pa = LANES, LANES
    pg = -(-lora_g // LANES) * LANES

    w_main = jnp.concatenate([w_in[:, :o_lora], w_in[:, o_gate:]], axis=1).astype(BF16)
    w_lo = jnp.concatenate([
        _pad_cols(w_in[:, o_lora:o_lora + lora_w], pw),
        _pad_cols(w_in[:, o_lora + lora_w:o_lora + lora_w + lora_a], pa),
        _pad_cols(w_in[:, o_lora + lora_w + lora_a:o_gate], pg)], axis=1).astype(BF16)
    sb_rkv = shift_b[:, :c_rkv]
    sb_lo = shift_b[:, c_rkv:]
    sb_lora = jnp.concatenate([
        _pad_cols(sb_lo[:, :lora_w], pw),
        _pad_cols(sb_lo[:, lora_w:lora_w + lora_a], pa),
        _pad_cols(sb_lo[:, lora_w + lora_a:], pg)], axis=1)

    g_mix2 = g_mix.reshape(1, d)
    p_main = _in_proj(x2, g_mix2, w_main, tm=1024, tn=512)
    p_lora = _in_proj(x2, g_mix2, w_lo, tm=1024, tn=w_lo.shape[1])

    bias_full = jnp.repeat(sgu_b.T, d // SGU_GROUPS, axis=1)
    ya = _sgu(p_main, sgu_ln_w.reshape(1, d), sgu_ln_b.reshape(1, d), sgu_w, bias_full,
              w_proj_a.astype(BF16), ts=256, d=d)

    e = (lax.broadcasted_iota(jnp.int32, (d, LANES), 0) // RWKV_HEAD
         == lax.broadcasted_iota(jnp.int32, (d, LANES), 1)).astype(BF16)
    et = e.T
    vec_prep = jnp.stack([w0, a0, k_k, k_a])
    r, k, v, lw, kk, aa, g = _rwkv_prep(
        p_main, p_lora, sb_rkv, sb_lora, _pad_rows(w_lora_w, pw), _pad_rows(a_lora_w, pa),
        _pad_rows(g_lora_w, pg), vec_prep, e, et, batch, seq, d, tt=256)

    o = _rwkv_scan(r, k, v, lw, kk, aa, batch, seq, d, tblk=512)

    vec_post = jnp.stack([ln_x_w, ln_x_b, r_k])
    h1 = _rwkv_post(o, r, k, v, g, p_main, ya, x2, vec_post, e, et, w_proj_b.astype(BF16),
                    w_out.astype(BF16), tm=512, d=d)
    return _ffn(h1, g_ffn.reshape(1, d), w_ffn1.astype(BF16), w_ffn2.astype(BF16),
                g_out.reshape(1, d), tm=1024, tf=512)


def kernel(x, g_mix, w_in, sgu_ln_w, sgu_ln_b, sgu_w, sgu_b, w_proj_a, shift_b, w_lora_w, w0,
           a_lora_w, a0, g_lora_w, k_k, k_a, r_k, ln_x_w, ln_x_b, w_proj_b, w_out, g_ffn,
           w_ffn1, w_ffn2, g_final):
    batch, seq, d = x.shape
    depth = w_in.shape[0]
    assert depth == 1, "the final RMSNorm is fused into the single layer's ffn call"
    h = x.reshape(batch * seq, d)
    l = 0
    h = _layer(h, batch, seq, g_mix[l], w_in[l], sgu_ln_w[l], sgu_ln_b[l], sgu_w[l], sgu_b[l],
               w_proj_a[l], shift_b[l], w_lora_w[l], w0[l], a_lora_w[l], a0[l], g_lora_w[l],
               k_k[l], k_a[l], r_k[l], ln_x_w[l], ln_x_b[l], w_proj_b[l], w_out[l], g_ffn[l],
               w_ffn1[l], w_ffn2[l], g_final)
    return h.reshape(batch, seq, d)
```

```python
import functools
import math

import jax
import jax.numpy as jnp
from jax import lax
from jax.experimental import pallas as pl
from jax.experimental.pallas import tpu as pltpu

F32 = jnp.float32
BF16 = jnp.bfloat16

SGU_CHUNK = 128
SGU_GROUPS = 8
RWKV_HEAD = 64
NORM_EPS = 1e-6
LN_EPS = 1e-5
GN_EPS = 64e-5

LANES = 128
PREV_ROWS = 16
SCAN_CHUNK = 64
SCAN_GROUP_HEADS = 4
VMEM_LIMIT = 56 * 1024 * 1024
SCAN_HILO = dict(ar=False, bq=False, kq=False, v=False, bk=False, s0=False, a_ab=False,
                 a_ak=False, a_r=False, x=False, p=False, rhs=False, sa=False)


def _cparams(sem):
    return pltpu.CompilerParams(dimension_semantics=sem, vmem_limit_bytes=VMEM_LIMIT)


def _dot(a, b, dims=(((1,), (0,)), ((), ()))):
    return lax.dot_general(a, b, dims, preferred_element_type=F32)


_NT = (((1,), (1,)), ((), ()))
_TN = (((0,), (0,)), ((), ()))


def _split2(x):
    hi = x.astype(BF16)
    lo = (x - hi.astype(F32)).astype(BF16)
    return hi, lo


def _split3(x):
    hi = x.astype(BF16)
    r1 = x - hi.astype(F32)
    mid = r1.astype(BF16)
    lo = (r1 - mid.astype(F32)).astype(BF16)
    return hi, mid, lo


def _split1(x):
    return x.astype(BF16), None


def _dot_hp(a2, b2, dims=(((1,), (0,)), ((), ()))):
    ah, al = a2
    bh, bl = b2
    out = _dot(ah, bh, dims)
    if bl is not None:
        out = out + _dot(ah, bl, dims)
    if al is not None:
        out = out + _dot(al, bh, dims)
    return out


def _dot_x2(x, w_exact):
    hi, lo = _split2(x)
    return _dot(hi, w_exact) + _dot(lo, w_exact)


def _head_sum(x, e_ref, et_ref):
    return _dot_x2(_dot_x2(x, e_ref[...]), et_ref[...])


def _rms_rows(x, g):
    ms = jnp.mean(x * x, axis=-1, keepdims=True)
    return (x * lax.rsqrt(ms + NORM_EPS)) * g


def _in_proj_kernel(x_ref, xp_ref, g_ref, w_ref, sb_ref, o_ref, a_scr, ap_scr, *,
                    row_chunk, tiles_per_seq, mix_lo, mix_hi):
    i = pl.program_id(0)
    j = pl.program_id(1)

    @pl.when(j == 0)
    def _():
        def body(c, carry):
            rows = pl.ds(pl.multiple_of(c * row_chunk, row_chunk), row_chunk)
            a_scr[rows, :] = _rms_rows(x_ref[rows, :], g_ref[...]).astype(BF16)
            return carry
        lax.fori_loop(0, x_ref.shape[0] // row_chunk, body, 0)
        ap_scr[...] = _rms_rows(xp_ref[...], g_ref[...]).astype(BF16)

    mixed = jnp.logical_and(j >= mix_lo, j < mix_hi)

    @pl.when(mixed)
    def _():
        p = _dot(a_scr[...], w_ref[...])
        pp = _dot(ap_scr[...], w_ref[...])
        first = (i % tiles_per_seq) == 0
        prev_row = jnp.where(first, 0.0, pp[PREV_ROWS - 1:, :])
        row = lax.broadcasted_iota(jnp.int32, p.shape, 0)
        shifted = jnp.where(row == 0, prev_row, pltpu.roll(p, 1, 0))
        o_ref[...] = p * sb_ref[0:1, :] + shifted * sb_ref[1:2, :]

    @pl.when(jnp.logical_not(mixed))
    def _():
        o_ref[...] = _dot(a_scr[...], w_ref[...])


def _in_proj(x2, g, w16, sb, seq, tm, tn, mix_lo, mix_hi):
    m, d = x2.shape
    n = w16.shape[1]
    assert seq % tm == 0
    return pl.pallas_call(
        functools.partial(_in_proj_kernel, row_chunk=256, tiles_per_seq=seq // tm,
                          mix_lo=mix_lo, mix_hi=mix_hi),
        grid=(m // tm, n // tn),
        in_specs=[
            pl.BlockSpec((tm, d), lambda i, j: (i, 0)),
            pl.BlockSpec((PREV_ROWS, d),
                         lambda i, j: (jnp.maximum(i * (tm // PREV_ROWS) - 1, 0), 0)),
            pl.BlockSpec((1, d), lambda i, j: (0, 0)),
            pl.BlockSpec((d, tn), lambda i, j: (0, j)),
            pl.BlockSpec((2, tn), lambda i, j: (0, j)),
        ],
        out_specs=pl.BlockSpec((tm, tn), lambda i, j: (i, j)),
        out_shape=jax.ShapeDtypeStruct((m, n), F32),
        scratch_shapes=[pltpu.VMEM((tm, d), BF16), pltpu.VMEM((PREV_ROWS, d), BF16)],
        compiler_params=_cparams(("parallel", "arbitrary")),
        name="in_proj",
    )(x2, x2, g, w16, sb)


def _gelu(x):
    return 0.5 * x * (1.0 + lax.erf(x * (1.0 / math.sqrt(2.0))))


def _sgu_kernel(u_ref, v_ref, ga_ref, lnw_ref, lnb_ref, ws_ref, bias_ref, wpa_ref, o_ref,
                wm_scr, s_scr):
    ch = SGU_CHUNK
    dg = LANES

    @pl.when(pl.program_id(0) == 0)
    def _():
        row = lax.broadcasted_iota(jnp.int32, (ch, ch), 0)
        col = lax.broadcasted_iota(jnp.int32, (ch, ch), 1)
        for g in range(SGU_GROUPS):
            wm_scr[g] = jnp.where(col <= row, ws_ref[g], 0.0).astype(BF16)

    for c in range(u_ref.shape[0] // ch):
        rows = pl.ds(c * ch, ch)
        zu = _gelu(u_ref[rows, :])
        zv = _gelu(v_ref[rows, :])
        mu = jnp.mean(zv, axis=-1, keepdims=True)
        zc = zv - mu
        var = jnp.mean(zc * zc, axis=-1, keepdims=True)
        vn = ((zc * lax.rsqrt(var + LN_EPS)) * lnw_ref[...] + lnb_ref[...]).astype(BF16)
        for g in range(SGU_GROUPS):
            cols = slice(g * dg, (g + 1) * dg)
            sv = _dot(wm_scr[g], vn[:, cols]) + bias_ref[:, cols]
            s_scr[rows, cols] = (zu[:, cols] * sv).astype(BF16)

    ya = _dot(s_scr[...], wpa_ref[...])
    o_ref[...] = jax.nn.sigmoid(ga_ref[...]) * ya


def _sgu(p_main, ln_w, ln_b, sgu_w, bias_full, wpa16, ts, d):
    m = p_main.shape[0]
    ch = SGU_CHUNK
    return pl.pallas_call(
        _sgu_kernel,
        grid=(m // ts,),
        in_specs=[
            pl.BlockSpec((ts, d), lambda i: (i, 0)),
            pl.BlockSpec((ts, d), lambda i: (i, 1)),
            pl.BlockSpec((ts, d), lambda i: (i, 5)),
            pl.BlockSpec((1, d), lambda i: (0, 0)),
            pl.BlockSpec((1, d), lambda i: (0, 0)),
            pl.BlockSpec((SGU_GROUPS, ch, ch), lambda i: (0, 0, 0)),
            pl.BlockSpec((ch, d), lambda i: (0, 0)),
            pl.BlockSpec((d, d), lambda i: (0, 0)),
        ],
        out_specs=pl.BlockSpec((ts, d), lambda i: (i, 0)),
        out_shape=jax.ShapeDtypeStruct((m, d), F32),
        scratch_shapes=[pltpu.VMEM((SGU_GROUPS, ch, ch), BF16), pltpu.VMEM((ts, d), BF16)],
        compiler_params=_cparams(("arbitrary",)),
        name="sgu",
    )(p_main, p_main, p_main, ln_w, ln_b, sgu_w, bias_full, wpa16)


def _rwkv_prep_kernel(qk_ref, ql_ref, ww_ref, wa_ref, wg_ref, vec_ref, e_ref, et_ref,
                      k_out, lw_out, kk_out, aa_out, g_out):
    k = qk_ref[...]
    lo = ql_ref[...]
    xw = lo[:, 0:LANES]
    xa = lo[:, LANES:2 * LANES]
    xg = lo[:, 2 * LANES:]
    w0 = vec_ref[0:1, :]
    a0 = vec_ref[1:2, :]
    k_k = vec_ref[2:3, :]
    k_a = vec_ref[3:4, :]

    zw = w0 + _dot_hp(_split2(jnp.tanh(xw)), _split2(ww_ref[...]))
    lw_out[...] = (-math.exp(-0.5)) * jax.nn.sigmoid(zw)
    aa = jax.nn.sigmoid(a0 + _dot_hp(_split2(xa), _split2(wa_ref[...])))
    g_out[...] = _dot_hp(_split2(jax.nn.sigmoid(xg)), _split2(wg_ref[...]))
    kraw = k * k_k
    ss = _head_sum(kraw * kraw, e_ref, et_ref)
    kk_out[...] = kraw * lax.rsqrt(jnp.maximum(ss, 1e-24))
    k_out[...] = k * (1.0 + (aa - 1.0) * k_a)
    aa_out[...] = aa


def _rwkv_prep(p_main, p_lora, ww, wa, wg, vecs, e, et, d, tt):
    m = p_main.shape[0]
    lw = p_lora.shape[1]
    full = lambda a: pl.BlockSpec(a.shape, lambda i: (0,) * a.ndim)
    out_spec = pl.BlockSpec((tt, d), lambda i: (i, 0))
    return pl.pallas_call(
        _rwkv_prep_kernel,
        grid=(m // tt,),
        in_specs=[
            pl.BlockSpec((tt, d), lambda i: (i, 3)),
            pl.BlockSpec((tt, lw), lambda i: (i, 0)),
            full(ww), full(wa), full(wg), full(vecs), full(e), full(et),
        ],
        out_specs=[out_spec] * 5,
        out_shape=[jax.ShapeDtypeStruct((m, d), F32)] * 5,
        compiler_params=_cparams(("parallel",)),
        name="rwkv_prep",
    )(p_main, p_lora, ww, wa, wg, vecs, e, et)


def _scan_kernel(r_ref, k_ref, v_ref, lw_ref, kk_ref, aa_ref, o_ref, s_scr, *, chunk, hd, gw):
    c_len = chunk
    assert c_len == hd
    tblk, d = r_ref.shape
    n_groups = d // gw
    hpg = gw // hd
    n_double = int(math.log2(c_len)) - 1

    rowi = lax.broadcasted_iota(jnp.int32, (c_len, gw), 0)
    sub = lax.broadcasted_iota(jnp.int32, (c_len, gw), 1) % c_len
    strict = sub < rowi
    incl = sub <= rowi
    eye = jnp.where(sub == rowi, 1.0, 0.0).astype(F32)
    bd_mask = (lax.broadcasted_iota(jnp.int32, (gw, gw), 0) // c_len
               == lax.broadcasted_iota(jnp.int32, (gw, gw), 1) // hd)
    ltri = jnp.where(lax.broadcasted_iota(jnp.int32, (c_len, c_len), 1)
                     <= lax.broadcasted_iota(jnp.int32, (c_len, c_len), 0), 1.0, 0.0).astype(BF16)

    @pl.when(pl.program_id(1) == 0)
    def _():
        s_scr[...] = jnp.zeros_like(s_scr)

    def bd(x16):
        t = jnp.concatenate([x16] * hpg, axis=0)
        return jnp.where(bd_mask, t, jnp.zeros_like(t))

    def bd2(x2):
        return bd(x2[0]), None if x2[1] is None else bd(x2[1])

    def cat2(a2, b2, axis):
        hi = jnp.concatenate([a2[0], b2[0]], axis=axis)
        if a2[1] is None and b2[1] is None:
            return hi, None
        lo = [jnp.zeros_like(t[0]) if t[1] is None else t[1] for t in (a2, b2)]
        return hi, jnp.concatenate(lo, axis=axis)

    def sp(name, x):
        return (_split2 if SCAN_HILO[name] else _split1)(x)

    groups = range(n_groups)

    def body(c, carry):
        rows = pl.ds(pl.multiple_of(c * c_len, c_len), c_len)

        def load(ref, g):
            return ref[rows, g * gw:(g + 1) * gw]

        ar, bq, kq, v2, bk, g_tot = [], [], [], [], [], []
        for g in groups:
            k = load(k_ref, g)
            lw = load(lw_ref, g)
            kk = load(kk_ref, g)
            b = kk * load(aa_ref, g)
            lh, lm, ll = _split3(lw)
            cum = _dot(ltri, lh) + (_dot(ltri, lm) + _dot(ltri, ll))
            cum_last = cum[c_len - 1:c_len, :]
            g_inv = jnp.exp(-cum)
            g_end = jnp.exp(cum_last - cum)
            aq = -(kk * jnp.exp(cum - lw))
            rq = load(r_ref, g) * jnp.exp(cum)
            ar.append(sp("ar", jnp.concatenate([aq, rq], axis=0)))
            bq.append(sp("bq", b * g_inv))
            kq.append(sp("kq", k * g_inv))
            v2.append(sp("v", load(v_ref, g)))
            bk.append(sp("bk", jnp.concatenate([b * g_end, k * g_end], axis=0)))
            g_tot.append(jnp.exp(cum_last))

        pb = [_dot_hp(ar[g], bd2(bq[g]), _NT) for g in groups]
        pk = [_dot_hp(ar[g], bd2(kq[g]), _NT) for g in groups]
        s0 = [s_scr[g] for g in groups]
        ar_s = [_dot_hp(ar[g], sp("s0", s0[g]), _NT) for g in groups]
        a_ab = [sp("a_ab", jnp.where(strict, pb[g][:c_len], 0.0)) for g in groups]
        a_ak = [sp("a_ak", jnp.where(strict, pk[g][:c_len], 0.0)) for g in groups]
        a_r = [sp("a_r", jnp.concatenate([jnp.where(incl, pb[g][c_len:], 0.0),
                                          jnp.where(incl, pk[g][c_len:], 0.0)], axis=1))
               for g in groups]
        v_bd = [bd2(v2[g]) for g in groups]
        rhs = [ar_s[g][:c_len] + _dot_hp(a_ak[g], v_bd[g]) for g in groups]

        p = [_dot_hp(a_ab[g], bd2(a_ab[g])) for g in groups]
        x = [eye + jnp.where(strict, pb[g][:c_len], 0.0) for g in groups]
        for _ in range(n_double - 1):
            xp = [_dot_hp(sp("x", jnp.concatenate([x[g], p[g]], axis=0)), bd2(sp("p", p[g])))
                  for g in groups]
            x = [x[g] + xp[g][:c_len] for g in groups]
            p = [xp[g][c_len:] for g in groups]
        x = [x[g] + _dot_hp(sp("x", x[g]), bd2(sp("p", p[g]))) for g in groups]

        sa = [_dot_hp(sp("x", x[g]), bd2(sp("rhs", rhs[g]))) for g in groups]
        sa2 = [sp("sa", sa[g]) for g in groups]
        for g in groups:
            o_ref[rows, g * gw:(g + 1) * gw] = ar_s[g][c_len:] + _dot_hp(
                a_r[g], cat2(bd2(sa2[g]), v_bd[g], 0))
        for g in groups:
            upd = _dot_hp(cat2(sa2[g], v2[g], 0), bk[g], _TN)
            s_scr[g] = jnp.where(bd_mask, s0[g] * g_tot[g] + upd, 0.0)
        return carry

    lax.fori_loop(0, tblk // c_len, body, 0)


def _rwkv_scan(p_main, k, lw, kk, aa, batch, seq, d, tblk):
    gw = SCAN_GROUP_HEADS * RWKV_HEAD
    nt = seq // tblk
    spec = pl.BlockSpec((tblk, d), lambda b, t: (b * nt + t, 0))
    col = lambda cb: pl.BlockSpec((tblk, d), lambda b, t: (b * nt + t, cb))
    return pl.pallas_call(
        functools.partial(_scan_kernel, chunk=SCAN_CHUNK, hd=RWKV_HEAD, gw=gw),
        grid=(batch, nt),
        in_specs=[col(2), spec, col(4), spec, spec, spec],
        out_specs=spec,
        out_shape=jax.ShapeDtypeStruct((batch * seq, d), F32),
        scratch_shapes=[pltpu.VMEM((d // gw, gw, gw), F32)],
        compiler_params=_cparams(("parallel", "arbitrary")),
        name="rwkv_scan",
    )(p_main, k, p_main, lw, kk, aa)


def _rwkv_post_kernel(o_ref, r_ref, k_ref, v_ref, g_ref, gb_ref, ya_ref, x_ref,
                      vec_ref, e_ref, et_ref, wpb_ref, wout_ref, h_ref, *, hd):
    lnx_w = vec_ref[0:1, :]
    lnx_b = vec_ref[1:2, :]
    r_k = vec_ref[2:3, :]
    o = o_ref[...]
    inv_n = 1.0 / hd
    mu = _head_sum(o, e_ref, et_ref) * inv_n
    oc = o - mu
    var = _head_sum(oc * oc, e_ref, et_ref) * inv_n
    on = (oc * lax.rsqrt(var + GN_EPS)) * lnx_w + lnx_b
    v = v_ref[...]
    bonus = _head_sum((r_ref[...] * k_ref[...]) * r_k, e_ref, et_ref) * v
    yb = _dot(((on + bonus) * g_ref[...]).astype(BF16), wpb_ref[...])
    mixed = ya_ref[...] + jax.nn.sigmoid(gb_ref[...]) * yb
    h_ref[...] = x_ref[...] + _dot(mixed.astype(BF16), wout_ref[...])


def _rwkv_post(o, k, g, p_main, ya, x2, vecs, e, et, wpb16, wout16, tm, d):
    m = x2.shape[0]
    tile = pl.BlockSpec((tm, d), lambda i: (i, 0))
    col = lambda cb: pl.BlockSpec((tm, d), lambda i: (i, cb))
    full = lambda a: pl.BlockSpec(a.shape, lambda i: (0,) * a.ndim)
    return pl.pallas_call(
        functools.partial(_rwkv_post_kernel, hd=RWKV_HEAD),
        grid=(m // tm,),
        in_specs=[tile, col(2), tile, col(4), tile, col(6),
                  tile, tile, full(vecs), full(e), full(et), full(wpb16), full(wout16)],
        out_specs=tile,
        out_shape=jax.ShapeDtypeStruct((m, d), F32),
        compiler_params=_cparams(("parallel",)),
        name="rwkv_post",
    )(o, p_main, k, p_main, g, p_main, ya, x2, vecs, e, et, wpb16, wout16)


def _ffn_kernel(h_ref, g_ref, w1_ref, w2_ref, gf_ref, o_ref, f_scr, acc_scr):
    j = pl.program_id(1)

    @pl.when(j == 0)
    def _():
        f_scr[...] = _rms_rows(h_ref[...], g_ref[...]).astype(BF16)
        acc_scr[...] = jnp.zeros_like(acc_scr)

    t = jnp.maximum(_dot(f_scr[...], w1_ref[...]), 0.0)
    acc_scr[...] += _dot((t * t).astype(BF16), w2_ref[...])

    @pl.when(j == pl.num_programs(1) - 1)
    def _():
        o_ref[...] = _rms_rows(h_ref[...] + acc_scr[...], gf_ref[...])


def _ffn(h, g, w1_16, w2_16, g_final, tm, tf):
    m, d = h.shape
    dff = w1_16.shape[1]
    return pl.pallas_call(
        _ffn_kernel,
        grid=(m // tm, dff // tf),
        in_specs=[
            pl.BlockSpec((tm, d), lambda i, j: (i, 0)),
            pl.BlockSpec((1, d), lambda i, j: (0, 0)),
            pl.BlockSpec((d, tf), lambda i, j: (0, j)),
            pl.BlockSpec((tf, d), lambda i, j: (j, 0)),
            pl.BlockSpec((1, d), lambda i, j: (0, 0)),
        ],
        out_specs=pl.BlockSpec((tm, d), lambda i, j: (i, 0)),
        out_shape=jax.ShapeDtypeStruct((m, d), F32),
        scratch_shapes=[pltpu.VMEM((tm, d), BF16), pltpu.VMEM((tm, d), F32)],
        compiler_params=_cparams(("parallel", "arbitrary")),
        name="ffn",
    )(h, g, w1_16, w2_16, g_final)


def _pad_cols(a, n):
    return jnp.pad(a, ((0, 0), (0, n - a.shape[1])))


def _pad_rows(a, n):
    return jnp.pad(a, ((0, n - a.shape[0]), (0, 0)))


def _layer(x2, batch, seq, g_mix, w_in, sgu_ln_w, sgu_ln_b, sgu_w, sgu_b, w_proj_a, shift_b,
           w_lora_w, w0, a_lora_w, a0, g_lora_w, k_k, k_a, r_k, ln_x_w, ln_x_b, w_proj_b,
           w_out, g_ffn, w_ffn1, w_ffn2, g_out):
    d = x2.shape[1]
    lora_w, lora_a, lora_g = w_lora_w.shape[0], a_lora_w.shape[0], g_lora_w.shape[0]
    c_sgu = 2 * d
    c_rkv = 3 * d
    c_lora = lora_w + lora_a + lora_g
    o_lora = c_sgu + c_rkv
    o_gate = o_lora + c_lora
    pw, pa = LANES, LANES
    pg = -(-lora_g // LANES) * LANES

    w_main = jnp.concatenate([w_in[:, :o_lora], w_in[:, o_gate:]], axis=1).astype(BF16)
    w_lo = jnp.concatenate([
        _pad_cols(w_in[:, o_lora:o_lora + lora_w], pw),
        _pad_cols(w_in[:, o_lora + lora_w:o_lora + lora_w + lora_a], pa),
        _pad_cols(w_in[:, o_lora + lora_w + lora_a:o_gate], pg)], axis=1).astype(BF16)
    sb_lo = shift_b[:, c_rkv:]
    sb_lora = jnp.concatenate([
        _pad_cols(sb_lo[:, :lora_w], pw),
        _pad_cols(sb_lo[:, lora_w:lora_w + lora_a], pa),
        _pad_cols(sb_lo[:, lora_w + lora_a:], pg)], axis=1)
    sb_main = jnp.concatenate([jnp.zeros((2, c_sgu), F32), shift_b[:, :c_rkv],
                               jnp.zeros((2, 2 * d), F32)], axis=1)

    g_mix2 = g_mix.reshape(1, d)
    tn = 512
    p_main = _in_proj(x2, g_mix2, w_main, sb_main, seq, tm=1024, tn=tn,
                      mix_lo=c_sgu // tn, mix_hi=(c_sgu + c_rkv) // tn)
    p_lora = _in_proj(x2, g_mix2, w_lo, sb_lora, seq, tm=1024, tn=w_lo.shape[1],
                      mix_lo=0, mix_hi=1)

    bias_full = jnp.repeat(sgu_b.T, d // SGU_GROUPS, axis=1)
    ya = _sgu(p_main, sgu_ln_w.reshape(1, d), sgu_ln_b.reshape(1, d), sgu_w, bias_full,
              w_proj_a.astype(BF16), ts=256, d=d)

    e = (lax.broadcasted_iota(jnp.int32, (d, LANES), 0) // RWKV_HEAD
         == lax.broadcasted_iota(jnp.int32, (d, LANES), 1)).astype(BF16)
    et = e.T
    vec_prep = jnp.stack([w0, a0, k_k, k_a])
    k, lw, kk, aa, g = _rwkv_prep(
        p_main, p_lora, _pad_rows(w_lora_w, pw), _pad_rows(a_lora_w, pa),
        _pad_rows(g_lora_w, pg), vec_prep, e, et, d, tt=256)

    o = _rwkv_scan(p_main, k, lw, kk, aa, batch, seq, d, tblk=512)

    vec_post = jnp.stack([ln_x_w, ln_x_b, r_k])
    h1 = _rwkv_post(o, k, g, p_main, ya, x2, vec_post, e, et, w_proj_b.astype(BF16),
                    w_out.astype(BF16), tm=512, d=d)
    return _ffn(h1, g_ffn.reshape(1, d), w_ffn1.astype(BF16), w_ffn2.astype(BF16),
                g_out.reshape(1, d), tm=1024, tf=512)


def kernel(x, g_mix, w_in, sgu_ln_w, sgu_ln_b, sgu_w, sgu_b, w_proj_a, shift_b, w_lora_w, w0,
           a_lora_w, a0, g_lora_w, k_k, k_a, r_k, ln_x_w, ln_x_b, w_proj_b, w_out, g_ffn,
           w_ffn1, w_ffn2, g_final):
    batch, seq, d = x.shape
    depth = w_in.shape[0]
    assert depth == 1, "the final RMSNorm is fused into the single layer's ffn call"
    h = x.reshape(batch * seq, d)
    l = 0
    h = _layer(h, batch, seq, g_mix[l], w_in[l], sgu_ln_w[l], sgu_ln_b[l], sgu_w[l], sgu_b[l],
               w_proj_a[l], shift_b[l], w_lora_w[l], w0[l], a_lora_w[l], a0[l], g_lora_w[l],
               k_k[l], k_a[l], r_k[l], ln_x_w[l], ln_x_b[l], w_proj_b[l], w_out[l], g_ffn[l],
               w_ffn1[l], w_ffn2[l], g_final)
    return h.reshape(batch, seq, d)
```

```python
import functools
import math

import jax
import jax.numpy as jnp
from jax import lax
from jax.experimental import pallas as pl
from jax.experimental.pallas import tpu as pltpu

F32 = jnp.float32
BF16 = jnp.bfloat16

SGU_CHUNK = 128
SGU_GROUPS = 8
RWKV_HEAD = 64
NORM_EPS = 1e-6
LN_EPS = 1e-5
GN_EPS = 64e-5

LANES = 128
PREV_ROWS = 16
SCAN_CHUNK = 64
SCAN_GROUP_HEADS = 4
VMEM_LIMIT = 56 * 1024 * 1024
SCAN_HILO = dict(ar=False, bq=False, kq=False, v=False, bk=False, s0=False, a_ab=False,
                 a_ak=False, a_r=False, x=False, p=False, rhs=False, sa=False)


def _cparams(sem):
    return pltpu.CompilerParams(dimension_semantics=sem, vmem_limit_bytes=VMEM_LIMIT)


def _dot(a, b, dims=(((1,), (0,)), ((), ()))):
    return lax.dot_general(a, b, dims, preferred_element_type=F32)


_NT = (((1,), (1,)), ((), ()))
_TN = (((0,), (0,)), ((), ()))


def _split2(x):
    hi = x.astype(BF16)
    lo = (x - hi.astype(F32)).astype(BF16)
    return hi, lo


def _split3(x):
    hi = x.astype(BF16)
    r1 = x - hi.astype(F32)
    mid = r1.astype(BF16)
    lo = (r1 - mid.astype(F32)).astype(BF16)
    return hi, mid, lo


def _split1(x):
    return x.astype(BF16), None


def _dot_hp(a2, b2, dims=(((1,), (0,)), ((), ()))):
    ah, al = a2
    bh, bl = b2
    out = _dot(ah, bh, dims)
    if bl is not None:
        out = out + _dot(ah, bl, dims)
    if al is not None:
        out = out + _dot(al, bh, dims)
    return out


def _dot_x2(x, w_exact):
    hi, lo = _split2(x)
    return _dot(hi, w_exact) + _dot(lo, w_exact)


def _head_sum(x, hb_ref):
    w = hb_ref.shape[0]
    hb = hb_ref[...]
    return jnp.concatenate(
        [_dot_x2(x[:, c * w:(c + 1) * w], hb) for c in range(x.shape[1] // w)], axis=1)


def _rms_rows(x, g):
    ms = jnp.mean(x * x, axis=-1, keepdims=True)
    return (x * lax.rsqrt(ms + NORM_EPS)) * g


def _in_proj_kernel(x_ref, xp_ref, g_ref, w_ref, sb_ref, o_ref, a_scr, ap_scr, *,
                    row_chunk, tiles_per_seq, mix_lo, mix_hi):
    i = pl.program_id(0)
    j = pl.program_id(1)

    @pl.when(j == 0)
    def _():
        def body(c, carry):
            rows = pl.ds(pl.multiple_of(c * row_chunk, row_chunk), row_chunk)
            a_scr[rows, :] = _rms_rows(x_ref[rows, :], g_ref[...]).astype(BF16)
            return carry
        lax.fori_loop(0, x_ref.shape[0] // row_chunk, body, 0)
        ap_scr[...] = _rms_rows(xp_ref[...], g_ref[...]).astype(BF16)

    mixed = jnp.logical_and(j >= mix_lo, j < mix_hi)

    @pl.when(mixed)
    def _():
        p = _dot(a_scr[...], w_ref[...])
        pp = _dot(ap_scr[...], w_ref[...])
        first = (i % tiles_per_seq) == 0
        prev_row = jnp.where(first, 0.0, pp[PREV_ROWS - 1:, :])
        row = lax.broadcasted_iota(jnp.int32, p.shape, 0)
        shifted = jnp.where(row == 0, prev_row, pltpu.roll(p, 1, 0))
        o_ref[...] = p * sb_ref[0:1, :] + shifted * sb_ref[1:2, :]

    @pl.when(jnp.logical_not(mixed))
    def _():
        o_ref[...] = _dot(a_scr[...], w_ref[...])


def _in_proj(x2, g, w16, sb, seq, tm, tn, mix_lo, mix_hi):
    m, d = x2.shape
    n = w16.shape[1]
    assert seq % tm == 0
    return pl.pallas_call(
        functools.partial(_in_proj_kernel, row_chunk=256, tiles_per_seq=seq // tm,
                          mix_lo=mix_lo, mix_hi=mix_hi),
        grid=(m // tm, n // tn),
        in_specs=[
            pl.BlockSpec((tm, d), lambda i, j: (i, 0)),
            pl.BlockSpec((PREV_ROWS, d),
                         lambda i, j: (jnp.maximum(i * (tm // PREV_ROWS) - 1, 0), 0)),
            pl.BlockSpec((1, d), lambda i, j: (0, 0)),
            pl.BlockSpec((d, tn), lambda i, j: (0, j)),
            pl.BlockSpec((2, tn), lambda i, j: (0, j)),
        ],
        out_specs=pl.BlockSpec((tm, tn), lambda i, j: (i, j)),
        out_shape=jax.ShapeDtypeStruct((m, n), F32),
        scratch_shapes=[pltpu.VMEM((tm, d), BF16), pltpu.VMEM((PREV_ROWS, d), BF16)],
        compiler_params=_cparams(("parallel", "arbitrary")),
        name="in_proj",
    )(x2, x2, g, w16, sb)


def _gelu(x):
    return 0.5 * x * (1.0 + lax.erf(x * (1.0 / math.sqrt(2.0))))


def _sgu_kernel(u_ref, v_ref, ga_ref, lnw_ref, lnb_ref, ws_ref, bias_ref, wpa_ref, o_ref,
                wm_scr, s_scr):
    ch = SGU_CHUNK
    dg = LANES

    @pl.when(pl.program_id(0) == 0)
    def _():
        row = lax.broadcasted_iota(jnp.int32, (ch, ch), 0)
        col = lax.broadcasted_iota(jnp.int32, (ch, ch), 1)
        for g in range(SGU_GROUPS):
            wm_scr[g] = jnp.where(col <= row, ws_ref[g], 0.0).astype(BF16)

    for c in range(u_ref.shape[0] // ch):
        rows = pl.ds(c * ch, ch)
        zu = _gelu(u_ref[rows, :])
        zv = _gelu(v_ref[rows, :])
        mu = jnp.mean(zv, axis=-1, keepdims=True)
        zc = zv - mu
        var = jnp.mean(zc * zc, axis=-1, keepdims=True)
        vn = ((zc * lax.rsqrt(var + LN_EPS)) * lnw_ref[...] + lnb_ref[...]).astype(BF16)
        for g in range(SGU_GROUPS):
            cols = slice(g * dg, (g + 1) * dg)
            sv = _dot(wm_scr[g], vn[:, cols]) + bias_ref[:, cols]
            s_scr[rows, cols] = (zu[:, cols] * sv).astype(BF16)

    ya = _dot(s_scr[...], wpa_ref[...])
    o_ref[...] = jax.nn.sigmoid(ga_ref[...]) * ya


def _sgu(p_main, ln_w, ln_b, sgu_w, bias_full, wpa16, ts, d):
    m = p_main.shape[0]
    ch = SGU_CHUNK
    return pl.pallas_call(
        _sgu_kernel,
        grid=(m // ts,),
        in_specs=[
            pl.BlockSpec((ts, d), lambda i: (i, 0)),
            pl.BlockSpec((ts, d), lambda i: (i, 1)),
            pl.BlockSpec((ts, d), lambda i: (i, 5)),
            pl.BlockSpec((1, d), lambda i: (0, 0)),
            pl.BlockSpec((1, d), lambda i: (0, 0)),
            pl.BlockSpec((SGU_GROUPS, ch, ch), lambda i: (0, 0, 0)),
            pl.BlockSpec((ch, d), lambda i: (0, 0)),
            pl.BlockSpec((d, d), lambda i: (0, 0)),
        ],
        out_specs=pl.BlockSpec((ts, d), lambda i: (i, 0)),
        out_shape=jax.ShapeDtypeStruct((m, d), F32),
        scratch_shapes=[pltpu.VMEM((SGU_GROUPS, ch, ch), BF16), pltpu.VMEM((ts, d), BF16)],
        compiler_params=_cparams(("arbitrary",)),
        name="sgu",
    )(p_main, p_main, p_main, ln_w, ln_b, sgu_w, bias_full, wpa16)


def _rwkv_prep_kernel(qk_ref, ql_ref, ww_ref, wa_ref, wg_ref, vec_ref, hb_ref,
                      k_out, lw_out, kk_out, aa_out, g_out):
    k = qk_ref[...]
    lo = ql_ref[...]
    xw = lo[:, 0:LANES]
    xa = lo[:, LANES:2 * LANES]
    xg = lo[:, 2 * LANES:]
    w0 = vec_ref[0:1, :]
    a0 = vec_ref[1:2, :]
    k_k = vec_ref[2:3, :]
    k_a = vec_ref[3:4, :]

    zw = w0 + _dot_hp(_split2(jnp.tanh(xw)), _split2(ww_ref[...]))
    lw_out[...] = (-math.exp(-0.5)) * jax.nn.sigmoid(zw)
    aa = jax.nn.sigmoid(a0 + _dot_hp(_split2(xa), _split2(wa_ref[...])))
    g_out[...] = _dot_hp(_split2(jax.nn.sigmoid(xg)), _split2(wg_ref[...]))
    kraw = k * k_k
    ss = _head_sum(kraw * kraw, hb_ref)
    kk_out[...] = kraw * lax.rsqrt(jnp.maximum(ss, 1e-24))
    k_out[...] = k * (1.0 + (aa - 1.0) * k_a)
    aa_out[...] = aa


def _rwkv_prep(p_main, p_lora, ww, wa, wg, vecs, hb, d, tt):
    m = p_main.shape[0]
    lw = p_lora.shape[1]
    full = lambda a: pl.BlockSpec(a.shape, lambda i: (0,) * a.ndim)
    out_spec = pl.BlockSpec((tt, d), lambda i: (i, 0))
    return pl.pallas_call(
        _rwkv_prep_kernel,
        grid=(m // tt,),
        in_specs=[
            pl.BlockSpec((tt, d), lambda i: (i, 3)),
            pl.BlockSpec((tt, lw), lambda i: (i, 0)),
            full(ww), full(wa), full(wg), full(vecs), full(hb),
        ],
        out_specs=[out_spec] * 5,
        out_shape=[jax.ShapeDtypeStruct((m, d), F32)] * 5,
        compiler_params=_cparams(("parallel",)),
        name="rwkv_prep",
    )(p_main, p_lora, ww, wa, wg, vecs, hb)


def _scan_kernel(r_ref, k_ref, v_ref, lw_ref, kk_ref, aa_ref, o_ref, s_scr, *, chunk, hd, gw):
    c_len = chunk
    assert c_len == hd
    tblk, d = r_ref.shape
    n_groups = d // gw
    hpg = gw // hd
    n_double = int(math.log2(c_len)) - 1

    rowi = lax.broadcasted_iota(jnp.int32, (c_len, gw), 0)
    sub = lax.broadcasted_iota(jnp.int32, (c_len, gw), 1) % c_len
    strict = sub < rowi
    incl = sub <= rowi
    eye = jnp.where(sub == rowi, 1.0, 0.0).astype(F32)
    bd_mask = (lax.broadcasted_iota(jnp.int32, (gw, gw), 0) // c_len
               == lax.broadcasted_iota(jnp.int32, (gw, gw), 1) // hd)
    ltri = jnp.where(lax.broadcasted_iota(jnp.int32, (c_len, c_len), 1)
                     <= lax.broadcasted_iota(jnp.int32, (c_len, c_len), 0), 1.0, 0.0).astype(BF16)

    @pl.when(pl.program_id(1) == 0)
    def _():
        s_scr[...] = jnp.zeros_like(s_scr)

    def bd(x16):
        t = jnp.concatenate([x16] * hpg, axis=0)
        return jnp.where(bd_mask, t, jnp.zeros_like(t))

    def bd2(x2):
        return bd(x2[0]), None if x2[1] is None else bd(x2[1])

    def cat2(a2, b2, axis):
        hi = jnp.concatenate([a2[0], b2[0]], axis=axis)
        if a2[1] is None and b2[1] is None:
            return hi, None
        lo = [jnp.zeros_like(t[0]) if t[1] is None else t[1] for t in (a2, b2)]
        return hi, jnp.concatenate(lo, axis=axis)

    def sp(name, x):
        return (_split2 if SCAN_HILO[name] else _split1)(x)

    groups = range(n_groups)

    def body(c, carry):
        rows = pl.ds(pl.multiple_of(c * c_len, c_len), c_len)

        def load(ref, g):
            return ref[rows, g * gw:(g + 1) * gw]

        ar, bq, kq, v2, bk, g_tot = [], [], [], [], [], []
        for g in groups:
            k = load(k_ref, g)
            lw = load(lw_ref, g)
            kk = load(kk_ref, g)
            b = kk * load(aa_ref, g)
            lh, lm, ll = _split3(lw)
            cum = _dot(ltri, lh) + (_dot(ltri, lm) + _dot(ltri, ll))
            cum_last = cum[c_len - 1:c_len, :]
            g_inv = jnp.exp(-cum)
            g_end = jnp.exp(cum_last - cum)
            aq = -(kk * jnp.exp(cum - lw))
            rq = load(r_ref, g) * jnp.exp(cum)
            ar.append(sp("ar", jnp.concatenate([aq, rq], axis=0)))
            bq.append(sp("bq", b * g_inv))
            kq.append(sp("kq", k * g_inv))
            v2.append(sp("v", load(v_ref, g)))
            bk.append(sp("bk", jnp.concatenate([b * g_end, k * g_end], axis=0)))
            g_tot.append(jnp.exp(cum_last))

        pb = [_dot_hp(ar[g], bd2(bq[g]), _NT) for g in groups]
        pk = [_dot_hp(ar[g], bd2(kq[g]), _NT) for g in groups]
        s0 = [s_scr[g] for g in groups]
        ar_s = [_dot_hp(ar[g], sp("s0", s0[g]), _NT) for g in groups]
        a_ab = [sp("a_ab", jnp.where(strict, pb[g][:c_len], 0.0)) for g in groups]
        a_ak = [sp("a_ak", jnp.where(strict, pk[g][:c_len], 0.0)) for g in groups]
        a_r = [sp("a_r", jnp.concatenate([jnp.where(incl, pb[g][c_len:], 0.0),
                                          jnp.where(incl, pk[g][c_len:], 0.0)], axis=1))
               for g in groups]
        v_bd = [bd2(v2[g]) for g in groups]
        rhs = [ar_s[g][:c_len] + _dot_hp(a_ak[g], v_bd[g]) for g in groups]

        p = [_dot_hp(a_ab[g], bd2(a_ab[g])) for g in groups]
        x = [eye + jnp.where(strict, pb[g][:c_len], 0.0) for g in groups]
        for _ in range(n_double - 1):
            xp = [_dot_hp(sp("x", jnp.concatenate([x[g], p[g]], axis=0)), bd2(sp("p", p[g])))
                  for g in groups]
            x = [x[g] + xp[g][:c_len] for g in groups]
            p = [xp[g][c_len:] for g in groups]
        x = [x[g] + _dot_hp(sp("x", x[g]), bd2(sp("p", p[g]))) for g in groups]

        sa = [_dot_hp(sp("x", x[g]), bd2(sp("rhs", rhs[g]))) for g in groups]
        sa2 = [sp("sa", sa[g]) for g in groups]
        for g in groups:
            o_ref[rows, g * gw:(g + 1) * gw] = ar_s[g][c_len:] + _dot_hp(
                a_r[g], cat2(bd2(sa2[g]), v_bd[g], 0))
        for g in groups:
            upd = _dot_hp(cat2(sa2[g], v2[g], 0), bk[g], _TN)
            s_scr[g] = jnp.where(bd_mask, s0[g] * g_tot[g] + upd, 0.0)
        return carry

    lax.fori_loop(0, tblk // c_len, body, 0)


def _rwkv_scan(p_main, k, lw, kk, aa, batch, seq, d, tblk):
    gw = SCAN_GROUP_HEADS * RWKV_HEAD
    nt = seq // tblk
    spec = pl.BlockSpec((tblk, d), lambda b, t: (b * nt + t, 0))
    col = lambda cb: pl.BlockSpec((tblk, d), lambda b, t: (b * nt + t, cb))
    return pl.pallas_call(
        functools.partial(_scan_kernel, chunk=SCAN_CHUNK, hd=RWKV_HEAD, gw=gw),
        grid=(batch, nt),
        in_specs=[col(2), spec, col(4), spec, spec, spec],
        out_specs=spec,
        out_shape=jax.ShapeDtypeStruct((batch * seq, d), F32),
        scratch_shapes=[pltpu.VMEM((d // gw, gw, gw), F32)],
        compiler_params=_cparams(("parallel", "arbitrary")),
        name="rwkv_scan",
    )(p_main, k, p_main, lw, kk, aa)


def _rwkv_post_kernel(o_ref, r_ref, k_ref, v_ref, g_ref, gb_ref, ya_ref, x_ref,
                      vec_ref, hb_ref, wpb_ref, wout_ref, h_ref, *, hd):
    lnx_w = vec_ref[0:1, :]
    lnx_b = vec_ref[1:2, :]
    r_k = vec_ref[2:3, :]
    o = o_ref[...]
    inv_n = 1.0 / hd
    mu = _head_sum(o, hb_ref) * inv_n
    oc = o - mu
    var = _head_sum(oc * oc, hb_ref) * inv_n
    on = (oc * lax.rsqrt(var + GN_EPS)) * lnx_w + lnx_b
    v = v_ref[...]
    bonus = _head_sum((r_ref[...] * k_ref[...]) * r_k, hb_ref) * v
    yb = _dot(((on + bonus) * g_ref[...]).astype(BF16), wpb_ref[...])
    mixed = ya_ref[...] + jax.nn.sigmoid(gb_ref[...]) * yb
    h_ref[...] = x_ref[...] + _dot(mixed.astype(BF16), wout_ref[...])


def _rwkv_post(o, k, g, p_main, ya, x2, vecs, hb, wpb16, wout16, tm, d):
    m = x2.shape[0]
    tile = pl.BlockSpec((tm, d), lambda i: (i, 0))
    col = lambda cb: pl.BlockSpec((tm, d), lambda i: (i, cb))
    full = lambda a: pl.BlockSpec(a.shape, lambda i: (0,) * a.ndim)
    return pl.pallas_call(
        functools.partial(_rwkv_post_kernel, hd=RWKV_HEAD),
        grid=(m // tm,),
        in_specs=[tile, col(2), tile, col(4), tile, col(6),
                  tile, tile, full(vecs), full(hb), full(wpb16), full(wout16)],
        out_specs=tile,
        out_shape=jax.ShapeDtypeStruct((m, d), F32),
        compiler_params=_cparams(("parallel",)),
        name="rwkv_post",
    )(o, p_main, k, p_main, g, p_main, ya, x2, vecs, hb, wpb16, wout16)


def _ffn_kernel(h_ref, g_ref, w1_ref, w2_ref, gf_ref, o_ref, f_scr, acc_scr):
    j = pl.program_id(1)

    @pl.when(j == 0)
    def _():
        f_scr[...] = _rms_rows(h_ref[...], g_ref[...]).astype(BF16)
        acc_scr[...] = jnp.zeros_like(acc_scr)

    t = jnp.maximum(_dot(f_scr[...], w1_ref[...]), 0.0)
    acc_scr[...] += _dot((t * t).astype(BF16), w2_ref[...])

    @pl.when(j == pl.num_programs(1) - 1)
    def _():
        o_ref[...] = _rms_rows(h_ref[...] + acc_scr[...], gf_ref[...])


def _ffn(h, g, w1_16, w2_16, g_final, tm, tf):
    m, d = h.shape
    dff = w1_16.shape[1]
    return pl.pallas_call(
        _ffn_kernel,
        grid=(m // tm, dff // tf),
        in_specs=[
            pl.BlockSpec((tm, d), lambda i, j: (i, 0)),
            pl.BlockSpec((1, d), lambda i, j: (0, 0)),
            pl.BlockSpec((d, tf), lambda i, j: (0, j)),
            pl.BlockSpec((tf, d), lambda i, j: (j, 0)),
            pl.BlockSpec((1, d), lambda i, j: (0, 0)),
        ],
        out_specs=pl.BlockSpec((tm, d), lambda i, j: (i, 0)),
        out_shape=jax.ShapeDtypeStruct((m, d), F32),
        scratch_shapes=[pltpu.VMEM((tm, d), BF16), pltpu.VMEM((tm, d), F32)],
        compiler_params=_cparams(("parallel", "arbitrary")),
        name="ffn",
    )(h, g, w1_16, w2_16, g_final)


def _pad_cols(a, n):
    return jnp.pad(a, ((0, 0), (0, n - a.shape[1])))


def _pad_rows(a, n):
    return jnp.pad(a, ((0, n - a.shape[0]), (0, 0)))


def _layer(x2, batch, seq, g_mix, w_in, sgu_ln_w, sgu_ln_b, sgu_w, sgu_b, w_proj_a, shift_b,
           w_lora_w, w0, a_lora_w, a0, g_lora_w, k_k, k_a, r_k, ln_x_w, ln_x_b, w_proj_b,
           w_out, g_ffn, w_ffn1, w_ffn2, g_out):
    d = x2.shape[1]
    lora_w, lora_a, lora_g = w_lora_w.shape[0], a_lora_w.shape[0], g_lora_w.shape[0]
    c_sgu = 2 * d
    c_rkv = 3 * d
    c_lora = lora_w + lora_a + lora_g
    o_lora = c_sgu + c_rkv
    o_gate = o_lora + c_lora
    pw, pa = LANES, LANES
    pg = -(-lora_g // LANES) * LANES

    w_main = jnp.concatenate([w_in[:, :o_lora], w_in[:, o_gate:]], axis=1).astype(BF16)
    w_lo = jnp.concatenate([
        _pad_cols(w_in[:, o_lora:o_lora + lora_w], pw),
        _pad_cols(w_in[:, o_lora + lora_w:o_lora + lora_w + lora_a], pa),
        _pad_cols(w_in[:, o_lora + lora_w + lora_a:o_gate], pg)], axis=1).astype(BF16)
    sb_lo = shift_b[:, c_rkv:]
    sb_lora = jnp.concatenate([
        _pad_cols(sb_lo[:, :lora_w], pw),
        _pad_cols(sb_lo[:, lora_w:lora_w + lora_a], pa),
        _pad_cols(sb_lo[:, lora_w + lora_a:], pg)], axis=1)
    sb_main = jnp.concatenate([jnp.zeros((2, c_sgu), F32), shift_b[:, :c_rkv],
                               jnp.zeros((2, 2 * d), F32)], axis=1)

    g_mix2 = g_mix.reshape(1, d)
    tn = 1024
    p_main = _in_proj(x2, g_mix2, w_main, sb_main, seq, tm=1024, tn=tn,
                      mix_lo=c_sgu // tn, mix_hi=(c_sgu + c_rkv) // tn)
    p_lora = _in_proj(x2, g_mix2, w_lo, sb_lora, seq, tm=1024, tn=w_lo.shape[1],
                      mix_lo=0, mix_hi=1)

    bias_full = jnp.repeat(sgu_b.T, d // SGU_GROUPS, axis=1)
    ya = _sgu(p_main, sgu_ln_w.reshape(1, d), sgu_ln_b.reshape(1, d), sgu_w, bias_full,
              w_proj_a.astype(BF16), ts=256, d=d)

    gw = SCAN_GROUP_HEADS * RWKV_HEAD
    hb = (lax.broadcasted_iota(jnp.int32, (gw, gw), 0) // RWKV_HEAD
          == lax.broadcasted_iota(jnp.int32, (gw, gw), 1) // RWKV_HEAD).astype(BF16)
    vec_prep = jnp.stack([w0, a0, k_k, k_a])
    k, lw, kk, aa, g = _rwkv_prep(
        p_main, p_lora, _pad_rows(w_lora_w, pw), _pad_rows(a_lora_w, pa),
        _pad_rows(g_lora_w, pg), vec_prep, hb, d, tt=256)

    o = _rwkv_scan(p_main, k, lw, kk, aa, batch, seq, d, tblk=512)

    vec_post = jnp.stack([ln_x_w, ln_x_b, r_k])
    h1 = _rwkv_post(o, k, g, p_main, ya, x2, vec_post, hb, w_proj_b.astype(BF16),
                    w_out.astype(BF16), tm=512, d=d)
    return _ffn(h1, g_ffn.reshape(1, d), w_ffn1.astype(BF16), w_ffn2.astype(BF16),
                g_out.reshape(1, d), tm=1024, tf=1024)


def kernel(x, g_mix, w_in, sgu_ln_w, sgu_ln_b, sgu_w, sgu_b, w_proj_a, shift_b, w_lora_w, w0,
           a_lora_w, a0, g_lora_w, k_k, k_a, r_k, ln_x_w, ln_x_b, w_proj_b, w_out, g_ffn,
           w_ffn1, w_ffn2, g_final):
    batch, seq, d = x.shape
    depth = w_in.shape[0]
    assert depth == 1, "the final RMSNorm is fused into the single layer's ffn call"
    h = x.reshape(batch * seq, d)
    l = 0
    h = _layer(h, batch, seq, g_mix[l], w_in[l], sgu_ln_w[l], sgu_ln_b[l], sgu_w[l], sgu_b[l],
               w_proj_a[l], shift_b[l], w_lora_w[l], w0[l], a_lora_w[l], a0[l], g_lora_w[l],
               k_k[l], k_a[l], r_k[l], ln_x_w[l], ln_x_b[l], w_proj_b[l], w_out[l], g_ffn[l],
               w_ffn1[l], w_ffn2[l], g_final)
    return h.reshape(batch, seq, d)
```

```python
import functools
import math

import jax
import jax.numpy as jnp
from jax import lax
from jax.experimental import pallas as pl
from jax.experimental.pallas import tpu as pltpu

F32 = jnp.float32
BF16 = jnp.bfloat16

SGU_CHUNK = 128
SGU_GROUPS = 8
RWKV_HEAD = 64
NORM_EPS = 1e-6
LN_EPS = 1e-5
GN_EPS = 64e-5

LANES = 128
PREV_ROWS = 16
SCAN_CHUNK = 64
SCAN_GROUP_HEADS = 4
VMEM_LIMIT = 56 * 1024 * 1024
SCAN_HILO = dict(ar=False, bq=False, kq=False, v=False, bk=False, s0=False, a_ab=False,
                 a_ak=False, a_r=False, x=False, p=False, rhs=False, sa=False)


def _cparams(sem):
    return pltpu.CompilerParams(dimension_semantics=sem, vmem_limit_bytes=VMEM_LIMIT)


def _dot(a, b, dims=(((1,), (0,)), ((), ()))):
    return lax.dot_general(a, b, dims, preferred_element_type=F32)


_NT = (((1,), (1,)), ((), ()))
_TN = (((0,), (0,)), ((), ()))


def _split2(x):
    hi = x.astype(BF16)
    lo = (x - hi.astype(F32)).astype(BF16)
    return hi, lo


def _split3(x):
    hi = x.astype(BF16)
    r1 = x - hi.astype(F32)
    mid = r1.astype(BF16)
    lo = (r1 - mid.astype(F32)).astype(BF16)
    return hi, mid, lo


def _split1(x):
    return x.astype(BF16), None


def _dot_hp(a2, b2, dims=(((1,), (0,)), ((), ()))):
    ah, al = a2
    bh, bl = b2
    out = _dot(ah, bh, dims)
    if bl is not None:
        out = out + _dot(ah, bl, dims)
    if al is not None:
        out = out + _dot(al, bh, dims)
    return out


def _dot_x2(x, w_exact):
    hi, lo = _split2(x)
    return _dot(hi, w_exact) + _dot(lo, w_exact)


def _head_sum(x, hb_ref):
    w = hb_ref.shape[0]
    hb = hb_ref[...]
    return jnp.concatenate(
        [_dot_x2(x[:, c * w:(c + 1) * w], hb) for c in range(x.shape[1] // w)], axis=1)


def _rms_rows(x, g):
    ms = jnp.mean(x * x, axis=-1, keepdims=True)
    return (x * lax.rsqrt(ms + NORM_EPS)) * g


def _in_proj_kernel(x_ref, xp_ref, g_ref, w_ref, sb_ref, o_ref, a_scr, ap_scr, *,
                    row_chunk, tiles_per_seq, mix_lo, mix_hi):
    i = pl.program_id(0)
    j = pl.program_id(1)

    @pl.when(j == 0)
    def _():
        def body(c, carry):
            rows = pl.ds(pl.multiple_of(c * row_chunk, row_chunk), row_chunk)
            a_scr[rows, :] = _rms_rows(x_ref[rows, :], g_ref[...]).astype(BF16)
            return carry
        lax.fori_loop(0, x_ref.shape[0] // row_chunk, body, 0)
        ap_scr[...] = _rms_rows(xp_ref[...], g_ref[...]).astype(BF16)

    mixed = jnp.logical_and(j >= mix_lo, j < mix_hi)

    @pl.when(mixed)
    def _():
        p = _dot(a_scr[...], w_ref[...])
        pp = _dot(ap_scr[...], w_ref[...])
        first = (i % tiles_per_seq) == 0
        prev_row = jnp.where(first, 0.0, pp[PREV_ROWS - 1:, :])
        row = lax.broadcasted_iota(jnp.int32, p.shape, 0)
        shifted = jnp.where(row == 0, prev_row, pltpu.roll(p, 1, 0))
        o_ref[...] = p * sb_ref[0:1, :] + shifted * sb_ref[1:2, :]

    @pl.when(jnp.logical_not(mixed))
    def _():
        o_ref[...] = _dot(a_scr[...], w_ref[...])


def _in_proj(x2, g, w16, sb, seq, tm, tn, mix_lo, mix_hi):
    m, d = x2.shape
    n = w16.shape[1]
    assert seq % tm == 0
    return pl.pallas_call(
        functools.partial(_in_proj_kernel, row_chunk=256, tiles_per_seq=seq // tm,
                          mix_lo=mix_lo, mix_hi=mix_hi),
        grid=(m // tm, n // tn),
        in_specs=[
            pl.BlockSpec((tm, d), lambda i, j: (i, 0)),
            pl.BlockSpec((PREV_ROWS, d),
                         lambda i, j: (jnp.maximum(i * (tm // PREV_ROWS) - 1, 0), 0)),
            pl.BlockSpec((1, d), lambda i, j: (0, 0)),
            pl.BlockSpec((d, tn), lambda i, j: (0, j)),
            pl.BlockSpec((2, tn), lambda i, j: (0, j)),
        ],
        out_specs=pl.BlockSpec((tm, tn), lambda i, j: (i, j)),
        out_shape=jax.ShapeDtypeStruct((m, n), F32),
        scratch_shapes=[pltpu.VMEM((tm, d), BF16), pltpu.VMEM((PREV_ROWS, d), BF16)],
        compiler_params=_cparams(("parallel", "arbitrary")),
        name="in_proj",
    )(x2, x2, g, w16, sb)


def _gelu(x):
    return 0.5 * x * (1.0 + lax.erf(x * (1.0 / math.sqrt(2.0))))


def _sgu_kernel(u_ref, v_ref, ga_ref, lnw_ref, lnb_ref, ws_ref, bias_ref, wpa_ref, o_ref,
                wm_scr, s_scr):
    ch = SGU_CHUNK
    dg = LANES

    @pl.when(pl.program_id(0) == 0)
    def _():
        row = lax.broadcasted_iota(jnp.int32, (ch, ch), 0)
        col = lax.broadcasted_iota(jnp.int32, (ch, ch), 1)
        for g in range(SGU_GROUPS):
            wm_scr[g] = jnp.where(col <= row, ws_ref[g], 0.0).astype(BF16)

    for c in range(u_ref.shape[0] // ch):
        rows = pl.ds(c * ch, ch)
        zu = _gelu(u_ref[rows, :])
        zv = _gelu(v_ref[rows, :])
        mu = jnp.mean(zv, axis=-1, keepdims=True)
        zc = zv - mu
        var = jnp.mean(zc * zc, axis=-1, keepdims=True)
        vn = ((zc * lax.rsqrt(var + LN_EPS)) * lnw_ref[...] + lnb_ref[...]).astype(BF16)
        for g in range(SGU_GROUPS):
            cols = slice(g * dg, (g + 1) * dg)
            sv = _dot(wm_scr[g], vn[:, cols]) + bias_ref[:, cols]
            s_scr[rows, cols] = (zu[:, cols] * sv).astype(BF16)

    ya = _dot(s_scr[...], wpa_ref[...])
    o_ref[...] = jax.nn.sigmoid(ga_ref[...]) * ya


def _sgu(p_main, ln_w, ln_b, sgu_w, bias_full, wpa16, ts, d):
    m = p_main.shape[0]
    ch = SGU_CHUNK
    return pl.pallas_call(
        _sgu_kernel,
        grid=(m // ts,),
        in_specs=[
            pl.BlockSpec((ts, d), lambda i: (i, 0)),
            pl.BlockSpec((ts, d), lambda i: (i, 1)),
            pl.BlockSpec((ts, d), lambda i: (i, 5)),
            pl.BlockSpec((1, d), lambda i: (0, 0)),
            pl.BlockSpec((1, d), lambda i: (0, 0)),
            pl.BlockSpec((SGU_GROUPS, ch, ch), lambda i: (0, 0, 0)),
            pl.BlockSpec((ch, d), lambda i: (0, 0)),
            pl.BlockSpec((d, d), lambda i: (0, 0)),
        ],
        out_specs=pl.BlockSpec((ts, d), lambda i: (i, 0)),
        out_shape=jax.ShapeDtypeStruct((m, d), F32),
        scratch_shapes=[pltpu.VMEM((SGU_GROUPS, ch, ch), BF16), pltpu.VMEM((ts, d), BF16)],
        compiler_params=_cparams(("arbitrary",)),
        name="sgu",
    )(p_main, p_main, p_main, ln_w, ln_b, sgu_w, bias_full, wpa16)


def _rwkv_prep_kernel(qk_ref, ql_ref, ww_ref, wa_ref, wg_ref, vec_ref, hb_ref,
                      k_out, lw_out, kk_out, aa_out, g_out):
    k = qk_ref[...]
    lo = ql_ref[...]
    xw = lo[:, 0:LANES]
    xa = lo[:, LANES:2 * LANES]
    xg = lo[:, 2 * LANES:]
    w0 = vec_ref[0:1, :]
    a0 = vec_ref[1:2, :]
    k_k = vec_ref[2:3, :]
    k_a = vec_ref[3:4, :]

    zw = w0 + _dot_hp(_split2(jnp.tanh(xw)), _split2(ww_ref[...]))
    lw_out[...] = (-math.exp(-0.5)) * jax.nn.sigmoid(zw)
    aa = jax.nn.sigmoid(a0 + _dot_hp(_split2(xa), _split2(wa_ref[...])))
    g_out[...] = _dot_hp(_split2(jax.nn.sigmoid(xg)), _split2(wg_ref[...]))
    kraw = k * k_k
    ss = _head_sum(kraw * kraw, hb_ref)
    kk_out[...] = kraw * lax.rsqrt(jnp.maximum(ss, 1e-24))
    k_out[...] = k * (1.0 + (aa - 1.0) * k_a)
    aa_out[...] = aa


def _rwkv_prep(p_main, p_lora, ww, wa, wg, vecs, hb, d, tt):
    m = p_main.shape[0]
    lw = p_lora.shape[1]
    full = lambda a: pl.BlockSpec(a.shape, lambda i: (0,) * a.ndim)
    out_spec = pl.BlockSpec((tt, d), lambda i: (i, 0))
    return pl.pallas_call(
        _rwkv_prep_kernel,
        grid=(m // tt,),
        in_specs=[
            pl.BlockSpec((tt, d), lambda i: (i, 3)),
            pl.BlockSpec((tt, lw), lambda i: (i, 0)),
            full(ww), full(wa), full(wg), full(vecs), full(hb),
        ],
        out_specs=[out_spec] * 5,
        out_shape=[jax.ShapeDtypeStruct((m, d), F32)] * 5,
        compiler_params=_cparams(("parallel",)),
        name="rwkv_prep",
    )(p_main, p_lora, ww, wa, wg, vecs, hb)


def _scan_kernel(r_ref, k_ref, v_ref, lw_ref, kk_ref, aa_ref, o_ref, s_scr, *, chunk, hd, gw):
    c_len = chunk
    assert c_len == hd
    bb, tblk, d = r_ref.shape
    n_groups = d // gw
    hpg = gw // hd
    n_double = int(math.log2(c_len)) - 1

    rowi = lax.broadcasted_iota(jnp.int32, (c_len, gw), 0)
    sub = lax.broadcasted_iota(jnp.int32, (c_len, gw), 1) % c_len
    strict = sub < rowi
    incl = sub <= rowi
    eye = jnp.where(sub == rowi, 1.0, 0.0).astype(F32)
    bd_mask = (lax.broadcasted_iota(jnp.int32, (gw, gw), 0) // c_len
               == lax.broadcasted_iota(jnp.int32, (gw, gw), 1) // hd)
    ltri = jnp.where(lax.broadcasted_iota(jnp.int32, (c_len, c_len), 1)
                     <= lax.broadcasted_iota(jnp.int32, (c_len, c_len), 0), 1.0, 0.0).astype(BF16)

    @pl.when(pl.program_id(1) == 0)
    def _():
        s_scr[...] = jnp.zeros_like(s_scr)

    def bd(x16):
        t = jnp.concatenate([x16] * hpg, axis=0)
        return jnp.where(bd_mask, t, jnp.zeros_like(t))

    chains = [(b, g) for b in range(bb) for g in range(n_groups)]
    n = range(len(chains))

    def body(c, carry):
        rows = pl.ds(pl.multiple_of(c * c_len, c_len), c_len)

        def at(ref, i):
            b, g = chains[i]
            return ref.at[b, rows, g * gw:(g + 1) * gw]

        ar, bq, kq, v16, bk, g_tot = [], [], [], [], [], []
        for i in n:
            k = at(k_ref, i)[...]
            lw = at(lw_ref, i)[...]
            kk = at(kk_ref, i)[...]
            b = kk * at(aa_ref, i)[...]
            lh, lm, ll = _split3(lw)
            cum = _dot(ltri, lh) + (_dot(ltri, lm) + _dot(ltri, ll))
            cum_last = cum[c_len - 1:c_len, :]
            g_inv = jnp.exp(-cum)
            g_end = jnp.exp(cum_last - cum)
            aq = -(kk * jnp.exp(cum - lw))
            rq = at(r_ref, i)[...] * jnp.exp(cum)
            ar.append(jnp.concatenate([aq, rq], axis=0).astype(BF16))
            bq.append((b * g_inv).astype(BF16))
            kq.append((k * g_inv).astype(BF16))
            v16.append(at(v_ref, i)[...].astype(BF16))
            bk.append(jnp.concatenate([b * g_end, k * g_end], axis=0).astype(BF16))
            g_tot.append(jnp.exp(cum_last))

        pb = [_dot(ar[i], bd(bq[i]), _NT) for i in n]
        pk = [_dot(ar[i], bd(kq[i]), _NT) for i in n]
        s0 = [s_scr[chains[i]] for i in n]
        ar_s = [_dot(ar[i], s0[i].astype(BF16), _NT) for i in n]
        a_ab = [jnp.where(strict, pb[i][:c_len], 0.0) for i in n]
        a_r = [jnp.concatenate([jnp.where(incl, pb[i][c_len:], 0.0),
                                jnp.where(incl, pk[i][c_len:], 0.0)], axis=1).astype(BF16)
               for i in n]
        v_bd = [bd(v16[i]) for i in n]
        rhs = [ar_s[i][:c_len]
               + _dot(jnp.where(strict, pk[i][:c_len], 0.0).astype(BF16), v_bd[i]) for i in n]

        p = [_dot(a_ab[i].astype(BF16), bd(a_ab[i].astype(BF16))) for i in n]
        x = [eye + a_ab[i] for i in n]
        for _ in range(n_double - 1):
            xp = [_dot(jnp.concatenate([x[i], p[i]], axis=0).astype(BF16), bd(p[i].astype(BF16)))
                  for i in n]
            x = [x[i] + xp[i][:c_len] for i in n]
            p = [xp[i][c_len:] for i in n]
        x = [x[i] + _dot(x[i].astype(BF16), bd(p[i].astype(BF16))) for i in n]

        sa16 = [_dot(x[i].astype(BF16), bd(rhs[i].astype(BF16))).astype(BF16) for i in n]
        for i in n:
            at(o_ref, i)[...] = ar_s[i][c_len:] + _dot(
                a_r[i], jnp.concatenate([bd(sa16[i]), v_bd[i]], axis=0))
        for i in n:
            upd = _dot(jnp.concatenate([sa16[i], v16[i]], axis=0), bk[i], _TN)
            s_scr[chains[i]] = jnp.where(bd_mask, s0[i] * g_tot[i] + upd, 0.0)
        return carry

    lax.fori_loop(0, tblk // c_len, body, 0)


def _rwkv_scan(p_main, k, lw, kk, aa, batch, seq, d, bb, tblk):
    gw = SCAN_GROUP_HEADS * RWKV_HEAD
    spec = pl.BlockSpec((bb, tblk, d), lambda b, t: (b, t, 0))
    col = lambda cb: pl.BlockSpec((bb, tblk, d), lambda b, t: (b, t, cb))
    as3 = lambda a: a.reshape(batch, seq, a.shape[-1])
    out = pl.pallas_call(
        functools.partial(_scan_kernel, chunk=SCAN_CHUNK, hd=RWKV_HEAD, gw=gw),
        grid=(batch // bb, seq // tblk),
        in_specs=[col(2), spec, col(4), spec, spec, spec],
        out_specs=spec,
        out_shape=jax.ShapeDtypeStruct((batch, seq, d), F32),
        scratch_shapes=[pltpu.VMEM((bb, d // gw, gw, gw), F32)],
        compiler_params=_cparams(("parallel", "arbitrary")),
        name="rwkv_scan",
    )(as3(p_main), as3(k), as3(p_main), as3(lw), as3(kk), as3(aa))
    return out.reshape(batch * seq, d)


def _rwkv_post_kernel(o_ref, r_ref, k_ref, v_ref, g_ref, gb_ref, ya_ref, x_ref,
                      vec_ref, hb_ref, wpb_ref, wout_ref, h_ref, *, hd):
    lnx_w = vec_ref[0:1, :]
    lnx_b = vec_ref[1:2, :]
    r_k = vec_ref[2:3, :]
    o = o_ref[...]
    inv_n = 1.0 / hd
    mu = _head_sum(o, hb_ref) * inv_n
    oc = o - mu
    var = _head_sum(oc * oc, hb_ref) * inv_n
    on = (oc * lax.rsqrt(var + GN_EPS)) * lnx_w + lnx_b
    v = v_ref[...]
    bonus = _head_sum((r_ref[...] * k_ref[...]) * r_k, hb_ref) * v
    yb = _dot(((on + bonus) * g_ref[...]).astype(BF16), wpb_ref[...])
    mixed = ya_ref[...] + jax.nn.sigmoid(gb_ref[...]) * yb
    h_ref[...] = x_ref[...] + _dot(mixed.astype(BF16), wout_ref[...])


def _rwkv_post(o, k, g, p_main, ya, x2, vecs, hb, wpb16, wout16, tm, d):
    m = x2.shape[0]
    tile = pl.BlockSpec((tm, d), lambda i: (i, 0))
    col = lambda cb: pl.BlockSpec((tm, d), lambda i: (i, cb))
    full = lambda a: pl.BlockSpec(a.shape, lambda i: (0,) * a.ndim)
    return pl.pallas_call(
        functools.partial(_rwkv_post_kernel, hd=RWKV_HEAD),
        grid=(m // tm,),
        in_specs=[tile, col(2), tile, col(4), tile, col(6),
                  tile, tile, full(vecs), full(hb), full(wpb16), full(wout16)],
        out_specs=tile,
        out_shape=jax.ShapeDtypeStruct((m, d), F32),
        compiler_params=_cparams(("parallel",)),
        name="rwkv_post",
    )(o, p_main, k, p_main, g, p_main, ya, x2, vecs, hb, wpb16, wout16)


def _ffn_kernel(h_ref, g_ref, w1_ref, w2_ref, gf_ref, o_ref, f_scr, acc_scr):
    j = pl.program_id(1)

    @pl.when(j == 0)
    def _():
        f_scr[...] = _rms_rows(h_ref[...], g_ref[...]).astype(BF16)
        acc_scr[...] = jnp.zeros_like(acc_scr)

    t = jnp.maximum(_dot(f_scr[...], w1_ref[...]), 0.0)
    acc_scr[...] += _dot((t * t).astype(BF16), w2_ref[...])

    @pl.when(j == pl.num_programs(1) - 1)
    def _():
        o_ref[...] = _rms_rows(h_ref[...] + acc_scr[...], gf_ref[...])


def _ffn(h, g, w1_16, w2_16, g_final, tm, tf):
    m, d = h.shape
    dff = w1_16.shape[1]
    return pl.pallas_call(
        _ffn_kernel,
        grid=(m // tm, dff // tf),
        in_specs=[
            pl.BlockSpec((tm, d), lambda i, j: (i, 0)),
            pl.BlockSpec((1, d), lambda i, j: (0, 0)),
            pl.BlockSpec((d, tf), lambda i, j: (0, j)),
            pl.BlockSpec((tf, d), lambda i, j: (j, 0)),
            pl.BlockSpec((1, d), lambda i, j: (0, 0)),
        ],
        out_specs=pl.BlockSpec((tm, d), lambda i, j: (i, 0)),
        out_shape=jax.ShapeDtypeStruct((m, d), F32),
        scratch_shapes=[pltpu.VMEM((tm, d), BF16), pltpu.VMEM((tm, d), F32)],
        compiler_params=_cparams(("parallel", "arbitrary")),
        name="ffn",
    )(h, g, w1_16, w2_16, g_final)


def _pad_cols(a, n):
    return jnp.pad(a, ((0, 0), (0, n - a.shape[1])))


def _pad_rows(a, n):
    return jnp.pad(a, ((0, n - a.shape[0]), (0, 0)))


def _layer(x2, batch, seq, g_mix, w_in, sgu_ln_w, sgu_ln_b, sgu_w, sgu_b, w_proj_a, shift_b,
           w_lora_w, w0, a_lora_w, a0, g_lora_w, k_k, k_a, r_k, ln_x_w, ln_x_b, w_proj_b,
           w_out, g_ffn, w_ffn1, w_ffn2, g_out):
    d = x2.shape[1]
    lora_w, lora_a, lora_g = w_lora_w.shape[0], a_lora_w.shape[0], g_lora_w.shape[0]
    c_sgu = 2 * d
    c_rkv = 3 * d
    c_lora = lora_w + lora_a + lora_g
    o_lora = c_sgu + c_rkv
    o_gate = o_lora + c_lora
    pw, pa = LANES, LANES
    pg = -(-lora_g // LANES) * LANES

    w_main = jnp.concatenate([w_in[:, :o_lora], w_in[:, o_gate:]], axis=1).astype(BF16)
    w_lo = jnp.concatenate([
        _pad_cols(w_in[:, o_lora:o_lora + lora_w], pw),
        _pad_cols(w_in[:, o_lora + lora_w:o_lora + lora_w + lora_a], pa),
        _pad_cols(w_in[:, o_lora + lora_w + lora_a:o_gate], pg)], axis=1).astype(BF16)
    sb_lo = shift_b[:, c_rkv:]
    sb_lora = jnp.concatenate([
        _pad_cols(sb_lo[:, :lora_w], pw),
        _pad_cols(sb_lo[:, lora_w:lora_w + lora_a], pa),
        _pad_cols(sb_lo[:, lora_w + lora_a:], pg)], axis=1)
    sb_main = jnp.concatenate([jnp.zeros((2, c_sgu), F32), shift_b[:, :c_rkv],
                               jnp.zeros((2, 2 * d), F32)], axis=1)

    g_mix2 = g_mix.reshape(1, d)
    tn = 1024
    p_main = _in_proj(x2, g_mix2, w_main, sb_main, seq, tm=1024, tn=tn,
                      mix_lo=c_sgu // tn, mix_hi=(c_sgu + c_rkv) // tn)
    p_lora = _in_proj(x2, g_mix2, w_lo, sb_lora, seq, tm=1024, tn=w_lo.shape[1],
                      mix_lo=0, mix_hi=1)

    bias_full = jnp.repeat(sgu_b.T, d // SGU_GROUPS, axis=1)
    ya = _sgu(p_main, sgu_ln_w.reshape(1, d), sgu_ln_b.reshape(1, d), sgu_w, bias_full,
              w_proj_a.astype(BF16), ts=256, d=d)

    gw = SCAN_GROUP_HEADS * RWKV_HEAD
    hb = (lax.broadcasted_iota(jnp.int32, (gw, gw), 0) // RWKV_HEAD
          == lax.broadcasted_iota(jnp.int32, (gw, gw), 1) // RWKV_HEAD).astype(BF16)
    vec_prep = jnp.stack([w0, a0, k_k, k_a])
    k, lw, kk, aa, g = _rwkv_prep(
        p_main, p_lora, _pad_rows(w_lora_w, pw), _pad_rows(a_lora_w, pa),
        _pad_rows(g_lora_w, pg), vec_prep, hb, d, tt=256)

    o = _rwkv_scan(p_main, k, lw, kk, aa, batch, seq, d, bb=4, tblk=128)

    vec_post = jnp.stack([ln_x_w, ln_x_b, r_k])
    h1 = _rwkv_post(o, k, g, p_main, ya, x2, vec_post, hb, w_proj_b.astype(BF16),
                    w_out.astype(BF16), tm=512, d=d)
    return _ffn(h1, g_ffn.reshape(1, d), w_ffn1.astype(BF16), w_ffn2.astype(BF16),
                g_out.reshape(1, d), tm=1024, tf=1024)


def kernel(x, g_mix, w_in, sgu_ln_w, sgu_ln_b, sgu_w, sgu_b, w_proj_a, shift_b, w_lora_w, w0,
           a_lora_w, a0, g_lora_w, k_k, k_a, r_k, ln_x_w, ln_x_b, w_proj_b, w_out, g_ffn,
           w_ffn1, w_ffn2, g_final):
    batch, seq, d = x.shape
    depth = w_in.shape[0]
    assert depth == 1, "the final RMSNorm is fused into the single layer's ffn call"
    h = x.reshape(batch * seq, d)
    l = 0
    h = _layer(h, batch, seq, g_mix[l], w_in[l], sgu_ln_w[l], sgu_ln_b[l], sgu_w[l], sgu_b[l],
               w_proj_a[l], shift_b[l], w_lora_w[l], w0[l], a_lora_w[l], a0[l], g_lora_w[l],
               k_k[l], k_a[l], r_k[l], ln_x_w[l], ln_x_b[l], w_proj_b[l], w_out[l], g_ffn[l],
               w_ffn1[l], w_ffn2[l], g_final)
    return h.reshape(batch, seq, d)
```

```python
import functools
import math

import jax
import jax.numpy as jnp
from jax import lax
from jax.experimental import pallas as pl
from jax.experimental.pallas import tpu as pltpu

F32 = jnp.float32
BF16 = jnp.bfloat16

SGU_CHUNK = 128
SGU_GROUPS = 8
RWKV_HEAD = 64
NORM_EPS = 1e-6
LN_EPS = 1e-5
GN_EPS = 64e-5

LANES = 128
PREV_ROWS = 16
SCAN_CHUNK = 64
SCAN_GROUP_HEADS = 4
VMEM_LIMIT = 56 * 1024 * 1024
SCAN_HILO = dict(ar=False, bq=False, kq=False, v=False, bk=False, s0=False, a_ab=False,
                 a_ak=False, a_r=False, x=False, p=False, rhs=False, sa=False)


def _cparams(sem):
    return pltpu.CompilerParams(dimension_semantics=sem, vmem_limit_bytes=VMEM_LIMIT)


def _dot(a, b, dims=(((1,), (0,)), ((), ()))):
    return lax.dot_general(a, b, dims, preferred_element_type=F32)


_NT = (((1,), (1,)), ((), ()))
_TN = (((0,), (0,)), ((), ()))


def _split2(x):
    hi = x.astype(BF16)
    lo = (x - hi.astype(F32)).astype(BF16)
    return hi, lo


def _split3(x):
    hi = x.astype(BF16)
    r1 = x - hi.astype(F32)
    mid = r1.astype(BF16)
    lo = (r1 - mid.astype(F32)).astype(BF16)
    return hi, mid, lo


def _split1(x):
    return x.astype(BF16), None


def _dot_hp(a2, b2, dims=(((1,), (0,)), ((), ()))):
    ah, al = a2
    bh, bl = b2
    out = _dot(ah, bh, dims)
    if bl is not None:
        out = out + _dot(ah, bl, dims)
    if al is not None:
        out = out + _dot(al, bh, dims)
    return out


def _dot_x2(x, w_exact):
    hi, lo = _split2(x)
    return _dot(hi, w_exact) + _dot(lo, w_exact)


def _head_sum(x, hb_ref, hilo=True):
    w = hb_ref.shape[0]
    hb = hb_ref[...]
    one = _dot_x2 if hilo else (lambda xs, m: _dot(xs.astype(BF16), m))
    return jnp.concatenate(
        [one(x[:, c * w:(c + 1) * w], hb) for c in range(x.shape[1] // w)], axis=1)


def _rms_rows(x, g):
    ms = jnp.mean(x * x, axis=-1, keepdims=True)
    return (x * lax.rsqrt(ms + NORM_EPS)) * g


def _in_proj_kernel(x_ref, xp_ref, g_ref, w_ref, sb_ref, o_ref, a_scr, ap_scr, *,
                    row_chunk, tiles_per_seq, mix_lo, mix_hi):
    i = pl.program_id(0)
    j = pl.program_id(1)

    @pl.when(j == 0)
    def _():
        def body(c, carry):
            rows = pl.ds(pl.multiple_of(c * row_chunk, row_chunk), row_chunk)
            a_scr[rows, :] = _rms_rows(x_ref[rows, :], g_ref[...]).astype(BF16)
            return carry
        lax.fori_loop(0, x_ref.shape[0] // row_chunk, body, 0)
        ap_scr[...] = _rms_rows(xp_ref[...], g_ref[...]).astype(BF16)

    mixed = jnp.logical_and(j >= mix_lo, j < mix_hi)

    @pl.when(mixed)
    def _():
        p = _dot(a_scr[...], w_ref[...])
        pp = _dot(ap_scr[...], w_ref[...])
        first = (i % tiles_per_seq) == 0
        prev_row = jnp.where(first, 0.0, pp[PREV_ROWS - 1:, :])
        row = lax.broadcasted_iota(jnp.int32, p.shape, 0)
        shifted = jnp.where(row == 0, prev_row, pltpu.roll(p, 1, 0))
        o_ref[...] = p * sb_ref[0:1, :] + shifted * sb_ref[1:2, :]

    @pl.when(jnp.logical_not(mixed))
    def _():
        o_ref[...] = _dot(a_scr[...], w_ref[...])


def _in_proj(x2, g, w16, sb, seq, tm, tn, mix_lo, mix_hi):
    m, d = x2.shape
    n = w16.shape[1]
    assert seq % tm == 0
    return pl.pallas_call(
        functools.partial(_in_proj_kernel, row_chunk=256, tiles_per_seq=seq // tm,
                          mix_lo=mix_lo, mix_hi=mix_hi),
        grid=(m // tm, n // tn),
        in_specs=[
            pl.BlockSpec((tm, d), lambda i, j: (i, 0)),
            pl.BlockSpec((PREV_ROWS, d),
                         lambda i, j: (jnp.maximum(i * (tm // PREV_ROWS) - 1, 0), 0)),
            pl.BlockSpec((1, d), lambda i, j: (0, 0)),
            pl.BlockSpec((d, tn), lambda i, j: (0, j)),
            pl.BlockSpec((2, tn), lambda i, j: (0, j)),
        ],
        out_specs=pl.BlockSpec((tm, tn), lambda i, j: (i, j)),
        out_shape=jax.ShapeDtypeStruct((m, n), F32),
        scratch_shapes=[pltpu.VMEM((tm, d), BF16), pltpu.VMEM((PREV_ROWS, d), BF16)],
        compiler_params=_cparams(("parallel", "arbitrary")),
        name="in_proj",
    )(x2, x2, g, w16, sb)


def _gelu(x):
    return 0.5 * x * (1.0 + lax.erf(x * (1.0 / math.sqrt(2.0))))


def _sgu_kernel(u_ref, v_ref, ga_ref, lnw_ref, lnb_ref, ws_ref, bias_ref, wpa_ref, o_ref,
                wm_scr, s_scr):
    ch = SGU_CHUNK
    dg = LANES

    @pl.when(pl.program_id(0) == 0)
    def _():
        row = lax.broadcasted_iota(jnp.int32, (ch, ch), 0)
        col = lax.broadcasted_iota(jnp.int32, (ch, ch), 1)
        for g in range(SGU_GROUPS):
            wm_scr[g] = jnp.where(col <= row, ws_ref[g], 0.0).astype(BF16)

    for c in range(u_ref.shape[0] // ch):
        rows = pl.ds(c * ch, ch)
        zu = _gelu(u_ref[rows, :])
        zv = _gelu(v_ref[rows, :])
        mu = jnp.mean(zv, axis=-1, keepdims=True)
        zc = zv - mu
        var = jnp.mean(zc * zc, axis=-1, keepdims=True)
        vn = ((zc * lax.rsqrt(var + LN_EPS)) * lnw_ref[...] + lnb_ref[...]).astype(BF16)
        for g in range(SGU_GROUPS):
            cols = slice(g * dg, (g + 1) * dg)
            sv = _dot(wm_scr[g], vn[:, cols]) + bias_ref[:, cols]
            s_scr[rows, cols] = (zu[:, cols] * sv).astype(BF16)

    ya = _dot(s_scr[...], wpa_ref[...])
    o_ref[...] = jax.nn.sigmoid(ga_ref[...]) * ya


def _sgu(p_main, ln_w, ln_b, sgu_w, bias_full, wpa16, ts, d):
    m = p_main.shape[0]
    ch = SGU_CHUNK
    return pl.pallas_call(
        _sgu_kernel,
        grid=(m // ts,),
        in_specs=[
            pl.BlockSpec((ts, d), lambda i: (i, 0)),
            pl.BlockSpec((ts, d), lambda i: (i, 1)),
            pl.BlockSpec((ts, d), lambda i: (i, 5)),
            pl.BlockSpec((1, d), lambda i: (0, 0)),
            pl.BlockSpec((1, d), lambda i: (0, 0)),
            pl.BlockSpec((SGU_GROUPS, ch, ch), lambda i: (0, 0, 0)),
            pl.BlockSpec((ch, d), lambda i: (0, 0)),
            pl.BlockSpec((d, d), lambda i: (0, 0)),
        ],
        out_specs=pl.BlockSpec((ts, d), lambda i: (i, 0)),
        out_shape=jax.ShapeDtypeStruct((m, d), F32),
        scratch_shapes=[pltpu.VMEM((SGU_GROUPS, ch, ch), BF16), pltpu.VMEM((ts, d), BF16)],
        compiler_params=_cparams(("arbitrary",)),
        name="sgu",
    )(p_main, p_main, p_main, ln_w, ln_b, sgu_w, bias_full, wpa16)


def _rwkv_prep_kernel(qk_ref, ql_ref, ww_ref, wa_ref, wg_ref, vec_ref, hb_ref,
                      k_out, lw_out, kk_out, aa_out, g_out):
    k = qk_ref[...]
    lo = ql_ref[...]
    xw = lo[:, 0:LANES]
    xa = lo[:, LANES:2 * LANES]
    xg = lo[:, 2 * LANES:]
    w0 = vec_ref[0:1, :]
    a0 = vec_ref[1:2, :]
    k_k = vec_ref[2:3, :]
    k_a = vec_ref[3:4, :]

    zw = w0 + _dot_hp(_split2(jnp.tanh(xw)), _split2(ww_ref[...]))
    lw_out[...] = (-math.exp(-0.5)) * jax.nn.sigmoid(zw)
    aa = jax.nn.sigmoid(a0 + _dot_hp(_split2(xa), _split2(wa_ref[...])))
    g_out[...] = _dot_hp(_split2(jax.nn.sigmoid(xg)), _split2(wg_ref[...])).astype(g_out.dtype)
    kraw = k * k_k
    ss = _head_sum(kraw * kraw, hb_ref)
    kk_out[...] = (kraw * lax.rsqrt(jnp.maximum(ss, 1e-24))).astype(kk_out.dtype)
    k_out[...] = (k * (1.0 + (aa - 1.0) * k_a)).astype(k_out.dtype)
    aa_out[...] = aa.astype(aa_out.dtype)


def _rwkv_prep(p_main, p_lora, ww, wa, wg, vecs, hb, d, tt):
    m = p_main.shape[0]
    lw = p_lora.shape[1]
    full = lambda a: pl.BlockSpec(a.shape, lambda i: (0,) * a.ndim)
    out_spec = pl.BlockSpec((tt, d), lambda i: (i, 0))
    return pl.pallas_call(
        _rwkv_prep_kernel,
        grid=(m // tt,),
        in_specs=[
            pl.BlockSpec((tt, d), lambda i: (i, 3)),
            pl.BlockSpec((tt, lw), lambda i: (i, 0)),
            full(ww), full(wa), full(wg), full(vecs), full(hb),
        ],
        out_specs=[out_spec] * 5,
        out_shape=[jax.ShapeDtypeStruct((m, d), dt) for dt in (BF16, F32, BF16, BF16, BF16)],
        compiler_params=_cparams(("parallel",)),
        name="rwkv_prep",
    )(p_main, p_lora, ww, wa, wg, vecs, hb)


def _scan_kernel(r_ref, k_ref, v_ref, lw_ref, kk_ref, aa_ref, o_ref, s_scr, *, chunk, hd, gw):
    c_len = chunk
    assert c_len == hd
    bb, tblk, d = r_ref.shape
    n_groups = d // gw
    hpg = gw // hd
    n_double = int(math.log2(c_len)) - 1

    rowi = lax.broadcasted_iota(jnp.int32, (c_len, gw), 0)
    sub = lax.broadcasted_iota(jnp.int32, (c_len, gw), 1) % c_len
    strict = sub < rowi
    incl = sub <= rowi
    eye = jnp.where(sub == rowi, 1.0, 0.0).astype(F32)
    bd_mask = (lax.broadcasted_iota(jnp.int32, (gw, gw), 0) // c_len
               == lax.broadcasted_iota(jnp.int32, (gw, gw), 1) // hd)
    ltri = jnp.where(lax.broadcasted_iota(jnp.int32, (c_len, c_len), 1)
                     <= lax.broadcasted_iota(jnp.int32, (c_len, c_len), 0), 1.0, 0.0).astype(BF16)

    @pl.when(pl.program_id(1) == 0)
    def _():
        s_scr[...] = jnp.zeros_like(s_scr)

    def bd(x16):
        t = jnp.concatenate([x16] * hpg, axis=0)
        return jnp.where(bd_mask, t, jnp.zeros_like(t))

    chains = [(b, g) for b in range(bb) for g in range(n_groups)]
    n = range(len(chains))

    def body(c, carry):
        rows = pl.ds(pl.multiple_of(c * c_len, c_len), c_len)

        def at(ref, i):
            b, g = chains[i]
            return ref.at[b, rows, g * gw:(g + 1) * gw]

        ar, bq, kq, v16, bk, g_tot = [], [], [], [], [], []
        for i in n:
            k = at(k_ref, i)[...].astype(F32)
            lw = at(lw_ref, i)[...]
            kk = at(kk_ref, i)[...].astype(F32)
            b = kk * at(aa_ref, i)[...].astype(F32)
            lh, lm, ll = _split3(lw)
            cum = _dot(ltri, lh) + (_dot(ltri, lm) + _dot(ltri, ll))
            cum_last = cum[c_len - 1:c_len, :]
            g_inv = jnp.exp(-cum)
            g_end = jnp.exp(cum_last - cum)
            aq = -(kk * jnp.exp(cum - lw))
            rq = at(r_ref, i)[...] * jnp.exp(cum)
            ar.append(jnp.concatenate([aq, rq], axis=0).astype(BF16))
            bq.append((b * g_inv).astype(BF16))
            kq.append((k * g_inv).astype(BF16))
            v16.append(at(v_ref, i)[...].astype(BF16))
            bk.append(jnp.concatenate([b * g_end, k * g_end], axis=0).astype(BF16))
            g_tot.append(jnp.exp(cum_last))

        pb = [_dot(ar[i], bd(bq[i]), _NT) for i in n]
        pk = [_dot(ar[i], bd(kq[i]), _NT) for i in n]
        s0 = [s_scr[chains[i]] for i in n]
        ar_s = [_dot(ar[i], s0[i].astype(BF16), _NT) for i in n]
        a_ab = [jnp.where(strict, pb[i][:c_len], 0.0) for i in n]
        a_rb = [jnp.where(incl, pb[i][c_len:], 0.0).astype(BF16) for i in n]
        akv = [_dot(jnp.concatenate([jnp.where(strict, pk[i][:c_len], 0.0),
                                     jnp.where(incl, pk[i][c_len:], 0.0)], axis=0).astype(BF16),
                    bd(v16[i])) for i in n]
        rhs = [ar_s[i][:c_len] + akv[i][:c_len] for i in n]

        p = [_dot(a_ab[i].astype(BF16), bd(a_ab[i].astype(BF16))) for i in n]
        x = [eye + a_ab[i] for i in n]
        for _ in range(n_double - 1):
            xp = [_dot(jnp.concatenate([x[i], p[i]], axis=0).astype(BF16), bd(p[i].astype(BF16)))
                  for i in n]
            x = [x[i] + xp[i][:c_len] for i in n]
            p = [xp[i][c_len:] for i in n]
        x = [x[i] + _dot(x[i].astype(BF16), bd(p[i].astype(BF16))) for i in n]

        sa16 = [_dot(x[i].astype(BF16), bd(rhs[i].astype(BF16))).astype(BF16) for i in n]
        for i in n:
            at(o_ref, i)[...] = (ar_s[i][c_len:] + akv[i][c_len:]) + _dot(a_rb[i], bd(sa16[i]))
        for i in n:
            upd = _dot(jnp.concatenate([sa16[i], v16[i]], axis=0), bk[i], _TN)
            s_scr[chains[i]] = jnp.where(bd_mask, s0[i] * g_tot[i] + upd, 0.0)
        return carry

    lax.fori_loop(0, tblk // c_len, body, 0)


def _rwkv_scan(p_main, k, lw, kk, aa, batch, seq, d, bb, tblk):
    gw = SCAN_GROUP_HEADS * RWKV_HEAD
    spec = pl.BlockSpec((bb, tblk, d), lambda b, t: (b, t, 0))
    col = lambda cb: pl.BlockSpec((bb, tblk, d), lambda b, t: (b, t, cb))
    as3 = lambda a: a.reshape(batch, seq, a.shape[-1])
    out = pl.pallas_call(
        functools.partial(_scan_kernel, chunk=SCAN_CHUNK, hd=RWKV_HEAD, gw=gw),
        grid=(batch // bb, seq // tblk),
        in_specs=[col(2), spec, col(4), spec, spec, spec],
        out_specs=spec,
        out_shape=jax.ShapeDtypeStruct((batch, seq, d), F32),
        scratch_shapes=[pltpu.VMEM((bb, d // gw, gw, gw), F32)],
        compiler_params=_cparams(("parallel", "arbitrary")),
        name="rwkv_scan",
    )(as3(p_main), as3(k), as3(p_main), as3(lw), as3(kk), as3(aa))
    return out.reshape(batch * seq, d)


def _rwkv_post_kernel(o_ref, r_ref, k_ref, v_ref, g_ref, gb_ref, ya_ref, x_ref,
                      vec_ref, hb_ref, wpb_ref, wout_ref, h_ref, *, hd):
    lnx_w = vec_ref[0:1, :]
    lnx_b = vec_ref[1:2, :]
    r_k = vec_ref[2:3, :]
    o = o_ref[...]
    inv_n = 1.0 / hd
    mu = _head_sum(o, hb_ref, hilo=False) * inv_n
    oc = o - mu
    var = _head_sum(oc * oc, hb_ref, hilo=False) * inv_n
    on = (oc * lax.rsqrt(var + GN_EPS)) * lnx_w + lnx_b
    v = v_ref[...]
    bonus = _head_sum((r_ref[...] * k_ref[...]) * r_k, hb_ref, hilo=False) * v
    yb = _dot(((on + bonus) * g_ref[...]).astype(BF16), wpb_ref[...])
    mixed = ya_ref[...] + jax.nn.sigmoid(gb_ref[...]) * yb
    h_ref[...] = x_ref[...] + _dot(mixed.astype(BF16), wout_ref[...])


def _rwkv_post(o, k, g, p_main, ya, x2, vecs, hb, wpb16, wout16, tm, d):
    m = x2.shape[0]
    tile = pl.BlockSpec((tm, d), lambda i: (i, 0))
    col = lambda cb: pl.BlockSpec((tm, d), lambda i: (i, cb))
    full = lambda a: pl.BlockSpec(a.shape, lambda i: (0,) * a.ndim)
    return pl.pallas_call(
        functools.partial(_rwkv_post_kernel, hd=RWKV_HEAD),
        grid=(m // tm,),
        in_specs=[tile, col(2), tile, col(4), tile, col(6),
                  tile, tile, full(vecs), full(hb), full(wpb16), full(wout16)],
        out_specs=tile,
        out_shape=jax.ShapeDtypeStruct((m, d), F32),
        compiler_params=_cparams(("parallel",)),
        name="rwkv_post",
    )(o, p_main, k, p_main, g, p_main, ya, x2, vecs, hb, wpb16, wout16)


def _ffn_kernel(h_ref, g_ref, w1_ref, w2_ref, gf_ref, o_ref, f_scr, acc_scr):
    j = pl.program_id(1)

    @pl.when(j == 0)
    def _():
        f_scr[...] = _rms_rows(h_ref[...], g_ref[...]).astype(BF16)
        acc_scr[...] = jnp.zeros_like(acc_scr)

    t = jnp.maximum(_dot(f_scr[...], w1_ref[...]), 0.0)
    acc_scr[...] += _dot((t * t).astype(BF16), w2_ref[...])

    @pl.when(j == pl.num_programs(1) - 1)
    def _():
        o_ref[...] = _rms_rows(h_ref[...] + acc_scr[...], gf_ref[...])


def _ffn(h, g, w1_16, w2_16, g_final, tm, tf):
    m, d = h.shape
    dff = w1_16.shape[1]
    return pl.pallas_call(
        _ffn_kernel,
        grid=(m // tm, dff // tf),
        in_specs=[
            pl.BlockSpec((tm, d), lambda i, j: (i, 0)),
            pl.BlockSpec((1, d), lambda i, j: (0, 0)),
            pl.BlockSpec((d, tf), lambda i, j: (0, j)),
            pl.BlockSpec((tf, d), lambda i, j: (j, 0)),
            pl.BlockSpec((1, d), lambda i, j: (0, 0)),
        ],
        out_specs=pl.BlockSpec((tm, d), lambda i, j: (i, 0)),
        out_shape=jax.ShapeDtypeStruct((m, d), F32),
        scratch_shapes=[pltpu.VMEM((tm, d), BF16), pltpu.VMEM((tm, d), F32)],
        compiler_params=_cparams(("parallel", "arbitrary")),
        name="ffn",
    )(h, g, w1_16, w2_16, g_final)


def _pad_cols(a, n):
    return jnp.pad(a, ((0, 0), (0, n - a.shape[1])))


def _pad_rows(a, n):
    return jnp.pad(a, ((0, n - a.shape[0]), (0, 0)))


def _layer(x2, batch, seq, g_mix, w_in, sgu_ln_w, sgu_ln_b, sgu_w, sgu_b, w_proj_a, shift_b,
           w_lora_w, w0, a_lora_w, a0, g_lora_w, k_k, k_a, r_k, ln_x_w, ln_x_b, w_proj_b,
           w_out, g_ffn, w_ffn1, w_ffn2, g_out):
    d = x2.shape[1]
    lora_w, lora_a, lora_g = w_lora_w.shape[0], a_lora_w.shape[0], g_lora_w.shape[0]
    c_sgu = 2 * d
    c_rkv = 3 * d
    c_lora = lora_w + lora_a + lora_g
    o_lora = c_sgu + c_rkv
    o_gate = o_lora + c_lora
    pw, pa = LANES, LANES
    pg = -(-lora_g // LANES) * LANES

    w_main = jnp.concatenate([w_in[:, :o_lora], w_in[:, o_gate:]], axis=1).astype(BF16)
    w_lo = jnp.concatenate([
        _pad_cols(w_in[:, o_lora:o_lora + lora_w], pw),
        _pad_cols(w_in[:, o_lora + lora_w:o_lora + lora_w + lora_a], pa),
        _pad_cols(w_in[:, o_lora + lora_w + lora_a:o_gate], pg)], axis=1).astype(BF16)
    sb_lo = shift_b[:, c_rkv:]
    sb_lora = jnp.concatenate([
        _pad_cols(sb_lo[:, :lora_w], pw),
        _pad_cols(sb_lo[:, lora_w:lora_w + lora_a], pa),
        _pad_cols(sb_lo[:, lora_w + lora_a:], pg)], axis=1)
    sb_main = jnp.concatenate([jnp.zeros((2, c_sgu), F32), shift_b[:, :c_rkv],
                               jnp.zeros((2, 2 * d), F32)], axis=1)

    g_mix2 = g_mix.reshape(1, d)
    tn = 1024
    p_main = _in_proj(x2, g_mix2, w_main, sb_main, seq, tm=1024, tn=tn,
                      mix_lo=c_sgu // tn, mix_hi=(c_sgu + c_rkv) // tn)
    p_lora = _in_proj(x2, g_mix2, w_lo, sb_lora, seq, tm=1024, tn=w_lo.shape[1],
                      mix_lo=0, mix_hi=1)

    bias_full = jnp.repeat(sgu_b.T, d // SGU_GROUPS, axis=1)
    ya = _sgu(p_main, sgu_ln_w.reshape(1, d), sgu_ln_b.reshape(1, d), sgu_w, bias_full,
              w_proj_a.astype(BF16), ts=256, d=d)

    gw = SCAN_GROUP_HEADS * RWKV_HEAD
    hb = (lax.broadcasted_iota(jnp.int32, (gw, gw), 0) // RWKV_HEAD
          == lax.broadcasted_iota(jnp.int32, (gw, gw), 1) // RWKV_HEAD).astype(BF16)
    vec_prep = jnp.stack([w0, a0, k_k, k_a])
    k, lw, kk, aa, g = _rwkv_prep(
        p_main, p_lora, _pad_rows(w_lora_w, pw), _pad_rows(a_lora_w, pa),
        _pad_rows(g_lora_w, pg), vec_prep, hb, d, tt=256)

    o = _rwkv_scan(p_main, k, lw, kk, aa, batch, seq, d, bb=4, tblk=128)

    vec_post = jnp.stack([ln_x_w, ln_x_b, r_k])
    h1 = _rwkv_post(o, k, g, p_main, ya, x2, vec_post, hb, w_proj_b.astype(BF16),
                    w_out.astype(BF16), tm=512, d=d)
    return _ffn(h1, g_ffn.reshape(1, d), w_ffn1.astype(BF16), w_ffn2.astype(BF16),
                g_out.reshape(1, d), tm=1024, tf=1024)


def kernel(x, g_mix, w_in, sgu_ln_w, sgu_ln_b, sgu_w, sgu_b, w_proj_a, shift_b, w_lora_w, w0,
           a_lora_w, a0, g_lora_w, k_k, k_a, r_k, ln_x_w, ln_x_b, w_proj_b, w_out, g_ffn,
           w_ffn1, w_ffn2, g_final):
    batch, seq, d = x.shape
    depth = w_in.shape[0]
    assert depth == 1, "the final RMSNorm is fused into the single layer's ffn call"
    h = x.reshape(batch * seq, d)
    l = 0
    h = _layer(h, batch, seq, g_mix[l], w_in[l], sgu_ln_w[l], sgu_ln_b[l], sgu_w[l], sgu_b[l],
               w_proj_a[l], shift_b[l], w_lora_w[l], w0[l], a_lora_w[l], a0[l], g_lora_w[l],
               k_k[l], k_a[l], r_k[l], ln_x_w[l], ln_x_b[l], w_proj_b[l], w_out[l], g_ffn[l],
               w_ffn1[l], w_ffn2[l], g_final)
    return h.reshape(batch, seq, d)
```

```python
import functools
import math

import jax
import jax.numpy as jnp
from jax import lax
from jax.experimental import pallas as pl
from jax.experimental.pallas import tpu as pltpu

F32 = jnp.float32
BF16 = jnp.bfloat16

SGU_CHUNK = 128
SGU_GROUPS = 8
RWKV_HEAD = 64
NORM_EPS = 1e-6
LN_EPS = 1e-5
GN_EPS = 64e-5

LANES = 128
PREV_ROWS = 16
SCAN_CHUNK = 64
SCAN_GROUP_HEADS = 4
VMEM_LIMIT = 56 * 1024 * 1024
SCAN_HILO = dict(ar=False, bq=False, kq=False, v=False, bk=False, s0=False, a_ab=False,
                 a_ak=False, a_r=False, x=False, p=False, rhs=False, sa=False)


def _cparams(sem):
    return pltpu.CompilerParams(dimension_semantics=sem, vmem_limit_bytes=VMEM_LIMIT)


def _dot(a, b, dims=(((1,), (0,)), ((), ()))):
    return lax.dot_general(a, b, dims, preferred_element_type=F32)


_NT = (((1,), (1,)), ((), ()))
_TN = (((0,), (0,)), ((), ()))


def _split2(x):
    hi = x.astype(BF16)
    lo = (x - hi.astype(F32)).astype(BF16)
    return hi, lo


def _split3(x):
    hi = x.astype(BF16)
    r1 = x - hi.astype(F32)
    mid = r1.astype(BF16)
    lo = (r1 - mid.astype(F32)).astype(BF16)
    return hi, mid, lo


def _split1(x):
    return x.astype(BF16), None


def _dot_hp(a2, b2, dims=(((1,), (0,)), ((), ()))):
    ah, al = a2
    bh, bl = b2
    out = _dot(ah, bh, dims)
    if bl is not None:
        out = out + _dot(ah, bl, dims)
    if al is not None:
        out = out + _dot(al, bh, dims)
    return out


def _dot_x2(x, w_exact):
    hi, lo = _split2(x)
    return _dot(hi, w_exact) + _dot(lo, w_exact)


def _head_sum(x, hb_ref, hilo=True):
    w = hb_ref.shape[0]
    hb = hb_ref[...]
    one = _dot_x2 if hilo else (lambda xs, m: _dot(xs.astype(BF16), m))
    return jnp.concatenate(
        [one(x[:, c * w:(c + 1) * w], hb) for c in range(x.shape[1] // w)], axis=1)


def _rms_rows(x, g):
    ms = jnp.mean(x * x, axis=-1, keepdims=True)
    return (x * lax.rsqrt(ms + NORM_EPS)) * g


def _gelu(x):
    return 0.5 * x * (1.0 + lax.erf(x * (1.0 / math.sqrt(2.0))))


def _norm_tile(x_ref, xp_ref, g_ref, a_scr, ap_scr, row_chunk):
    def body(c, carry):
        rows = pl.ds(pl.multiple_of(c * row_chunk, row_chunk), row_chunk)
        a_scr[rows, :] = _rms_rows(x_ref[rows, :], g_ref[...]).astype(BF16)
        return carry
    lax.fori_loop(0, x_ref.shape[0] // row_chunk, body, 0)
    ap_scr[...] = _rms_rows(xp_ref[...], g_ref[...]).astype(BF16)


def _shift_mix_tile(a_scr, ap_scr, w_ref, sb_ref, first):
    p = _dot(a_scr[...], w_ref[...])
    pp = _dot(ap_scr[...], w_ref[...])
    prev_row = jnp.where(first, 0.0, pp[PREV_ROWS - 1:, :])
    row = lax.broadcasted_iota(jnp.int32, p.shape, 0)
    shifted = jnp.where(row == 0, prev_row, pltpu.roll(p, 1, 0))
    return p * sb_ref[0:1, :] + shifted * sb_ref[1:2, :]


def _in_proj_kernel(x_ref, xp_ref, g_ref, w_ref, sb_ref, lnw_ref, lnb_ref,
                    zu_ref, vn_ref, p_ref, a_scr, ap_scr, *, row_chunk, tiles_per_seq, n_mix):
    i = pl.program_id(0)
    j = pl.program_id(1)

    @pl.when(j == 0)
    def _():
        _norm_tile(x_ref, xp_ref, g_ref, a_scr, ap_scr, row_chunk)
        zu_ref[...] = _gelu(_dot(a_scr[...], w_ref[...])).astype(BF16)

    @pl.when(j == 1)
    def _():
        zv = _gelu(_dot(a_scr[...], w_ref[...]))
        mu = jnp.mean(zv, axis=-1, keepdims=True)
        zc = zv - mu
        var = jnp.mean(zc * zc, axis=-1, keepdims=True)
        vn_ref[...] = ((zc * lax.rsqrt(var + LN_EPS)) * lnw_ref[...] + lnb_ref[...]).astype(BF16)

    @pl.when(jnp.logical_and(j >= 2, j < 2 + n_mix))
    def _():
        p_ref[...] = _shift_mix_tile(a_scr, ap_scr, w_ref, sb_ref, (i % tiles_per_seq) == 0)

    @pl.when(j >= 2 + n_mix)
    def _():
        p_ref[...] = _dot(a_scr[...], w_ref[...])


def _in_proj(x2, g, w16, sb, ln_w, ln_b, seq, tm, n_mix):
    m, d = x2.shape
    n = w16.shape[1]
    assert seq % tm == 0 and n % d == 0
    row = lambda r: pl.BlockSpec((r, d), lambda i, j: (0, 0))
    return pl.pallas_call(
        functools.partial(_in_proj_kernel, row_chunk=256, tiles_per_seq=seq // tm, n_mix=n_mix),
        grid=(m // tm, n // d),
        in_specs=[
            pl.BlockSpec((tm, d), lambda i, j: (i, 0)),
            pl.BlockSpec((PREV_ROWS, d),
                         lambda i, j: (jnp.maximum(i * (tm // PREV_ROWS) - 1, 0), 0)),
            row(1),
            pl.BlockSpec((d, d), lambda i, j: (0, j)),
            pl.BlockSpec((2, d), lambda i, j: (0, j)),
            row(1), row(1),
        ],
        out_specs=[
            pl.BlockSpec((tm, d), lambda i, j: (i, 0)),
            pl.BlockSpec((tm, d), lambda i, j: (i, 0)),
            pl.BlockSpec((tm, d), lambda i, j: (i, jnp.maximum(j - 2, 0))),
        ],
        out_shape=[jax.ShapeDtypeStruct((m, d), BF16), jax.ShapeDtypeStruct((m, d), BF16),
                   jax.ShapeDtypeStruct((m, n - 2 * d), F32)],
        scratch_shapes=[pltpu.VMEM((tm, d), BF16), pltpu.VMEM((PREV_ROWS, d), BF16)],
        compiler_params=_cparams(("parallel", "arbitrary")),
        name="in_proj",
    )(x2, x2, g, w16, sb, ln_w, ln_b)


def _lora_proj_kernel(x_ref, xp_ref, g_ref, w_ref, sb_ref, o_ref, a_scr, ap_scr, *,
                      row_chunk, tiles_per_seq):
    _norm_tile(x_ref, xp_ref, g_ref, a_scr, ap_scr, row_chunk)
    o_ref[...] = _shift_mix_tile(a_scr, ap_scr, w_ref, sb_ref,
                                 (pl.program_id(0) % tiles_per_seq) == 0)


def _lora_proj(x2, g, w16, sb, seq, tm):
    m, d = x2.shape
    n = w16.shape[1]
    assert seq % tm == 0
    return pl.pallas_call(
        functools.partial(_lora_proj_kernel, row_chunk=256, tiles_per_seq=seq // tm),
        grid=(m // tm,),
        in_specs=[
            pl.BlockSpec((tm, d), lambda i: (i, 0)),
            pl.BlockSpec((PREV_ROWS, d), lambda i: (jnp.maximum(i * (tm // PREV_ROWS) - 1, 0), 0)),
            pl.BlockSpec((1, d), lambda i: (0, 0)),
            pl.BlockSpec((d, n), lambda i: (0, 0)),
            pl.BlockSpec((2, n), lambda i: (0, 0)),
        ],
        out_specs=pl.BlockSpec((tm, n), lambda i: (i, 0)),
        out_shape=jax.ShapeDtypeStruct((m, n), F32),
        scratch_shapes=[pltpu.VMEM((tm, d), BF16), pltpu.VMEM((PREV_ROWS, d), BF16)],
        compiler_params=_cparams(("parallel",)),
        name="lora_proj",
    )(x2, x2, g, w16, sb)


def _sgu_kernel(zu_ref, vn_ref, ga_ref, ws_ref, bias_ref, wpa_ref, o_ref, wm_scr, s_scr):
    ch = SGU_CHUNK
    dg = LANES

    @pl.when(pl.program_id(0) == 0)
    def _():
        row = lax.broadcasted_iota(jnp.int32, (ch, ch), 0)
        col = lax.broadcasted_iota(jnp.int32, (ch, ch), 1)
        for g in range(SGU_GROUPS):
            wm_scr[g] = jnp.where(col <= row, ws_ref[g], 0.0).astype(BF16)

    for c in range(zu_ref.shape[0] // ch):
        rows = pl.ds(c * ch, ch)
        for g in range(SGU_GROUPS):
            cols = slice(g * dg, (g + 1) * dg)
            sv = _dot(wm_scr[g], vn_ref[rows, cols]) + bias_ref[:, cols]
            s_scr[rows, cols] = (zu_ref[rows, cols] * sv).astype(BF16)

    ya = _dot(s_scr[...], wpa_ref[...])
    o_ref[...] = jax.nn.sigmoid(ga_ref[...]) * ya


def _sgu(zu, vn, p_main, sgu_w, bias_full, wpa16, ts, d):
    m = zu.shape[0]
    ch = SGU_CHUNK
    tile = pl.BlockSpec((ts, d), lambda i: (i, 0))
    return pl.pallas_call(
        _sgu_kernel,
        grid=(m // ts,),
        in_specs=[
            tile, tile,
            pl.BlockSpec((ts, d), lambda i: (i, 3)),
            pl.BlockSpec((SGU_GROUPS, ch, ch), lambda i: (0, 0, 0)),
            pl.BlockSpec((ch, d), lambda i: (0, 0)),
            pl.BlockSpec((d, d), lambda i: (0, 0)),
        ],
        out_specs=tile,
        out_shape=jax.ShapeDtypeStruct((m, d), F32),
        scratch_shapes=[pltpu.VMEM((SGU_GROUPS, ch, ch), BF16), pltpu.VMEM((ts, d), BF16)],
        compiler_params=_cparams(("arbitrary",)),
        name="sgu",
    )(zu, vn, p_main, sgu_w, bias_full, wpa16)


def _rwkv_prep_kernel(qk_ref, ql_ref, ww_ref, wa_ref, wg_ref, vec_ref, hb_ref,
                      k_out, lw_out, kk_out, aa_out, g_out):
    k = qk_ref[...]
    lo = ql_ref[...]
    xw = lo[:, 0:LANES]
    xa = lo[:, LANES:2 * LANES]
    xg = lo[:, 2 * LANES:]
    w0 = vec_ref[0:1, :]
    a0 = vec_ref[1:2, :]
    k_k = vec_ref[2:3, :]
    k_a = vec_ref[3:4, :]

    zw = w0 + _dot_hp(_split2(jnp.tanh(xw)), _split2(ww_ref[...]))
    lw_out[...] = (-math.exp(-0.5)) * jax.nn.sigmoid(zw)
    aa = jax.nn.sigmoid(a0 + _dot_hp(_split2(xa), _split2(wa_ref[...])))
    g_out[...] = _dot_hp(_split2(jax.nn.sigmoid(xg)), _split2(wg_ref[...])).astype(g_out.dtype)
    kraw = k * k_k
    ss = _head_sum(kraw * kraw, hb_ref)
    kk_out[...] = (kraw * lax.rsqrt(jnp.maximum(ss, 1e-24))).astype(kk_out.dtype)
    k_out[...] = (k * (1.0 + (aa - 1.0) * k_a)).astype(k_out.dtype)
    aa_out[...] = aa.astype(aa_out.dtype)


def _rwkv_prep(p_main, p_lora, ww, wa, wg, vecs, hb, d, tt):
    m = p_main.shape[0]
    lw = p_lora.shape[1]
    full = lambda a: pl.BlockSpec(a.shape, lambda i: (0,) * a.ndim)
    out_spec = pl.BlockSpec((tt, d), lambda i: (i, 0))
    return pl.pallas_call(
        _rwkv_prep_kernel,
        grid=(m // tt,),
        in_specs=[
            pl.BlockSpec((tt, d), lambda i: (i, 1)),
            pl.BlockSpec((tt, lw), lambda i: (i, 0)),
            full(ww), full(wa), full(wg), full(vecs), full(hb),
        ],
        out_specs=[out_spec] * 5,
        out_shape=[jax.ShapeDtypeStruct((m, d), dt) for dt in (BF16, F32, BF16, BF16, BF16)],
        compiler_params=_cparams(("parallel",)),
        name="rwkv_prep",
    )(p_main, p_lora, ww, wa, wg, vecs, hb)


def _scan_kernel(r_ref, k_ref, v_ref, lw_ref, kk_ref, aa_ref, o_ref, s_scr, *, chunk, hd, gw):
    c_len = chunk
    assert c_len == hd
    bb, tblk, d = r_ref.shape
    n_groups = d // gw
    hpg = gw // hd
    n_double = int(math.log2(c_len)) - 1

    rowi = lax.broadcasted_iota(jnp.int32, (c_len, gw), 0)
    sub = lax.broadcasted_iota(jnp.int32, (c_len, gw), 1) % c_len
    strict = sub < rowi
    incl = sub <= rowi
    eye = jnp.where(sub == rowi, 1.0, 0.0).astype(F32)
    bd_mask = (lax.broadcasted_iota(jnp.int32, (gw, gw), 0) // c_len
               == lax.broadcasted_iota(jnp.int32, (gw, gw), 1) // hd)
    ltri = jnp.where(lax.broadcasted_iota(jnp.int32, (c_len, c_len), 1)
                     <= lax.broadcasted_iota(jnp.int32, (c_len, c_len), 0), 1.0, 0.0).astype(BF16)

    @pl.when(pl.program_id(1) == 0)
    def _():
        s_scr[...] = jnp.zeros_like(s_scr)

    def bd(x16):
        t = jnp.concatenate([x16] * hpg, axis=0)
        return jnp.where(bd_mask, t, jnp.zeros_like(t))

    chains = [(b, g) for b in range(bb) for g in range(n_groups)]
    n = range(len(chains))

    def body(c, carry):
        rows = pl.ds(pl.multiple_of(c * c_len, c_len), c_len)

        def at(ref, i):
            b, g = chains[i]
            return ref.at[b, rows, g * gw:(g + 1) * gw]

        ar, bq, kq, v16, bk, g_tot = [], [], [], [], [], []
        for i in n:
            k = at(k_ref, i)[...].astype(F32)
            lw = at(lw_ref, i)[...]
            kk = at(kk_ref, i)[...].astype(F32)
            b = kk * at(aa_ref, i)[...].astype(F32)
            lh, lm, ll = _split3(lw)
            cum = _dot(ltri, lh) + (_dot(ltri, lm) + _dot(ltri, ll))
            cum_last = cum[c_len - 1:c_len, :]
            g_inv = jnp.exp(-cum)
            g_end = jnp.exp(cum_last - cum)
            aq = -(kk * jnp.exp(cum - lw))
            rq = at(r_ref, i)[...] * jnp.exp(cum)
            ar.append(jnp.concatenate([aq, rq], axis=0).astype(BF16))
            bq.append((b * g_inv).astype(BF16))
            kq.append((k * g_inv).astype(BF16))
            v16.append(at(v_ref, i)[...].astype(BF16))
            bk.append(jnp.concatenate([b * g_end, k * g_end], axis=0).astype(BF16))
            g_tot.append(jnp.exp(cum_last))

        pb = [_dot(ar[i], bd(bq[i]), _NT) for i in n]
        pk = [_dot(ar[i], bd(kq[i]), _NT) for i in n]
        s0 = [s_scr[chains[i]] for i in n]
        ar_s = [_dot(ar[i], s0[i].astype(BF16), _NT) for i in n]
        a_ab = [jnp.where(strict, pb[i][:c_len], 0.0) for i in n]
        a_rb = [jnp.where(incl, pb[i][c_len:], 0.0).astype(BF16) for i in n]
        akv = [_dot(jnp.concatenate([jnp.where(strict, pk[i][:c_len], 0.0),
                                     jnp.where(incl, pk[i][c_len:], 0.0)], axis=0).astype(BF16),
                    bd(v16[i])) for i in n]
        rhs = [ar_s[i][:c_len] + akv[i][:c_len] for i in n]

        p = [_dot(a_ab[i].astype(BF16), bd(a_ab[i].astype(BF16))) for i in n]
        x = [eye + a_ab[i] for i in n]
        for _ in range(n_double - 1):
            xp = [_dot(jnp.concatenate([x[i], p[i]], axis=0).astype(BF16), bd(p[i].astype(BF16)))
                  for i in n]
            x = [x[i] + xp[i][:c_len] for i in n]
            p = [xp[i][c_len:] for i in n]
        x = [x[i] + _dot(x[i].astype(BF16), bd(p[i].astype(BF16))) for i in n]

        sa16 = [_dot(x[i].astype(BF16), bd(rhs[i].astype(BF16))).astype(BF16) for i in n]
        for i in n:
            at(o_ref, i)[...] = (ar_s[i][c_len:] + akv[i][c_len:]) + _dot(a_rb[i], bd(sa16[i]))
        for i in n:
            upd = _dot(jnp.concatenate([sa16[i], v16[i]], axis=0), bk[i], _TN)
            s_scr[chains[i]] = jnp.where(bd_mask, s0[i] * g_tot[i] + upd, 0.0)
        return carry

    lax.fori_loop(0, tblk // c_len, body, 0)


def _rwkv_scan(p_main, k, lw, kk, aa, batch, seq, d, bb, tblk):
    gw = SCAN_GROUP_HEADS * RWKV_HEAD
    spec = pl.BlockSpec((bb, tblk, d), lambda b, t: (b, t, 0))
    col = lambda cb: pl.BlockSpec((bb, tblk, d), lambda b, t: (b, t, cb))
    as3 = lambda a: a.reshape(batch, seq, a.shape[-1])
    out = pl.pallas_call(
        functools.partial(_scan_kernel, chunk=SCAN_CHUNK, hd=RWKV_HEAD, gw=gw),
        grid=(batch // bb, seq // tblk),
        in_specs=[col(0), spec, col(2), spec, spec, spec],
        out_specs=spec,
        out_shape=jax.ShapeDtypeStruct((batch, seq, d), F32),
        scratch_shapes=[pltpu.VMEM((bb, d // gw, gw, gw), F32)],
        compiler_params=_cparams(("parallel", "arbitrary")),
        name="rwkv_scan",
    )(as3(p_main), as3(k), as3(p_main), as3(lw), as3(kk), as3(aa))
    return out.reshape(batch * seq, d)


def _rwkv_post_kernel(o_ref, r_ref, k_ref, v_ref, g_ref, gb_ref, ya_ref, x_ref,
                      vec_ref, hb_ref, wpb_ref, wout_ref, h_ref, *, hd):
    lnx_w = vec_ref[0:1, :]
    lnx_b = vec_ref[1:2, :]
    r_k = vec_ref[2:3, :]
    o = o_ref[...]
    inv_n = 1.0 / hd
    mu = _head_sum(o, hb_ref, hilo=False) * inv_n
    oc = o - mu
    var = _head_sum(oc * oc, hb_ref, hilo=False) * inv_n
    on = (oc * lax.rsqrt(var + GN_EPS)) * lnx_w + lnx_b
    v = v_ref[...]
    bonus = _head_sum((r_ref[...] * k_ref[...]) * r_k, hb_ref, hilo=False) * v
    yb = _dot(((on + bonus) * g_ref[...]).astype(BF16), wpb_ref[...])
    mixed = ya_ref[...] + jax.nn.sigmoid(gb_ref[...]) * yb
    h_ref[...] = x_ref[...] + _dot(mixed.astype(BF16), wout_ref[...])


def _rwkv_post(o, k, g, p_main, ya, x2, vecs, hb, wpb16, wout16, tm, d):
    m = x2.shape[0]
    tile = pl.BlockSpec((tm, d), lambda i: (i, 0))
    col = lambda cb: pl.BlockSpec((tm, d), lambda i: (i, cb))
    full = lambda a: pl.BlockSpec(a.shape, lambda i: (0,) * a.ndim)
    return pl.pallas_call(
        functools.partial(_rwkv_post_kernel, hd=RWKV_HEAD),
        grid=(m // tm,),
        in_specs=[tile, col(0), tile, col(2), tile, col(4),
                  tile, tile, full(vecs), full(hb), full(wpb16), full(wout16)],
        out_specs=tile,
        out_shape=jax.ShapeDtypeStruct((m, d), F32),
        compiler_params=_cparams(("parallel",)),
        name="rwkv_post",
    )(o, p_main, k, p_main, g, p_main, ya, x2, vecs, hb, wpb16, wout16)


def _ffn_kernel(h_ref, g_ref, w1_ref, w2_ref, gf_ref, o_ref, f_scr, acc_scr):
    j = pl.program_id(1)

    @pl.when(j == 0)
    def _():
        f_scr[...] = _rms_rows(h_ref[...], g_ref[...]).astype(BF16)
        acc_scr[...] = jnp.zeros_like(acc_scr)

    t = jnp.maximum(_dot(f_scr[...], w1_ref[...]), 0.0)
    acc_scr[...] += _dot((t * t).astype(BF16), w2_ref[...])

    @pl.when(j == pl.num_programs(1) - 1)
    def _():
        o_ref[...] = _rms_rows(h_ref[...] + acc_scr[...], gf_ref[...])


def _ffn(h, g, w1_16, w2_16, g_final, tm, tf):
    m, d = h.shape
    dff = w1_16.shape[1]
    return pl.pallas_call(
        _ffn_kernel,
        grid=(m // tm, dff // tf),
        in_specs=[
            pl.BlockSpec((tm, d), lambda i, j: (i, 0)),
            pl.BlockSpec((1, d), lambda i, j: (0, 0)),
            pl.BlockSpec((d, tf), lambda i, j: (0, j)),
            pl.BlockSpec((tf, d), lambda i, j: (j, 0)),
            pl.BlockSpec((1, d), lambda i, j: (0, 0)),
        ],
        out_specs=pl.BlockSpec((tm, d), lambda i, j: (i, 0)),
        out_shape=jax.ShapeDtypeStruct((m, d), F32),
        scratch_shapes=[pltpu.VMEM((tm, d), BF16), pltpu.VMEM((tm, d), F32)],
        compiler_params=_cparams(("parallel", "arbitrary")),
        name="ffn",
    )(h, g, w1_16, w2_16, g_final)


def _pad_cols(a, n):
    return jnp.pad(a, ((0, 0), (0, n - a.shape[1])))


def _pad_rows(a, n):
    return jnp.pad(a, ((0, n - a.shape[0]), (0, 0)))


def _layer(x2, batch, seq, g_mix, w_in, sgu_ln_w, sgu_ln_b, sgu_w, sgu_b, w_proj_a, shift_b,
           w_lora_w, w0, a_lora_w, a0, g_lora_w, k_k, k_a, r_k, ln_x_w, ln_x_b, w_proj_b,
           w_out, g_ffn, w_ffn1, w_ffn2, g_out):
    d = x2.shape[1]
    lora_w, lora_a, lora_g = w_lora_w.shape[0], a_lora_w.shape[0], g_lora_w.shape[0]
    c_sgu = 2 * d
    c_rkv = 3 * d
    c_lora = lora_w + lora_a + lora_g
    o_lora = c_sgu + c_rkv
    o_gate = o_lora + c_lora
    pw, pa = LANES, LANES
    pg = -(-lora_g // LANES) * LANES

    w_main = jnp.concatenate([w_in[:, :o_lora], w_in[:, o_gate:]], axis=1).astype(BF16)
    w_lo = jnp.concatenate([
        _pad_cols(w_in[:, o_lora:o_lora + lora_w], pw),
        _pad_cols(w_in[:, o_lora + lora_w:o_lora + lora_w + lora_a], pa),
        _pad_cols(w_in[:, o_lora + lora_w + lora_a:o_gate], pg)], axis=1).astype(BF16)
    sb_lo = shift_b[:, c_rkv:]
    sb_lora = jnp.concatenate([
        _pad_cols(sb_lo[:, :lora_w], pw),
        _pad_cols(sb_lo[:, lora_w:lora_w + lora_a], pa),
        _pad_cols(sb_lo[:, lora_w + lora_a:], pg)], axis=1)
    sb_main = jnp.concatenate([jnp.zeros((2, c_sgu), F32), shift_b[:, :c_rkv],
                               jnp.zeros((2, 2 * d), F32)], axis=1)

    g_mix2 = g_mix.reshape(1, d)
    zu, vn, p_main = _in_proj(x2, g_mix2, w_main, sb_main, sgu_ln_w.reshape(1, d),
                              sgu_ln_b.reshape(1, d), seq, tm=1024, n_mix=c_rkv // d)
    p_lora = _lora_proj(x2, g_mix2, w_lo, sb_lora, seq, tm=1024)

    bias_full = jnp.repeat(sgu_b.T, d // SGU_GROUPS, axis=1)
    ya = _sgu(zu, vn, p_main, sgu_w, bias_full, w_proj_a.astype(BF16), ts=256, d=d)

    gw = SCAN_GROUP_HEADS * RWKV_HEAD
    hb = (lax.broadcasted_iota(jnp.int32, (gw, gw), 0) // RWKV_HEAD
          == lax.broadcasted_iota(jnp.int32, (gw, gw), 1) // RWKV_HEAD).astype(BF16)
    vec_prep = jnp.stack([w0, a0, k_k, k_a])
    k, lw, kk, aa, g = _rwkv_prep(
        p_main, p_lora, _pad_rows(w_lora_w, pw), _pad_rows(a_lora_w, pa),
        _pad_rows(g_lora_w, pg), vec_prep, hb, d, tt=256)

    o = _rwkv_scan(p_main, k, lw, kk, aa, batch, seq, d, bb=4, tblk=128)

    vec_post = jnp.stack([ln_x_w, ln_x_b, r_k])
    h1 = _rwkv_post(o, k, g, p_main, ya, x2, vec_post, hb, w_proj_b.astype(BF16),
                    w_out.astype(BF16), tm=512, d=d)
    return _ffn(h1, g_ffn.reshape(1, d), w_ffn1.astype(BF16), w_ffn2.astype(BF16),
                g_out.reshape(1, d), tm=1024, tf=1024)


def kernel(x, g_mix, w_in, sgu_ln_w, sgu_ln_b, sgu_w, sgu_b, w_proj_a, shift_b, w_lora_w, w0,
           a_lora_w, a0, g_lora_w, k_k, k_a, r_k, ln_x_w, ln_x_b, w_proj_b, w_out, g_ffn,
           w_ffn1, w_ffn2, g_final):
    batch, seq, d = x.shape
    depth = w_in.shape[0]
    assert depth == 1, "the final RMSNorm is fused into the single layer's ffn call"
    h = x.reshape(batch * seq, d)
    l = 0
    h = _layer(h, batch, seq, g_mix[l], w_in[l], sgu_ln_w[l], sgu_ln_b[l], sgu_w[l], sgu_b[l],
               w_proj_a[l], shift_b[l], w_lora_w[l], w0[l], a_lora_w[l], a0[l], g_lora_w[l],
               k_k[l], k_a[l], r_k[l], ln_x_w[l], ln_x_b[l], w_proj_b[l], w_out[l], g_ffn[l],
               w_ffn1[l], w_ffn2[l], g_final)
    return h.reshape(batch, seq, d)
```

```python
import functools
import math

import jax
import jax.numpy as jnp
from jax import lax
from jax.experimental import pallas as pl
from jax.experimental.pallas import tpu as pltpu

F32 = jnp.float32
BF16 = jnp.bfloat16

SGU_CHUNK = 128
SGU_GROUPS = 8
RWKV_HEAD = 64
NORM_EPS = 1e-6
LN_EPS = 1e-5
GN_EPS = 64e-5

LANES = 128
PREV_ROWS = 16
SCAN_CHUNK = 64
SCAN_GROUP_HEADS = 4
VMEM_LIMIT = 56 * 1024 * 1024
SCAN_HILO = dict(ar=False, bq=False, kq=False, v=False, bk=False, s0=False, a_ab=False,
                 a_ak=False, a_r=False, x=False, p=False, rhs=False, sa=False)


def _cparams(sem):
    return pltpu.CompilerParams(dimension_semantics=sem, vmem_limit_bytes=VMEM_LIMIT)


def _dot(a, b, dims=(((1,), (0,)), ((), ()))):
    return lax.dot_general(a, b, dims, preferred_element_type=F32)


_NT = (((1,), (1,)), ((), ()))
_TN = (((0,), (0,)), ((), ()))


def _split2(x):
    hi = x.astype(BF16)
    lo = (x - hi.astype(F32)).astype(BF16)
    return hi, lo


def _split3(x):
    hi = x.astype(BF16)
    r1 = x - hi.astype(F32)
    mid = r1.astype(BF16)
    lo = (r1 - mid.astype(F32)).astype(BF16)
    return hi, mid, lo


def _split1(x):
    return x.astype(BF16), None


def _dot_hp(a2, b2, dims=(((1,), (0,)), ((), ()))):
    ah, al = a2
    bh, bl = b2
    out = _dot(ah, bh, dims)
    if bl is not None:
        out = out + _dot(ah, bl, dims)
    if al is not None:
        out = out + _dot(al, bh, dims)
    return out


def _dot_x2(x, w_exact):
    hi, lo = _split2(x)
    return _dot(hi, w_exact) + _dot(lo, w_exact)


def _head_sum(x, hb_ref, hilo=True):
    w = hb_ref.shape[0]
    hb = hb_ref[...]
    one = _dot_x2 if hilo else (lambda xs, m: _dot(xs.astype(BF16), m))
    return jnp.concatenate(
        [one(x[:, c * w:(c + 1) * w], hb) for c in range(x.shape[1] // w)], axis=1)


def _rms_rows(x, g):
    ms = jnp.mean(x * x, axis=-1, keepdims=True)
    return (x * lax.rsqrt(ms + NORM_EPS)) * g


def _gelu(x):
    return 0.5 * x * (1.0 + lax.erf(x * (1.0 / math.sqrt(2.0))))


def _norm_tile(x_ref, xp_ref, g_ref, a_scr, ap_scr, row_chunk):
    def body(c, carry):
        rows = pl.ds(pl.multiple_of(c * row_chunk, row_chunk), row_chunk)
        a_scr[rows, :] = _rms_rows(x_ref[rows, :], g_ref[...]).astype(BF16)
        return carry
    lax.fori_loop(0, x_ref.shape[0] // row_chunk, body, 0)
    ap_scr[...] = _rms_rows(xp_ref[...], g_ref[...]).astype(BF16)


def _shift_mix_tile(a_scr, ap_scr, w_ref, sb_ref, first):
    p = _dot(a_scr[...], w_ref[...])
    pp = _dot(ap_scr[...], w_ref[...])
    prev_row = jnp.where(first, 0.0, pp[PREV_ROWS - 1:, :])
    row = lax.broadcasted_iota(jnp.int32, p.shape, 0)
    shifted = jnp.where(row == 0, prev_row, pltpu.roll(p, 1, 0))
    return p * sb_ref[0:1, :] + shifted * sb_ref[1:2, :]


def _in_proj_kernel(x_ref, xp_ref, g_ref, w_ref, sb_ref, lnw_ref, lnb_ref,
                    zu_ref, vn_ref, p_ref, a_scr, ap_scr, *, row_chunk, tiles_per_seq, n_mix):
    i = pl.program_id(0)
    j = pl.program_id(1)

    @pl.when(j == 0)
    def _():
        _norm_tile(x_ref, xp_ref, g_ref, a_scr, ap_scr, row_chunk)
        zu_ref[...] = _gelu(_dot(a_scr[...], w_ref[...])).astype(BF16)

    @pl.when(j == 1)
    def _():
        zv = _gelu(_dot(a_scr[...], w_ref[...]))
        mu = jnp.mean(zv, axis=-1, keepdims=True)
        zc = zv - mu
        var = jnp.mean(zc * zc, axis=-1, keepdims=True)
        vn_ref[...] = ((zc * lax.rsqrt(var + LN_EPS)) * lnw_ref[...] + lnb_ref[...]).astype(BF16)

    @pl.when(jnp.logical_and(j >= 2, j < 2 + n_mix))
    def _():
        p_ref[...] = _shift_mix_tile(a_scr, ap_scr, w_ref, sb_ref,
                                     (i % tiles_per_seq) == 0).astype(p_ref.dtype)

    @pl.when(j >= 2 + n_mix)
    def _():
        p_ref[...] = _dot(a_scr[...], w_ref[...]).astype(p_ref.dtype)


def _in_proj(x2, g, w16, sb, ln_w, ln_b, seq, tm, n_mix):
    m, d = x2.shape
    n = w16.shape[1]
    assert seq % tm == 0 and n % d == 0
    row = lambda r: pl.BlockSpec((r, d), lambda i, j: (0, 0))
    return pl.pallas_call(
        functools.partial(_in_proj_kernel, row_chunk=256, tiles_per_seq=seq // tm, n_mix=n_mix),
        grid=(m // tm, n // d),
        in_specs=[
            pl.BlockSpec((tm, d), lambda i, j: (i, 0)),
            pl.BlockSpec((PREV_ROWS, d),
                         lambda i, j: (jnp.maximum(i * (tm // PREV_ROWS) - 1, 0), 0)),
            row(1),
            pl.BlockSpec((d, d), lambda i, j: (0, j)),
            pl.BlockSpec((2, d), lambda i, j: (0, j)),
            row(1), row(1),
        ],
        out_specs=[
            pl.BlockSpec((tm, d), lambda i, j: (i, 0)),
            pl.BlockSpec((tm, d), lambda i, j: (i, 0)),
            pl.BlockSpec((tm, d), lambda i, j: (i, jnp.maximum(j - 2, 0))),
        ],
        out_shape=[jax.ShapeDtypeStruct((m, d), BF16), jax.ShapeDtypeStruct((m, d), BF16),
                   jax.ShapeDtypeStruct((m, n - 2 * d), BF16)],
        scratch_shapes=[pltpu.VMEM((tm, d), BF16), pltpu.VMEM((PREV_ROWS, d), BF16)],
        compiler_params=_cparams(("parallel", "arbitrary")),
        name="in_proj",
    )(x2, x2, g, w16, sb, ln_w, ln_b)


def _lora_proj_kernel(x_ref, xp_ref, g_ref, w_ref, sb_ref, o_ref, a_scr, ap_scr, *,
                      row_chunk, tiles_per_seq):
    _norm_tile(x_ref, xp_ref, g_ref, a_scr, ap_scr, row_chunk)
    o_ref[...] = _shift_mix_tile(a_scr, ap_scr, w_ref, sb_ref,
                                 (pl.program_id(0) % tiles_per_seq) == 0)


def _lora_proj(x2, g, w16, sb, seq, tm):
    m, d = x2.shape
    n = w16.shape[1]
    assert seq % tm == 0
    return pl.pallas_call(
        functools.partial(_lora_proj_kernel, row_chunk=256, tiles_per_seq=seq // tm),
        grid=(m // tm,),
        in_specs=[
            pl.BlockSpec((tm, d), lambda i: (i, 0)),
            pl.BlockSpec((PREV_ROWS, d), lambda i: (jnp.maximum(i * (tm // PREV_ROWS) - 1, 0), 0)),
            pl.BlockSpec((1, d), lambda i: (0, 0)),
            pl.BlockSpec((d, n), lambda i: (0, 0)),
            pl.BlockSpec((2, n), lambda i: (0, 0)),
        ],
        out_specs=pl.BlockSpec((tm, n), lambda i: (i, 0)),
        out_shape=jax.ShapeDtypeStruct((m, n), F32),
        scratch_shapes=[pltpu.VMEM((tm, d), BF16), pltpu.VMEM((PREV_ROWS, d), BF16)],
        compiler_params=_cparams(("parallel",)),
        name="lora_proj",
    )(x2, x2, g, w16, sb)


def _sgu_kernel(zu_ref, vn_ref, ga_ref, ws_ref, bias_ref, wpa_ref, o_ref, wm_scr, s_scr):
    ch = SGU_CHUNK
    dg = LANES

    @pl.when(pl.program_id(0) == 0)
    def _():
        row = lax.broadcasted_iota(jnp.int32, (ch, ch), 0)
        col = lax.broadcasted_iota(jnp.int32, (ch, ch), 1)
        for g in range(SGU_GROUPS):
            wm_scr[g] = jnp.where(col <= row, ws_ref[g], 0.0).astype(BF16)

    for c in range(zu_ref.shape[0] // ch):
        rows = pl.ds(c * ch, ch)
        for g in range(SGU_GROUPS):
            cols = slice(g * dg, (g + 1) * dg)
            sv = _dot(wm_scr[g], vn_ref[rows, cols]) + bias_ref[:, cols]
            s_scr[rows, cols] = (zu_ref[rows, cols] * sv).astype(BF16)

    ya = _dot(s_scr[...], wpa_ref[...])
    o_ref[...] = (jax.nn.sigmoid(ga_ref[...].astype(F32)) * ya).astype(o_ref.dtype)


def _sgu(zu, vn, p_main, sgu_w, bias_full, wpa16, ts, d):
    m = zu.shape[0]
    ch = SGU_CHUNK
    tile = pl.BlockSpec((ts, d), lambda i: (i, 0))
    return pl.pallas_call(
        _sgu_kernel,
        grid=(m // ts,),
        in_specs=[
            tile, tile,
            pl.BlockSpec((ts, d), lambda i: (i, 3)),
            pl.BlockSpec((SGU_GROUPS, ch, ch), lambda i: (0, 0, 0)),
            pl.BlockSpec((ch, d), lambda i: (0, 0)),
            pl.BlockSpec((d, d), lambda i: (0, 0)),
        ],
        out_specs=tile,
        out_shape=jax.ShapeDtypeStruct((m, d), BF16),
        scratch_shapes=[pltpu.VMEM((SGU_GROUPS, ch, ch), BF16), pltpu.VMEM((ts, d), BF16)],
        compiler_params=_cparams(("arbitrary",)),
        name="sgu",
    )(zu, vn, p_main, sgu_w, bias_full, wpa16)


def _rwkv_prep_kernel(qk_ref, ql_ref, ww_ref, wa_ref, wg_ref, vec_ref, hb_ref,
                      k_out, lw_out, kk_out, aa_out, g_out):
    k = qk_ref[...].astype(F32)
    lo = ql_ref[...]
    xw = lo[:, 0:LANES]
    xa = lo[:, LANES:2 * LANES]
    xg = lo[:, 2 * LANES:]
    w0 = vec_ref[0:1, :]
    a0 = vec_ref[1:2, :]
    k_k = vec_ref[2:3, :]
    k_a = vec_ref[3:4, :]

    zw = w0 + _dot_hp(_split2(jnp.tanh(xw)), _split2(ww_ref[...]))
    lw_out[...] = (-math.exp(-0.5)) * jax.nn.sigmoid(zw)
    aa = jax.nn.sigmoid(a0 + _dot_hp(_split2(xa), _split2(wa_ref[...])))
    g_out[...] = _dot_hp(_split2(jax.nn.sigmoid(xg)), _split2(wg_ref[...])).astype(g_out.dtype)
    kraw = k * k_k
    ss = _head_sum(kraw * kraw, hb_ref)
    kk_out[...] = (kraw * lax.rsqrt(jnp.maximum(ss, 1e-24))).astype(kk_out.dtype)
    k_out[...] = (k * (1.0 + (aa - 1.0) * k_a)).astype(k_out.dtype)
    aa_out[...] = aa.astype(aa_out.dtype)


def _rwkv_prep(p_main, p_lora, ww, wa, wg, vecs, hb, d, tt):
    m = p_main.shape[0]
    lw = p_lora.shape[1]
    full = lambda a: pl.BlockSpec(a.shape, lambda i: (0,) * a.ndim)
    out_spec = pl.BlockSpec((tt, d), lambda i: (i, 0))
    return pl.pallas_call(
        _rwkv_prep_kernel,
        grid=(m // tt,),
        in_specs=[
            pl.BlockSpec((tt, d), lambda i: (i, 1)),
            pl.BlockSpec((tt, lw), lambda i: (i, 0)),
            full(ww), full(wa), full(wg), full(vecs), full(hb),
        ],
        out_specs=[out_spec] * 5,
        out_shape=[jax.ShapeDtypeStruct((m, d), dt) for dt in (BF16, F32, BF16, BF16, BF16)],
        compiler_params=_cparams(("parallel",)),
        name="rwkv_prep",
    )(p_main, p_lora, ww, wa, wg, vecs, hb)


def _scan_kernel(r_ref, k_ref, v_ref, lw_ref, kk_ref, aa_ref, o_ref, s_scr, *, chunk, hd, gw):
    c_len = chunk
    assert c_len == hd
    bb, tblk, d = r_ref.shape
    n_groups = d // gw
    hpg = gw // hd
    n_double = int(math.log2(c_len)) - 1

    rowi = lax.broadcasted_iota(jnp.int32, (c_len, gw), 0)
    sub = lax.broadcasted_iota(jnp.int32, (c_len, gw), 1) % c_len
    strict = sub < rowi
    incl = sub <= rowi
    eye = jnp.where(sub == rowi, 1.0, 0.0).astype(F32)
    bd_mask = (lax.broadcasted_iota(jnp.int32, (gw, gw), 0) // c_len
               == lax.broadcasted_iota(jnp.int32, (gw, gw), 1) // hd)
    ltri = jnp.where(lax.broadcasted_iota(jnp.int32, (c_len, c_len), 1)
                     <= lax.broadcasted_iota(jnp.int32, (c_len, c_len), 0), 1.0, 0.0).astype(BF16)

    @pl.when(pl.program_id(1) == 0)
    def _():
        s_scr[...] = jnp.zeros_like(s_scr)

    def bd(x16):
        t = jnp.concatenate([x16] * hpg, axis=0)
        return jnp.where(bd_mask, t, jnp.zeros_like(t))

    chains = [(b, g) for b in range(bb) for g in range(n_groups)]
    n = range(len(chains))

    def body(c, carry):
        rows = pl.ds(pl.multiple_of(c * c_len, c_len), c_len)

        def at(ref, i):
            b, g = chains[i]
            return ref.at[b, rows, g * gw:(g + 1) * gw]

        ar, bq, kq, v16, bk, g_tot = [], [], [], [], [], []
        for i in n:
            k = at(k_ref, i)[...].astype(F32)
            lw = at(lw_ref, i)[...]
            kk = at(kk_ref, i)[...].astype(F32)
            b = kk * at(aa_ref, i)[...].astype(F32)
            lh, lm, ll = _split3(lw)
            cum = _dot(ltri, lh) + (_dot(ltri, lm) + _dot(ltri, ll))
            cum_last = cum[c_len - 1:c_len, :]
            g_inv = jnp.exp(-cum)
            g_end = jnp.exp(cum_last - cum)
            aq = -(kk * jnp.exp(cum - lw))
            rq = at(r_ref, i)[...].astype(F32) * jnp.exp(cum)
            ar.append(jnp.concatenate([aq, rq], axis=0).astype(BF16))
            bq.append((b * g_inv).astype(BF16))
            kq.append((k * g_inv).astype(BF16))
            v16.append(at(v_ref, i)[...].astype(BF16))
            bk.append(jnp.concatenate([b * g_end, k * g_end], axis=0).astype(BF16))
            g_tot.append(jnp.exp(cum_last))

        pb = [_dot(ar[i], bd(bq[i]), _NT) for i in n]
        pk = [_dot(ar[i], bd(kq[i]), _NT) for i in n]
        s0 = [s_scr[chains[i]] for i in n]
        ar_s = [_dot(ar[i], s0[i].astype(BF16), _NT) for i in n]
        a_ab = [jnp.where(strict, pb[i][:c_len], 0.0) for i in n]
        a_rb = [jnp.where(incl, pb[i][c_len:], 0.0).astype(BF16) for i in n]
        akv = [_dot(jnp.concatenate([jnp.where(strict, pk[i][:c_len], 0.0),
                                     jnp.where(incl, pk[i][c_len:], 0.0)], axis=0).astype(BF16),
                    bd(v16[i])) for i in n]
        rhs = [ar_s[i][:c_len] + akv[i][:c_len] for i in n]

        p = [_dot(a_ab[i].astype(BF16), bd(a_ab[i].astype(BF16))) for i in n]
        x = [eye + a_ab[i] for i in n]
        for _ in range(n_double - 1):
            xp = [_dot(jnp.concatenate([x[i], p[i]], axis=0).astype(BF16), bd(p[i].astype(BF16)))
                  for i in n]
            x = [x[i] + xp[i][:c_len] for i in n]
            p = [xp[i][c_len:] for i in n]
        x = [x[i] + _dot(x[i].astype(BF16), bd(p[i].astype(BF16))) for i in n]

        sa16 = [_dot(x[i].astype(BF16), bd(rhs[i].astype(BF16))).astype(BF16) for i in n]
        for i in n:
            at(o_ref, i)[...] = ((ar_s[i][c_len:] + akv[i][c_len:])
                                 + _dot(a_rb[i], bd(sa16[i]))).astype(o_ref.dtype)
        for i in n:
            upd = _dot(jnp.concatenate([sa16[i], v16[i]], axis=0), bk[i], _TN)
            s_scr[chains[i]] = jnp.where(bd_mask, s0[i] * g_tot[i] + upd, 0.0)
        return carry

    lax.fori_loop(0, tblk // c_len, body, 0)


def _rwkv_scan(p_main, k, lw, kk, aa, batch, seq, d, bb, tblk):
    gw = SCAN_GROUP_HEADS * RWKV_HEAD
    spec = pl.BlockSpec((bb, tblk, d), lambda b, t: (b, t, 0))
    col = lambda cb: pl.BlockSpec((bb, tblk, d), lambda b, t: (b, t, cb))
    as3 = lambda a: a.reshape(batch, seq, a.shape[-1])
    out = pl.pallas_call(
        functools.partial(_scan_kernel, chunk=SCAN_CHUNK, hd=RWKV_HEAD, gw=gw),
        grid=(batch // bb, seq // tblk),
        in_specs=[col(0), spec, col(2), spec, spec, spec],
        out_specs=spec,
        out_shape=jax.ShapeDtypeStruct((batch, seq, d), BF16),
        scratch_shapes=[pltpu.VMEM((bb, d // gw, gw, gw), F32)],
        compiler_params=_cparams(("parallel", "arbitrary")),
        name="rwkv_scan",
    )(as3(p_main), as3(k), as3(p_main), as3(lw), as3(kk), as3(aa))
    return out.reshape(batch * seq, d)


def _rwkv_post_kernel(o_ref, r_ref, k_ref, v_ref, g_ref, gb_ref, ya_ref, x_ref,
                      vec_ref, hb_ref, wpb_ref, wout_ref, h_ref, *, hd):
    lnx_w = vec_ref[0:1, :]
    lnx_b = vec_ref[1:2, :]
    r_k = vec_ref[2:3, :]
    o = o_ref[...].astype(F32)
    inv_n = 1.0 / hd
    mu = _head_sum(o, hb_ref, hilo=False) * inv_n
    oc = o - mu
    var = _head_sum(oc * oc, hb_ref, hilo=False) * inv_n
    on = (oc * lax.rsqrt(var + GN_EPS)) * lnx_w + lnx_b
    v = v_ref[...].astype(F32)
    rk = r_ref[...].astype(F32) * k_ref[...].astype(F32)
    bonus = _head_sum(rk * r_k, hb_ref, hilo=False) * v
    yb = _dot(((on + bonus) * g_ref[...]).astype(BF16), wpb_ref[...])
    mixed = ya_ref[...].astype(F32) + jax.nn.sigmoid(gb_ref[...].astype(F32)) * yb
    h_ref[...] = x_ref[...] + _dot(mixed.astype(BF16), wout_ref[...])


def _rwkv_post(o, k, g, p_main, ya, x2, vecs, hb, wpb16, wout16, tm, d):
    m = x2.shape[0]
    tile = pl.BlockSpec((tm, d), lambda i: (i, 0))
    col = lambda cb: pl.BlockSpec((tm, d), lambda i: (i, cb))
    full = lambda a: pl.BlockSpec(a.shape, lambda i: (0,) * a.ndim)
    return pl.pallas_call(
        functools.partial(_rwkv_post_kernel, hd=RWKV_HEAD),
        grid=(m // tm,),
        in_specs=[tile, col(0), tile, col(2), tile, col(4),
                  tile, tile, full(vecs), full(hb), full(wpb16), full(wout16)],
        out_specs=tile,
        out_shape=jax.ShapeDtypeStruct((m, d), F32),
        compiler_params=_cparams(("parallel",)),
        name="rwkv_post",
    )(o, p_main, k, p_main, g, p_main, ya, x2, vecs, hb, wpb16, wout16)


def _ffn_kernel(h_ref, g_ref, w1_ref, w2_ref, gf_ref, o_ref, f_scr, acc_scr):
    j = pl.program_id(1)

    @pl.when(j == 0)
    def _():
        f_scr[...] = _rms_rows(h_ref[...], g_ref[...]).astype(BF16)
        acc_scr[...] = jnp.zeros_like(acc_scr)

    t = jnp.maximum(_dot(f_scr[...], w1_ref[...]), 0.0)
    acc_scr[...] += _dot((t * t).astype(BF16), w2_ref[...])

    @pl.when(j == pl.num_programs(1) - 1)
    def _():
        o_ref[...] = _rms_rows(h_ref[...] + acc_scr[...], gf_ref[...])


def _ffn(h, g, w1_16, w2_16, g_final, tm, tf):
    m, d = h.shape
    dff = w1_16.shape[1]
    return pl.pallas_call(
        _ffn_kernel,
        grid=(m // tm, dff // tf),
        in_specs=[
            pl.BlockSpec((tm, d), lambda i, j: (i, 0)),
            pl.BlockSpec((1, d), lambda i, j: (0, 0)),
            pl.BlockSpec((d, tf), lambda i, j: (0, j)),
            pl.BlockSpec((tf, d), lambda i, j: (j, 0)),
            pl.BlockSpec((1, d), lambda i, j: (0, 0)),
        ],
        out_specs=pl.BlockSpec((tm, d), lambda i, j: (i, 0)),
        out_shape=jax.ShapeDtypeStruct((m, d), F32),
        scratch_shapes=[pltpu.VMEM((tm, d), BF16), pltpu.VMEM((tm, d), F32)],
        compiler_params=_cparams(("parallel", "arbitrary")),
        name="ffn",
    )(h, g, w1_16, w2_16, g_final)


def _pad_cols(a, n):
    return jnp.pad(a, ((0, 0), (0, n - a.shape[1])))


def _pad_rows(a, n):
    return jnp.pad(a, ((0, n - a.shape[0]), (0, 0)))


def _layer(x2, batch, seq, g_mix, w_in, sgu_ln_w, sgu_ln_b, sgu_w, sgu_b, w_proj_a, shift_b,
           w_lora_w, w0, a_lora_w, a0, g_lora_w, k_k, k_a, r_k, ln_x_w, ln_x_b, w_proj_b,
           w_out, g_ffn, w_ffn1, w_ffn2, g_out):
    d = x2.shape[1]
    lora_w, lora_a, lora_g = w_lora_w.shape[0], a_lora_w.shape[0], g_lora_w.shape[0]
    c_sgu = 2 * d
    c_rkv = 3 * d
    c_lora = lora_w + lora_a + lora_g
    o_lora = c_sgu + c_rkv
    o_gate = o_lora + c_lora
    pw, pa = LANES, LANES
    pg = -(-lora_g // LANES) * LANES

    w_main = jnp.concatenate([w_in[:, :o_lora], w_in[:, o_gate:]], axis=1).astype(BF16)
    w_lo = jnp.concatenate([
        _pad_cols(w_in[:, o_lora:o_lora + lora_w], pw),
        _pad_cols(w_in[:, o_lora + lora_w:o_lora + lora_w + lora_a], pa),
        _pad_cols(w_in[:, o_lora + lora_w + lora_a:o_gate], pg)], axis=1).astype(BF16)
    sb_lo = shift_b[:, c_rkv:]
    sb_lora = jnp.concatenate([
        _pad_cols(sb_lo[:, :lora_w], pw),
        _pad_cols(sb_lo[:, lora_w:lora_w + lora_a], pa),
        _pad_cols(sb_lo[:, lora_w + lora_a:], pg)], axis=1)
    sb_main = jnp.concatenate([jnp.zeros((2, c_sgu), F32), shift_b[:, :c_rkv],
                               jnp.zeros((2, 2 * d), F32)], axis=1)

    g_mix2 = g_mix.reshape(1, d)
    zu, vn, p_main = _in_proj(x2, g_mix2, w_main, sb_main, sgu_ln_w.reshape(1, d),
                              sgu_ln_b.reshape(1, d), seq, tm=1024, n_mix=c_rkv // d)
    p_lora = _lora_proj(x2, g_mix2, w_lo, sb_lora, seq, tm=1024)

    bias_full = jnp.repeat(sgu_b.T, d // SGU_GROUPS, axis=1)
    ya = _sgu(zu, vn, p_main, sgu_w, bias_full, w_proj_a.astype(BF16), ts=512, d=d)

    gw = SCAN_GROUP_HEADS * RWKV_HEAD
    hb = (lax.broadcasted_iota(jnp.int32, (gw, gw), 0) // RWKV_HEAD
          == lax.broadcasted_iota(jnp.int32, (gw, gw), 1) // RWKV_HEAD).astype(BF16)
    vec_prep = jnp.stack([w0, a0, k_k, k_a])
    k, lw, kk, aa, g = _rwkv_prep(
        p_main, p_lora, _pad_rows(w_lora_w, pw), _pad_rows(a_lora_w, pa),
        _pad_rows(g_lora_w, pg), vec_prep, hb, d, tt=256)

    o = _rwkv_scan(p_main, k, lw, kk, aa, batch, seq, d, bb=4, tblk=128)

    vec_post = jnp.stack([ln_x_w, ln_x_b, r_k])
    h1 = _rwkv_post(o, k, g, p_main, ya, x2, vec_post, hb, w_proj_b.astype(BF16),
                    w_out.astype(BF16), tm=512, d=d)
    return _ffn(h1, g_ffn.reshape(1, d), w_ffn1.astype(BF16), w_ffn2.astype(BF16),
                g_out.reshape(1, d), tm=1024, tf=1024)


def kernel(x, g_mix, w_in, sgu_ln_w, sgu_ln_b, sgu_w, sgu_b, w_proj_a, shift_b, w_lora_w, w0,
           a_lora_w, a0, g_lora_w, k_k, k_a, r_k, ln_x_w, ln_x_b, w_proj_b, w_out, g_ffn,
           w_ffn1, w_ffn2, g_final):
    batch, seq, d = x.shape
    depth = w_in.shape[0]
    assert depth == 1, "the final RMSNorm is fused into the single layer's ffn call"
    h = x.reshape(batch * seq, d)
    l = 0
    h = _layer(h, batch, seq, g_mix[l], w_in[l], sgu_ln_w[l], sgu_ln_b[l], sgu_w[l], sgu_b[l],
               w_proj_a[l], shift_b[l], w_lora_w[l], w0[l], a_lora_w[l], a0[l], g_lora_w[l],
               k_k[l], k_a[l], r_k[l], ln_x_w[l], ln_x_b[l], w_proj_b[l], w_out[l], g_ffn[l],
               w_ffn1[l], w_ffn2[l], g_final)
    return h.reshape(batch, seq, d)
```

```python
import functools
import math

import jax
import jax.numpy as jnp
from jax import lax
from jax.experimental import pallas as pl
from jax.experimental.pallas import tpu as pltpu

F32 = jnp.float32
BF16 = jnp.bfloat16

SGU_CHUNK = 128
SGU_GROUPS = 8
RWKV_HEAD = 64
NORM_EPS = 1e-6
LN_EPS = 1e-5
GN_EPS = 64e-5

LANES = 128
PREV_ROWS = 16
SCAN_CHUNK = 64
SCAN_GROUP_HEADS = 4
VMEM_LIMIT = 56 * 1024 * 1024
SCAN_HILO = dict(ar=False, bq=False, kq=False, v=False, bk=False, s0=False, a_ab=False,
                 a_ak=False, a_r=False, x=False, p=False, rhs=False, sa=False)


def _cparams(sem):
    return pltpu.CompilerParams(dimension_semantics=sem, vmem_limit_bytes=VMEM_LIMIT)


def _dot(a, b, dims=(((1,), (0,)), ((), ()))):
    return lax.dot_general(a, b, dims, preferred_element_type=F32)


_NT = (((1,), (1,)), ((), ()))
_TN = (((0,), (0,)), ((), ()))


def _split2(x):
    hi = x.astype(BF16)
    lo = (x - hi.astype(F32)).astype(BF16)
    return hi, lo


def _split3(x):
    hi = x.astype(BF16)
    r1 = x - hi.astype(F32)
    mid = r1.astype(BF16)
    lo = (r1 - mid.astype(F32)).astype(BF16)
    return hi, mid, lo


def _split1(x):
    return x.astype(BF16), None


def _dot_hp(a2, b2, dims=(((1,), (0,)), ((), ()))):
    ah, al = a2
    bh, bl = b2
    out = _dot(ah, bh, dims)
    if bl is not None:
        out = out + _dot(ah, bl, dims)
    if al is not None:
        out = out + _dot(al, bh, dims)
    return out


def _dot_x2(x, w_exact):
    hi, lo = _split2(x)
    return _dot(hi, w_exact) + _dot(lo, w_exact)


def _head_sum(x, hb_ref, hilo=True):
    w = hb_ref.shape[0]
    hb = hb_ref[...]
    one = _dot_x2 if hilo else (lambda xs, m: _dot(xs.astype(BF16), m))
    return jnp.concatenate(
        [one(x[:, c * w:(c + 1) * w], hb) for c in range(x.shape[1] // w)], axis=1)


def _rms_rows(x, g):
    ms = jnp.mean(x * x, axis=-1, keepdims=True)
    return (x * lax.rsqrt(ms + NORM_EPS)) * g


def _gelu(x):
    return 0.5 * x * (1.0 + lax.erf(x * (1.0 / math.sqrt(2.0))))


def _norm_tile(x_ref, xp_ref, g_ref, a_scr, ap_scr, row_chunk):
    def body(c, carry):
        rows = pl.ds(pl.multiple_of(c * row_chunk, row_chunk), row_chunk)
        a_scr[rows, :] = _rms_rows(x_ref[rows, :], g_ref[...]).astype(BF16)
        return carry
    lax.fori_loop(0, x_ref.shape[0] // row_chunk, body, 0)
    ap_scr[...] = _rms_rows(xp_ref[...], g_ref[...]).astype(BF16)


def _shift_mix_tile(a_scr, ap_scr, w_ref, sb_ref, first, ncols=None):
    cols = slice(0, w_ref.shape[1] if ncols is None else ncols)
    w = w_ref[:, cols]
    p = _dot(a_scr[...], w)
    pp = _dot(ap_scr[...], w)
    prev_row = jnp.where(first, 0.0, pp[PREV_ROWS - 1:, :])
    row = lax.broadcasted_iota(jnp.int32, p.shape, 0)
    shifted = jnp.where(row == 0, prev_row, pltpu.roll(p, 1, 0))
    return p * sb_ref[0:1, cols] + shifted * sb_ref[1:2, cols]


def _in_proj_kernel(x_ref, xp_ref, g_ref, w_ref, sb_ref, lnw_ref, lnb_ref,
                    zu_ref, vn_ref, p_ref, lo_ref, a_scr, ap_scr, *,
                    row_chunk, tiles_per_seq, n_mix, n_tiles):
    i = pl.program_id(0)
    j = pl.program_id(1)

    @pl.when(j == 0)
    def _():
        _norm_tile(x_ref, xp_ref, g_ref, a_scr, ap_scr, row_chunk)
        zu_ref[...] = _gelu(_dot(a_scr[...], w_ref[...])).astype(BF16)

    @pl.when(j == 1)
    def _():
        zv = _gelu(_dot(a_scr[...], w_ref[...]))
        mu = jnp.mean(zv, axis=-1, keepdims=True)
        zc = zv - mu
        var = jnp.mean(zc * zc, axis=-1, keepdims=True)
        vn_ref[...] = ((zc * lax.rsqrt(var + LN_EPS)) * lnw_ref[...] + lnb_ref[...]).astype(BF16)

    @pl.when(jnp.logical_and(j >= 2, j < 2 + n_mix))
    def _():
        p_ref[...] = _shift_mix_tile(a_scr, ap_scr, w_ref, sb_ref,
                                     (i % tiles_per_seq) == 0).astype(p_ref.dtype)

    @pl.when(jnp.logical_and(j >= 2 + n_mix, j < n_tiles - 1))
    def _():
        p_ref[...] = _dot(a_scr[...], w_ref[...]).astype(p_ref.dtype)

    @pl.when(j == n_tiles - 1)
    def _():
        lo_ref[...] = _shift_mix_tile(a_scr, ap_scr, w_ref, sb_ref, (i % tiles_per_seq) == 0,
                                      ncols=lo_ref.shape[1])


def _in_proj(x2, g, w16, sb, ln_w, ln_b, seq, tm, n_mix, n_lora):
    m, d = x2.shape
    n_tiles = w16.shape[1] // d
    n_main = n_tiles - 3
    assert seq % tm == 0 and w16.shape[1] % d == 0
    row = lambda r: pl.BlockSpec((r, d), lambda i, j: (0, 0))
    return pl.pallas_call(
        functools.partial(_in_proj_kernel, row_chunk=256, tiles_per_seq=seq // tm, n_mix=n_mix,
                          n_tiles=n_tiles),
        grid=(m // tm, n_tiles),
        in_specs=[
            pl.BlockSpec((tm, d), lambda i, j: (i, 0)),
            pl.BlockSpec((PREV_ROWS, d),
                         lambda i, j: (jnp.maximum(i * (tm // PREV_ROWS) - 1, 0), 0)),
            row(1),
            pl.BlockSpec((d, d), lambda i, j: (0, j)),
            pl.BlockSpec((2, d), lambda i, j: (0, j)),
            row(1), row(1),
        ],
        out_specs=[
            pl.BlockSpec((tm, d), lambda i, j: (i, 0)),
            pl.BlockSpec((tm, d), lambda i, j: (i, 0)),
            pl.BlockSpec((tm, d), lambda i, j: (i, jnp.clip(j - 2, 0, n_main - 1))),
            pl.BlockSpec((tm, n_lora), lambda i, j: (i, 0)),
        ],
        out_shape=[jax.ShapeDtypeStruct((m, d), BF16), jax.ShapeDtypeStruct((m, d), BF16),
                   jax.ShapeDtypeStruct((m, n_main * d), BF16),
                   jax.ShapeDtypeStruct((m, n_lora), F32)],
        scratch_shapes=[pltpu.VMEM((tm, d), BF16), pltpu.VMEM((PREV_ROWS, d), BF16)],
        compiler_params=_cparams(("parallel", "arbitrary")),
        name="in_proj",
    )(x2, x2, g, w16, sb, ln_w, ln_b)


def _sgu_kernel(zu_ref, vn_ref, ga_ref, ws_ref, bias_ref, wpa_ref, o_ref, wm_scr, s_scr):
    ch = SGU_CHUNK
    dg = LANES

    @pl.when(pl.program_id(0) == 0)
    def _():
        row = lax.broadcasted_iota(jnp.int32, (ch, ch), 0)
        col = lax.broadcasted_iota(jnp.int32, (ch, ch), 1)
        for g in range(SGU_GROUPS):
            wm_scr[g] = jnp.where(col <= row, ws_ref[g], 0.0).astype(BF16)

    for c in range(zu_ref.shape[0] // ch):
        rows = pl.ds(c * ch, ch)
        for g in range(SGU_GROUPS):
            cols = slice(g * dg, (g + 1) * dg)
            sv = _dot(wm_scr[g], vn_ref[rows, cols]) + bias_ref[:, cols]
            s_scr[rows, cols] = (zu_ref[rows, cols] * sv).astype(BF16)

    ya = _dot(s_scr[...], wpa_ref[...])
    o_ref[...] = (jax.nn.sigmoid(ga_ref[...].astype(F32)) * ya).astype(o_ref.dtype)


def _sgu(zu, vn, p_main, sgu_w, bias_full, wpa16, ts, d):
    m = zu.shape[0]
    ch = SGU_CHUNK
    tile = pl.BlockSpec((ts, d), lambda i: (i, 0))
    return pl.pallas_call(
        _sgu_kernel,
        grid=(m // ts,),
        in_specs=[
            tile, tile,
            pl.BlockSpec((ts, d), lambda i: (i, 3)),
            pl.BlockSpec((SGU_GROUPS, ch, ch), lambda i: (0, 0, 0)),
            pl.BlockSpec((ch, d), lambda i: (0, 0)),
            pl.BlockSpec((d, d), lambda i: (0, 0)),
        ],
        out_specs=tile,
        out_shape=jax.ShapeDtypeStruct((m, d), BF16),
        scratch_shapes=[pltpu.VMEM((SGU_GROUPS, ch, ch), BF16), pltpu.VMEM((ts, d), BF16)],
        compiler_params=_cparams(("arbitrary",)),
        name="sgu",
    )(zu, vn, p_main, sgu_w, bias_full, wpa16)


def _rwkv_prep_kernel(qk_ref, ql_ref, ww_ref, wa_ref, wg_ref, vec_ref, hb_ref,
                      k_out, lw_out, kk_out, aa_out, g_out):
    k = qk_ref[...].astype(F32)
    lo = ql_ref[...]
    xw = lo[:, 0:LANES]
    xa = lo[:, LANES:2 * LANES]
    xg = lo[:, 2 * LANES:]
    w0 = vec_ref[0:1, :]
    a0 = vec_ref[1:2, :]
    k_k = vec_ref[2:3, :]
    k_a = vec_ref[3:4, :]

    zw = w0 + _dot_hp(_split2(jnp.tanh(xw)), (ww_ref[0], ww_ref[1]))
    lw_out[...] = (-math.exp(-0.5)) * jax.nn.sigmoid(zw)
    aa = jax.nn.sigmoid(a0 + _dot_hp(_split2(xa), (wa_ref[0], wa_ref[1])))
    g_out[...] = _dot_hp(_split2(jax.nn.sigmoid(xg)), (wg_ref[0], wg_ref[1])).astype(g_out.dtype)
    kraw = k * k_k
    ss = _head_sum(kraw * kraw, hb_ref)
    kk_out[...] = (kraw * lax.rsqrt(jnp.maximum(ss, 1e-24))).astype(kk_out.dtype)
    k_out[...] = (k * (1.0 + (aa - 1.0) * k_a)).astype(k_out.dtype)
    aa_out[...] = aa.astype(aa_out.dtype)


def _rwkv_prep(p_main, p_lora, ww, wa, wg, vecs, hb, d, tt):
    m = p_main.shape[0]
    lw = p_lora.shape[1]
    full = lambda a: pl.BlockSpec(a.shape, lambda i: (0,) * a.ndim)
    out_spec = pl.BlockSpec((tt, d), lambda i: (i, 0))
    return pl.pallas_call(
        _rwkv_prep_kernel,
        grid=(m // tt,),
        in_specs=[
            pl.BlockSpec((tt, d), lambda i: (i, 1)),
            pl.BlockSpec((tt, lw), lambda i: (i, 0)),
            full(ww), full(wa), full(wg), full(vecs), full(hb),
        ],
        out_specs=[out_spec] * 5,
        out_shape=[jax.ShapeDtypeStruct((m, d), dt) for dt in (BF16, F32, BF16, BF16, BF16)],
        compiler_params=_cparams(("parallel",)),
        name="rwkv_prep",
    )(p_main, p_lora, ww, wa, wg, vecs, hb)


def _scan_kernel(r_ref, k_ref, v_ref, lw_ref, kk_ref, aa_ref, o_ref, s_scr, *, chunk, hd, gw):
    c_len = chunk
    assert c_len == hd
    bb, tblk, d = r_ref.shape
    n_groups = d // gw
    hpg = gw // hd
    n_double = int(math.log2(c_len)) - 1

    rowi = lax.broadcasted_iota(jnp.int32, (c_len, gw), 0)
    sub = lax.broadcasted_iota(jnp.int32, (c_len, gw), 1) % c_len
    strict = sub < rowi
    incl = sub <= rowi
    eye = jnp.where(sub == rowi, 1.0, 0.0).astype(F32)
    bd_mask = (lax.broadcasted_iota(jnp.int32, (gw, gw), 0) // c_len
               == lax.broadcasted_iota(jnp.int32, (gw, gw), 1) // hd)
    ltri = jnp.where(lax.broadcasted_iota(jnp.int32, (c_len, c_len), 1)
                     <= lax.broadcasted_iota(jnp.int32, (c_len, c_len), 0), 1.0, 0.0).astype(BF16)

    @pl.when(pl.program_id(1) == 0)
    def _():
        s_scr[...] = jnp.zeros_like(s_scr)

    def bd(x16):
        t = jnp.concatenate([x16] * hpg, axis=0)
        return jnp.where(bd_mask, t, jnp.zeros_like(t))

    chains = [(b, g) for b in range(bb) for g in range(n_groups)]
    n = range(len(chains))

    def body(c, carry):
        rows = pl.ds(pl.multiple_of(c * c_len, c_len), c_len)

        def at(ref, i):
            b, g = chains[i]
            return ref.at[b, rows, g * gw:(g + 1) * gw]

        ar, bq, kq, v16, bk, g_tot = [], [], [], [], [], []
        for i in n:
            k = at(k_ref, i)[...].astype(F32)
            lw = at(lw_ref, i)[...]
            kk = at(kk_ref, i)[...].astype(F32)
            b = kk * at(aa_ref, i)[...].astype(F32)
            lh, lm, ll = _split3(lw)
            cum = _dot(ltri, lh) + (_dot(ltri, lm) + _dot(ltri, ll))
            cum_last = cum[c_len - 1:c_len, :]
            g_inv = jnp.exp(-cum)
            g_end = jnp.exp(cum_last - cum)
            aq = -(kk * jnp.exp(cum - lw))
            rq = at(r_ref, i)[...].astype(F32) * jnp.exp(cum)
            ar.append(jnp.concatenate([aq, rq], axis=0).astype(BF16))
            bq.append((b * g_inv).astype(BF16))
            kq.append((k * g_inv).astype(BF16))
            v16.append(at(v_ref, i)[...].astype(BF16))
            bk.append(jnp.concatenate([b * g_end, k * g_end], axis=0).astype(BF16))
            g_tot.append(jnp.exp(cum_last))

        pb = [_dot(ar[i], bd(bq[i]), _NT) for i in n]
        pk = [_dot(ar[i], bd(kq[i]), _NT) for i in n]
        s0 = [s_scr[chains[i]] for i in n]
        ar_s = [_dot(ar[i], s0[i].astype(BF16), _NT) for i in n]
        a_ab = [jnp.where(strict, pb[i][:c_len], 0.0) for i in n]
        a_rb = [jnp.where(incl, pb[i][c_len:], 0.0).astype(BF16) for i in n]
        akv = [_dot(jnp.concatenate([jnp.where(strict, pk[i][:c_len], 0.0),
                                     jnp.where(incl, pk[i][c_len:], 0.0)], axis=0).astype(BF16),
                    bd(v16[i])) for i in n]
        rhs = [ar_s[i][:c_len] + akv[i][:c_len] for i in n]

        p = [_dot(a_ab[i].astype(BF16), bd(a_ab[i].astype(BF16))) for i in n]
        x = [eye + a_ab[i] for i in n]
        for _ in range(n_double - 1):
            xp = [_dot(jnp.concatenate([x[i], p[i]], axis=0).astype(BF16), bd(p[i].astype(BF16)))
                  for i in n]
            x = [x[i] + xp[i][:c_len] for i in n]
            p = [xp[i][c_len:] for i in n]
        x = [x[i] + _dot(x[i].astype(BF16), bd(p[i].astype(BF16))) for i in n]

        sa16 = [_dot(x[i].astype(BF16), bd(rhs[i].astype(BF16))).astype(BF16) for i in n]
        for i in n:
            at(o_ref, i)[...] = ((ar_s[i][c_len:] + akv[i][c_len:])
                                 + _dot(a_rb[i], bd(sa16[i]))).astype(o_ref.dtype)
        for i in n:
            upd = _dot(jnp.concatenate([sa16[i], v16[i]], axis=0), bk[i], _TN)
            s_scr[chains[i]] = jnp.where(bd_mask, s0[i] * g_tot[i] + upd, 0.0)
        return carry

    lax.fori_loop(0, tblk // c_len, body, 0)


def _rwkv_scan(p_main, k, lw, kk, aa, batch, seq, d, bb, tblk):
    gw = SCAN_GROUP_HEADS * RWKV_HEAD
    spec = pl.BlockSpec((bb, tblk, d), lambda b, t: (b, t, 0))
    col = lambda cb: pl.BlockSpec((bb, tblk, d), lambda b, t: (b, t, cb))
    as3 = lambda a: a.reshape(batch, seq, a.shape[-1])
    out = pl.pallas_call(
        functools.partial(_scan_kernel, chunk=SCAN_CHUNK, hd=RWKV_HEAD, gw=gw),
        grid=(batch // bb, seq // tblk),
        in_specs=[col(0), spec, col(2), spec, spec, spec],
        out_specs=spec,
        out_shape=jax.ShapeDtypeStruct((batch, seq, d), BF16),
        scratch_shapes=[pltpu.VMEM((bb, d // gw, gw, gw), F32)],
        compiler_params=_cparams(("parallel", "arbitrary")),
        name="rwkv_scan",
    )(as3(p_main), as3(k), as3(p_main), as3(lw), as3(kk), as3(aa))
    return out.reshape(batch * seq, d)


def _rwkv_post_kernel(o_ref, r_ref, k_ref, v_ref, g_ref, gb_ref, ya_ref, x_ref,
                      vec_ref, hb_ref, wpb_ref, wout_ref, h_ref, *, hd):
    lnx_w = vec_ref[0:1, :]
    lnx_b = vec_ref[1:2, :]
    r_k = vec_ref[2:3, :]
    o = o_ref[...].astype(F32)
    inv_n = 1.0 / hd
    mu = _head_sum(o, hb_ref, hilo=False) * inv_n
    oc = o - mu
    var = _head_sum(oc * oc, hb_ref, hilo=False) * inv_n
    on = (oc * lax.rsqrt(var + GN_EPS)) * lnx_w + lnx_b
    v = v_ref[...].astype(F32)
    rk = r_ref[...].astype(F32) * k_ref[...].astype(F32)
    bonus = _head_sum(rk * r_k, hb_ref, hilo=False) * v
    yb = _dot(((on + bonus) * g_ref[...]).astype(BF16), wpb_ref[...])
    mixed = ya_ref[...].astype(F32) + jax.nn.sigmoid(gb_ref[...].astype(F32)) * yb
    h_ref[...] = x_ref[...] + _dot(mixed.astype(BF16), wout_ref[...])


def _rwkv_post(o, k, g, p_main, ya, x2, vecs, hb, wpb16, wout16, tm, d):
    m = x2.shape[0]
    tile = pl.BlockSpec((tm, d), lambda i: (i, 0))
    col = lambda cb: pl.BlockSpec((tm, d), lambda i: (i, cb))
    full = lambda a: pl.BlockSpec(a.shape, lambda i: (0,) * a.ndim)
    return pl.pallas_call(
        functools.partial(_rwkv_post_kernel, hd=RWKV_HEAD),
        grid=(m // tm,),
        in_specs=[tile, col(0), tile, col(2), tile, col(4),
                  tile, tile, full(vecs), full(hb), full(wpb16), full(wout16)],
        out_specs=tile,
        out_shape=jax.ShapeDtypeStruct((m, d), F32),
        compiler_params=_cparams(("parallel",)),
        name="rwkv_post",
    )(o, p_main, k, p_main, g, p_main, ya, x2, vecs, hb, wpb16, wout16)


def _ffn_kernel(h_ref, g_ref, w1_ref, w2_ref, gf_ref, o_ref, f_scr, acc_scr):
    j = pl.program_id(1)

    @pl.when(j == 0)
    def _():
        f_scr[...] = _rms_rows(h_ref[...], g_ref[...]).astype(BF16)
        acc_scr[...] = jnp.zeros_like(acc_scr)

    t = jnp.maximum(_dot(f_scr[...], w1_ref[...]), 0.0)
    acc_scr[...] += _dot((t * t).astype(BF16), w2_ref[...])

    @pl.when(j == pl.num_programs(1) - 1)
    def _():
        o_ref[...] = _rms_rows(h_ref[...] + acc_scr[...], gf_ref[...])


def _ffn(h, g, w1_16, w2_16, g_final, tm, tf):
    m, d = h.shape
    dff = w1_16.shape[1]
    return pl.pallas_call(
        _ffn_kernel,
        grid=(m // tm, dff // tf),
        in_specs=[
            pl.BlockSpec((tm, d), lambda i, j: (i, 0)),
            pl.BlockSpec((1, d), lambda i, j: (0, 0)),
            pl.BlockSpec((d, tf), lambda i, j: (0, j)),
            pl.BlockSpec((tf, d), lambda i, j: (j, 0)),
            pl.BlockSpec((1, d), lambda i, j: (0, 0)),
        ],
        out_specs=pl.BlockSpec((tm, d), lambda i, j: (i, 0)),
        out_shape=jax.ShapeDtypeStruct((m, d), F32),
        scratch_shapes=[pltpu.VMEM((tm, d), BF16), pltpu.VMEM((tm, d), F32)],
        compiler_params=_cparams(("parallel", "arbitrary")),
        name="ffn",
    )(h, g, w1_16, w2_16, g_final)


def _pad_cols(a, n):
    return jnp.pad(a, ((0, 0), (0, n - a.shape[1])))


def _pad_rows(a, n):
    return jnp.pad(a, ((0, n - a.shape[0]), (0, 0)))


def _layer(x2, batch, seq, g_mix, w_in, sgu_ln_w, sgu_ln_b, sgu_w, sgu_b, w_proj_a, shift_b,
           w_lora_w, w0, a_lora_w, a0, g_lora_w, k_k, k_a, r_k, ln_x_w, ln_x_b, w_proj_b,
           w_out, g_ffn, w_ffn1, w_ffn2, g_out):
    d = x2.shape[1]
    lora_w, lora_a, lora_g = w_lora_w.shape[0], a_lora_w.shape[0], g_lora_w.shape[0]
    c_sgu = 2 * d
    c_rkv = 3 * d
    c_lora = lora_w + lora_a + lora_g
    o_lora = c_sgu + c_rkv
    o_gate = o_lora + c_lora
    pw, pa = LANES, LANES
    pg = -(-lora_g // LANES) * LANES

    n_lora = pw + pa + pg
    w16 = w_in.astype(BF16)
    w_all = jnp.concatenate([
        w16[:, :o_lora], w16[:, o_gate:],
        _pad_cols(w16[:, o_lora:o_lora + lora_w], pw),
        _pad_cols(w16[:, o_lora + lora_w:o_lora + lora_w + lora_a], pa),
        _pad_cols(w16[:, o_lora + lora_w + lora_a:o_gate], pg + d - n_lora)], axis=1)
    sb_lo = shift_b[:, c_rkv:]
    sb_all = jnp.concatenate([
        jnp.zeros((2, c_sgu), F32), shift_b[:, :c_rkv], jnp.zeros((2, 2 * d), F32),
        _pad_cols(sb_lo[:, :lora_w], pw),
        _pad_cols(sb_lo[:, lora_w:lora_w + lora_a], pa),
        _pad_cols(sb_lo[:, lora_w + lora_a:], pg + d - n_lora)], axis=1)

    g_mix2 = g_mix.reshape(1, d)
    zu, vn, p_main, p_lora = _in_proj(
        x2, g_mix2, w_all, sb_all, sgu_ln_w.reshape(1, d), sgu_ln_b.reshape(1, d), seq,
        tm=1024, n_mix=c_rkv // d, n_lora=n_lora)

    bias_full = jnp.repeat(sgu_b.T, d // SGU_GROUPS, axis=1)
    ya = _sgu(zu, vn, p_main, sgu_w, bias_full, w_proj_a.astype(BF16), ts=512, d=d)

    gw = SCAN_GROUP_HEADS * RWKV_HEAD
    hb = (lax.broadcasted_iota(jnp.int32, (gw, gw), 0) // RWKV_HEAD
          == lax.broadcasted_iota(jnp.int32, (gw, gw), 1) // RWKV_HEAD).astype(BF16)
    vec_prep = jnp.stack([w0, a0, k_k, k_a])
    hilo = lambda w, rows: jnp.stack(_split2(_pad_rows(w, rows)))
    k, lw, kk, aa, g = _rwkv_prep(
        p_main, p_lora, hilo(w_lora_w, pw), hilo(a_lora_w, pa), hilo(g_lora_w, pg),
        vec_prep, hb, d, tt=256)

    o = _rwkv_scan(p_main, k, lw, kk, aa, batch, seq, d, bb=4, tblk=256)

    vec_post = jnp.stack([ln_x_w, ln_x_b, r_k])
    h1 = _rwkv_post(o, k, g, p_main, ya, x2, vec_post, hb, w_proj_b.astype(BF16),
                    w_out.astype(BF16), tm=512, d=d)
    return _ffn(h1, g_ffn.reshape(1, d), w_ffn1.astype(BF16), w_ffn2.astype(BF16),
                g_out.reshape(1, d), tm=1024, tf=1024)


def kernel(x, g_mix, w_in, sgu_ln_w, sgu_ln_b, sgu_w, sgu_b, w_proj_a, shift_b, w_lora_w, w0,
           a_lora_w, a0, g_lora_w, k_k, k_a, r_k, ln_x_w, ln_x_b, w_proj_b, w_out, g_ffn,
           w_ffn1, w_ffn2, g_final):
    batch, seq, d = x.shape
    depth = w_in.shape[0]
    assert depth == 1, "the final RMSNorm is fused into the single layer's ffn call"
    h = x.reshape(batch * seq, d)
    l = 0
    h = _layer(h, batch, seq, g_mix[l], w_in[l], sgu_ln_w[l], sgu_ln_b[l], sgu_w[l], sgu_b[l],
               w_proj_a[l], shift_b[l], w_lora_w[l], w0[l], a_lora_w[l], a0[l], g_lora_w[l],
               k_k[l], k_a[l], r_k[l], ln_x_w[l], ln_x_b[l], w_proj_b[l], w_out[l], g_ffn[l],
               w_ffn1[l], w_ffn2[l], g_final)
    return h.reshape(batch, seq, d)
```

```python
import functools
import math

import jax
import jax.numpy as jnp
from jax import lax
from jax.experimental import pallas as pl
from jax.experimental.pallas import tpu as pltpu

F32 = jnp.float32
BF16 = jnp.bfloat16

SGU_CHUNK = 128
SGU_GROUPS = 8
RWKV_HEAD = 64
NORM_EPS = 1e-6
LN_EPS = 1e-5
GN_EPS = 64e-5

LANES = 128
PREV_ROWS = 16
SCAN_CHUNK = 64
SCAN_GROUP_HEADS = 4
VMEM_LIMIT = 56 * 1024 * 1024
SCAN_HILO = dict(ar=False, bq=False, kq=False, v=False, bk=False, s0=False, a_ab=False,
                 a_ak=False, a_r=False, x=False, p=False, rhs=False, sa=False)


def _cparams(sem):
    return pltpu.CompilerParams(dimension_semantics=sem, vmem_limit_bytes=VMEM_LIMIT)


def _dot(a, b, dims=(((1,), (0,)), ((), ()))):
    return lax.dot_general(a, b, dims, preferred_element_type=F32)


_NT = (((1,), (1,)), ((), ()))
_TN = (((0,), (0,)), ((), ()))


def _split2(x):
    hi = x.astype(BF16)
    lo = (x - hi.astype(F32)).astype(BF16)
    return hi, lo


def _split3(x):
    hi = x.astype(BF16)
    r1 = x - hi.astype(F32)
    mid = r1.astype(BF16)
    lo = (r1 - mid.astype(F32)).astype(BF16)
    return hi, mid, lo


def _split1(x):
    return x.astype(BF16), None


def _dot_hp(a2, b2, dims=(((1,), (0,)), ((), ()))):
    ah, al = a2
    bh, bl = b2
    out = _dot(ah, bh, dims)
    if bl is not None:
        out = out + _dot(ah, bl, dims)
    if al is not None:
        out = out + _dot(al, bh, dims)
    return out


def _dot_x2(x, w_exact):
    hi, lo = _split2(x)
    return _dot(hi, w_exact) + _dot(lo, w_exact)


def _head_sum(x, hb_ref, hilo=True):
    w = hb_ref.shape[0]
    hb = hb_ref[...]
    one = _dot_x2 if hilo else (lambda xs, m: _dot(xs.astype(BF16), m))
    return jnp.concatenate(
        [one(x[:, c * w:(c + 1) * w], hb) for c in range(x.shape[1] // w)], axis=1)


def _rms_rows(x, g):
    ms = jnp.mean(x * x, axis=-1, keepdims=True)
    return (x * lax.rsqrt(ms + NORM_EPS)) * g


def _gelu(x):
    return 0.5 * x * (1.0 + lax.erf(x * (1.0 / math.sqrt(2.0))))


def _norm_tile(x_ref, xp_ref, g_ref, a_scr, ap_scr, row_chunk):
    def body(c, carry):
        rows = pl.ds(pl.multiple_of(c * row_chunk, row_chunk), row_chunk)
        a_scr[rows, :] = _rms_rows(x_ref[rows, :], g_ref[...]).astype(BF16)
        return carry
    lax.fori_loop(0, x_ref.shape[0] // row_chunk, body, 0)
    ap_scr[...] = _rms_rows(xp_ref[...], g_ref[...]).astype(BF16)


def _shift_mix_tile(a_scr, ap_scr, w_ref, sb_ref, first, ncols=None):
    cols = slice(0, w_ref.shape[1] if ncols is None else ncols)
    w = w_ref[:, cols]
    p = _dot(a_scr[...], w)
    pp = _dot(ap_scr[...], w)
    prev_row = jnp.where(first, 0.0, pp[PREV_ROWS - 1:, :])
    row = lax.broadcasted_iota(jnp.int32, p.shape, 0)
    shifted = jnp.where(row == 0, prev_row, pltpu.roll(p, 1, 0))
    return p * sb_ref[0:1, cols] + shifted * sb_ref[1:2, cols]


def _in_proj_kernel(x_ref, xp_ref, g_ref, w_ref, sb_ref, lnw_ref, lnb_ref,
                    zu_ref, vn_ref, p_ref, lo_ref, a_scr, ap_scr, *,
                    row_chunk, tiles_per_seq, n_mix, n_tiles):
    i = pl.program_id(0)
    j = pl.program_id(1)

    @pl.when(j == 0)
    def _():
        _norm_tile(x_ref, xp_ref, g_ref, a_scr, ap_scr, row_chunk)
        zu_ref[...] = _gelu(_dot(a_scr[...], w_ref[...])).astype(BF16)

    @pl.when(j == 1)
    def _():
        for c in range(x_ref.shape[0] // row_chunk):
            rows = pl.ds(c * row_chunk, row_chunk)
            zv = _gelu(_dot(a_scr[rows, :], w_ref[...]))
            mu = jnp.mean(zv, axis=-1, keepdims=True)
            zc = zv - mu
            var = jnp.mean(zc * zc, axis=-1, keepdims=True)
            vn_ref[rows, :] = ((zc * lax.rsqrt(var + LN_EPS)) * lnw_ref[...]
                               + lnb_ref[...]).astype(BF16)

    @pl.when(jnp.logical_and(j >= 2, j < 2 + n_mix))
    def _():
        p_ref[...] = _shift_mix_tile(a_scr, ap_scr, w_ref, sb_ref,
                                     (i % tiles_per_seq) == 0).astype(p_ref.dtype)

    @pl.when(jnp.logical_and(j >= 2 + n_mix, j < n_tiles - 1))
    def _():
        p_ref[...] = _dot(a_scr[...], w_ref[...]).astype(p_ref.dtype)

    @pl.when(j == n_tiles - 1)
    def _():
        lo_ref[...] = _shift_mix_tile(a_scr, ap_scr, w_ref, sb_ref, (i % tiles_per_seq) == 0,
                                      ncols=lo_ref.shape[1])


def _in_proj(x2, g, w16, sb, ln_w, ln_b, seq, tm, n_mix, n_lora):
    m, d = x2.shape
    n_tiles = w16.shape[1] // d
    n_main = n_tiles - 3
    assert seq % tm == 0 and w16.shape[1] % d == 0
    row = lambda r: pl.BlockSpec((r, d), lambda i, j: (0, 0))
    return pl.pallas_call(
        functools.partial(_in_proj_kernel, row_chunk=256, tiles_per_seq=seq // tm, n_mix=n_mix,
                          n_tiles=n_tiles),
        grid=(m // tm, n_tiles),
        in_specs=[
            pl.BlockSpec((tm, d), lambda i, j: (i, 0)),
            pl.BlockSpec((PREV_ROWS, d),
                         lambda i, j: (jnp.maximum(i * (tm // PREV_ROWS) - 1, 0), 0)),
            row(1),
            pl.BlockSpec((d, d), lambda i, j: (0, j)),
            pl.BlockSpec((2, d), lambda i, j: (0, j)),
            row(1), row(1),
        ],
        out_specs=[
            pl.BlockSpec((tm, d), lambda i, j: (i, 0)),
            pl.BlockSpec((tm, d), lambda i, j: (i, 0)),
            pl.BlockSpec((tm, d), lambda i, j: (i, jnp.clip(j - 2, 0, n_main - 1))),
            pl.BlockSpec((tm, n_lora), lambda i, j: (i, 0)),
        ],
        out_shape=[jax.ShapeDtypeStruct((m, d), BF16), jax.ShapeDtypeStruct((m, d), BF16),
                   jax.ShapeDtypeStruct((m, n_main * d), BF16),
                   jax.ShapeDtypeStruct((m, n_lora), F32)],
        scratch_shapes=[pltpu.VMEM((tm, d), BF16), pltpu.VMEM((PREV_ROWS, d), BF16)],
        compiler_params=_cparams(("parallel", "arbitrary")),
        name="in_proj",
    )(x2, x2, g, w16, sb, ln_w, ln_b)


def _sgu_kernel(zu_ref, vn_ref, ga_ref, ws_ref, bias_ref, wpa_ref, o_ref, wm_scr, s_scr):
    ch = SGU_CHUNK
    dg = LANES

    @pl.when(pl.program_id(0) == 0)
    def _():
        row = lax.broadcasted_iota(jnp.int32, (ch, ch), 0)
        col = lax.broadcasted_iota(jnp.int32, (ch, ch), 1)
        for g in range(SGU_GROUPS):
            wm_scr[g] = jnp.where(col <= row, ws_ref[g], 0.0).astype(BF16)

    for c in range(zu_ref.shape[0] // ch):
        rows = pl.ds(c * ch, ch)
        for g in range(SGU_GROUPS):
            cols = slice(g * dg, (g + 1) * dg)
            sv = _dot(wm_scr[g], vn_ref[rows, cols]) + bias_ref[:, cols]
            s_scr[rows, cols] = (zu_ref[rows, cols] * sv).astype(BF16)

    ya = _dot(s_scr[...], wpa_ref[...])
    o_ref[...] = (jax.nn.sigmoid(ga_ref[...].astype(F32)) * ya).astype(o_ref.dtype)


def _sgu(zu, vn, p_main, sgu_w, bias_full, wpa16, ts, d):
    m = zu.shape[0]
    ch = SGU_CHUNK
    tile = pl.BlockSpec((ts, d), lambda i: (i, 0))
    return pl.pallas_call(
        _sgu_kernel,
        grid=(m // ts,),
        in_specs=[
            tile, tile,
            pl.BlockSpec((ts, d), lambda i: (i, 3)),
            pl.BlockSpec((SGU_GROUPS, ch, ch), lambda i: (0, 0, 0)),
            pl.BlockSpec((ch, d), lambda i: (0, 0)),
            pl.BlockSpec((d, d), lambda i: (0, 0)),
        ],
        out_specs=tile,
        out_shape=jax.ShapeDtypeStruct((m, d), BF16),
        scratch_shapes=[pltpu.VMEM((SGU_GROUPS, ch, ch), BF16), pltpu.VMEM((ts, d), BF16)],
        compiler_params=_cparams(("arbitrary",)),
        name="sgu",
    )(zu, vn, p_main, sgu_w, bias_full, wpa16)


def _rwkv_prep_kernel(qk_ref, ql_ref, ww_ref, wa_ref, wg_ref, vec_ref, hb_ref,
                      k_out, lw_out, kk_out, aa_out, g_out):
    k = qk_ref[...].astype(F32)
    lo = ql_ref[...]
    xw = lo[:, 0:LANES]
    xa = lo[:, LANES:2 * LANES]
    xg = lo[:, 2 * LANES:]
    w0 = vec_ref[0:1, :]
    a0 = vec_ref[1:2, :]
    k_k = vec_ref[2:3, :]
    k_a = vec_ref[3:4, :]

    zw = w0 + _dot(jnp.tanh(xw).astype(BF16), ww_ref[...])
    lw_out[...] = (-math.exp(-0.5)) * jax.nn.sigmoid(zw)
    aa = jax.nn.sigmoid(a0 + _dot(xa.astype(BF16), wa_ref[...]))
    g_out[...] = _dot(jax.nn.sigmoid(xg).astype(BF16), wg_ref[...]).astype(g_out.dtype)
    kraw = k * k_k
    ss = _head_sum(kraw * kraw, hb_ref)
    kk_out[...] = (kraw * lax.rsqrt(jnp.maximum(ss, 1e-24))).astype(kk_out.dtype)
    k_out[...] = (k * (1.0 + (aa - 1.0) * k_a)).astype(k_out.dtype)
    aa_out[...] = aa.astype(aa_out.dtype)


def _rwkv_prep(p_main, p_lora, ww, wa, wg, vecs, hb, d, tt):
    m = p_main.shape[0]
    lw = p_lora.shape[1]
    full = lambda a: pl.BlockSpec(a.shape, lambda i: (0,) * a.ndim)
    out_spec = pl.BlockSpec((tt, d), lambda i: (i, 0))
    return pl.pallas_call(
        _rwkv_prep_kernel,
        grid=(m // tt,),
        in_specs=[
            pl.BlockSpec((tt, d), lambda i: (i, 1)),
            pl.BlockSpec((tt, lw), lambda i: (i, 0)),
            full(ww), full(wa), full(wg), full(vecs), full(hb),
        ],
        out_specs=[out_spec] * 5,
        out_shape=[jax.ShapeDtypeStruct((m, d), dt) for dt in (BF16, F32, BF16, BF16, BF16)],
        compiler_params=_cparams(("parallel",)),
        name="rwkv_prep",
    )(p_main, p_lora, ww, wa, wg, vecs, hb)


def _scan_kernel(r_ref, k_ref, v_ref, lw_ref, kk_ref, aa_ref, o_ref, s_scr, *, chunk, hd, gw):
    c_len = chunk
    assert c_len == hd
    bb, tblk, d = r_ref.shape
    n_groups = d // gw
    hpg = gw // hd
    n_double = int(math.log2(c_len)) - 1

    rowi = lax.broadcasted_iota(jnp.int32, (c_len, gw), 0)
    sub = lax.broadcasted_iota(jnp.int32, (c_len, gw), 1) % c_len
    strict = sub < rowi
    incl = sub <= rowi
    eye = jnp.where(sub == rowi, 1.0, 0.0).astype(F32)
    bd_mask = (lax.broadcasted_iota(jnp.int32, (gw, gw), 0) // c_len
               == lax.broadcasted_iota(jnp.int32, (gw, gw), 1) // hd)
    ltri = jnp.where(lax.broadcasted_iota(jnp.int32, (c_len, c_len), 1)
                     <= lax.broadcasted_iota(jnp.int32, (c_len, c_len), 0), 1.0, 0.0).astype(BF16)

    @pl.when(pl.program_id(1) == 0)
    def _():
        s_scr[...] = jnp.zeros_like(s_scr)

    def bd(x16):
        t = jnp.concatenate([x16] * hpg, axis=0)
        return jnp.where(bd_mask, t, jnp.zeros_like(t))

    chains = [(b, g) for b in range(bb) for g in range(n_groups)]
    n = range(len(chains))

    def body(c, carry):
        rows = pl.ds(pl.multiple_of(c * c_len, c_len), c_len)

        def at(ref, i):
            b, g = chains[i]
            return ref.at[b, rows, g * gw:(g + 1) * gw]

        ar, bq, kq, v16, bk, g_tot = [], [], [], [], [], []
        for i in n:
            k = at(k_ref, i)[...].astype(F32)
            lw = at(lw_ref, i)[...]
            kk = at(kk_ref, i)[...].astype(F32)
            b = kk * at(aa_ref, i)[...].astype(F32)
            lh, lm, ll = _split3(lw)
            cum = _dot(ltri, lh) + (_dot(ltri, lm) + _dot(ltri, ll))
            cum_last = cum[c_len - 1:c_len, :]
            g_inv = jnp.exp(-cum)
            g_end = jnp.exp(cum_last - cum)
            aq = -(kk * jnp.exp(cum - lw))
            rq = at(r_ref, i)[...].astype(F32) * jnp.exp(cum)
            ar.append(jnp.concatenate([aq, rq], axis=0).astype(BF16))
            bq.append((b * g_inv).astype(BF16))
            kq.append((k * g_inv).astype(BF16))
            v16.append(at(v_ref, i)[...].astype(BF16))
            bk.append(jnp.concatenate([b * g_end, k * g_end], axis=0).astype(BF16))
            g_tot.append(jnp.exp(cum_last))

        pb = [_dot(ar[i], bd(bq[i]), _NT) for i in n]
        pk = [_dot(ar[i], bd(kq[i]), _NT) for i in n]
        s0 = [s_scr[chains[i]] for i in n]
        ar_s = [_dot(ar[i], s0[i].astype(BF16), _NT) for i in n]
        a_ab = [jnp.where(strict, pb[i][:c_len], 0.0) for i in n]
        a_rb = [jnp.where(incl, pb[i][c_len:], 0.0).astype(BF16) for i in n]
        akv = [_dot(jnp.concatenate([jnp.where(strict, pk[i][:c_len], 0.0),
                                     jnp.where(incl, pk[i][c_len:], 0.0)], axis=0).astype(BF16),
                    bd(v16[i])) for i in n]
        rhs = [ar_s[i][:c_len] + akv[i][:c_len] for i in n]

        p = [_dot(a_ab[i].astype(BF16), bd(a_ab[i].astype(BF16))) for i in n]
        x = [eye + a_ab[i] for i in n]
        for _ in range(n_double - 1):
            xp = [_dot(jnp.concatenate([x[i], p[i]], axis=0).astype(BF16), bd(p[i].astype(BF16)))
                  for i in n]
            x = [x[i] + xp[i][:c_len] for i in n]
            p = [xp[i][c_len:] for i in n]
        x = [x[i] + _dot(x[i].astype(BF16), bd(p[i].astype(BF16))) for i in n]

        sa16 = [_dot(x[i].astype(BF16), bd(rhs[i].astype(BF16))).astype(BF16) for i in n]
        for i in n:
            at(o_ref, i)[...] = ((ar_s[i][c_len:] + akv[i][c_len:])
                                 + _dot(a_rb[i], bd(sa16[i]))).astype(o_ref.dtype)
        for i in n:
            upd = _dot(jnp.concatenate([sa16[i], v16[i]], axis=0), bk[i], _TN)
            s_scr[chains[i]] = jnp.where(bd_mask, s0[i] * g_tot[i] + upd, 0.0)
        return carry

    lax.fori_loop(0, tblk // c_len, body, 0)


def _rwkv_scan(p_main, k, lw, kk, aa, batch, seq, d, bb, tblk):
    gw = SCAN_GROUP_HEADS * RWKV_HEAD
    spec = pl.BlockSpec((bb, tblk, d), lambda b, t: (b, t, 0))
    col = lambda cb: pl.BlockSpec((bb, tblk, d), lambda b, t: (b, t, cb))
    as3 = lambda a: a.reshape(batch, seq, a.shape[-1])
    out = pl.pallas_call(
        functools.partial(_scan_kernel, chunk=SCAN_CHUNK, hd=RWKV_HEAD, gw=gw),
        grid=(batch // bb, seq // tblk),
        in_specs=[col(0), spec, col(2), spec, spec, spec],
        out_specs=spec,
        out_shape=jax.ShapeDtypeStruct((batch, seq, d), BF16),
        scratch_shapes=[pltpu.VMEM((bb, d // gw, gw, gw), F32)],
        compiler_params=_cparams(("parallel", "arbitrary")),
        name="rwkv_scan",
    )(as3(p_main), as3(k), as3(p_main), as3(lw), as3(kk), as3(aa))
    return out.reshape(batch * seq, d)


def _rwkv_post_kernel(o_ref, r_ref, k_ref, v_ref, g_ref, gb_ref, ya_ref, x_ref,
                      vec_ref, hb_ref, wpb_ref, wout_ref, h_ref, f_ref, *, hd):
    lnx_w = vec_ref[0:1, :]
    lnx_b = vec_ref[1:2, :]
    r_k = vec_ref[2:3, :]
    g_ffn = vec_ref[3:4, :]
    o = o_ref[...].astype(F32)
    inv_n = 1.0 / hd
    mu = _head_sum(o, hb_ref, hilo=False) * inv_n
    oc = o - mu
    var = _head_sum(oc * oc, hb_ref, hilo=False) * inv_n
    on = (oc * lax.rsqrt(var + GN_EPS)) * lnx_w + lnx_b
    v = v_ref[...].astype(F32)
    rk = r_ref[...].astype(F32) * k_ref[...].astype(F32)
    bonus = _head_sum(rk * r_k, hb_ref, hilo=False) * v
    yb = _dot(((on + bonus) * g_ref[...]).astype(BF16), wpb_ref[...])
    mixed = ya_ref[...].astype(F32) + jax.nn.sigmoid(gb_ref[...].astype(F32)) * yb
    h = x_ref[...] + _dot(mixed.astype(BF16), wout_ref[...])
    h_ref[...] = h
    f_ref[...] = _rms_rows(h, g_ffn).astype(BF16)


def _rwkv_post(o, k, g, p_main, ya, x2, vecs, hb, wpb16, wout16, tm, d):
    m = x2.shape[0]
    tile = pl.BlockSpec((tm, d), lambda i: (i, 0))
    col = lambda cb: pl.BlockSpec((tm, d), lambda i: (i, cb))
    full = lambda a: pl.BlockSpec(a.shape, lambda i: (0,) * a.ndim)
    return pl.pallas_call(
        functools.partial(_rwkv_post_kernel, hd=RWKV_HEAD),
        grid=(m // tm,),
        in_specs=[tile, col(0), tile, col(2), tile, col(4),
                  tile, tile, full(vecs), full(hb), full(wpb16), full(wout16)],
        out_specs=[tile, tile],
        out_shape=[jax.ShapeDtypeStruct((m, d), F32), jax.ShapeDtypeStruct((m, d), BF16)],
        compiler_params=_cparams(("parallel",)),
        name="rwkv_post",
    )(o, p_main, k, p_main, g, p_main, ya, x2, vecs, hb, wpb16, wout16)


def _ffn_kernel(h_ref, f_ref, w1_ref, w2_ref, gf_ref, o_ref, acc_scr):
    j = pl.program_id(1)
    last = pl.num_programs(1) - 1

    def partial_sum():
        t = jnp.maximum(_dot(f_ref[...], w1_ref[...]), 0.0)
        return _dot((t * t).astype(BF16), w2_ref[...])

    @pl.when(j == 0)
    def _():
        acc_scr[...] = partial_sum()

    @pl.when(jnp.logical_and(j > 0, j < last))
    def _():
        acc_scr[...] += partial_sum()

    @pl.when(j == last)
    def _():
        o_ref[...] = _rms_rows(h_ref[...] + (acc_scr[...] + partial_sum()), gf_ref[...])


def _ffn(h, f, w1_16, w2_16, g_final, tm, tf):
    m, d = h.shape
    dff = w1_16.shape[1]
    assert dff // tf >= 2
    tile = pl.BlockSpec((tm, d), lambda i, j: (i, 0))
    return pl.pallas_call(
        _ffn_kernel,
        grid=(m // tm, dff // tf),
        in_specs=[
            tile, tile,
            pl.BlockSpec((d, tf), lambda i, j: (0, j)),
            pl.BlockSpec((tf, d), lambda i, j: (j, 0)),
            pl.BlockSpec((1, d), lambda i, j: (0, 0)),
        ],
        out_specs=tile,
        out_shape=jax.ShapeDtypeStruct((m, d), F32),
        scratch_shapes=[pltpu.VMEM((tm, d), F32)],
        compiler_params=_cparams(("parallel", "arbitrary")),
        name="ffn",
    )(h, f, w1_16, w2_16, g_final)


def _pad_cols(a, n):
    return jnp.pad(a, ((0, 0), (0, n - a.shape[1])))


def _pad_rows(a, n):
    return jnp.pad(a, ((0, n - a.shape[0]), (0, 0)))


def _layer(x2, batch, seq, g_mix, w_in, sgu_ln_w, sgu_ln_b, sgu_w, sgu_b, w_proj_a, shift_b,
           w_lora_w, w0, a_lora_w, a0, g_lora_w, k_k, k_a, r_k, ln_x_w, ln_x_b, w_proj_b,
           w_out, g_ffn, w_ffn1, w_ffn2, g_out):
    d = x2.shape[1]
    lora_w, lora_a, lora_g = w_lora_w.shape[0], a_lora_w.shape[0], g_lora_w.shape[0]
    c_sgu = 2 * d
    c_rkv = 3 * d
    c_lora = lora_w + lora_a + lora_g
    o_lora = c_sgu + c_rkv
    o_gate = o_lora + c_lora
    pw, pa = LANES, LANES
    pg = -(-lora_g // LANES) * LANES

    n_lora = pw + pa + pg
    w16 = w_in.astype(BF16)
    w_all = jnp.concatenate([
        w16[:, :o_lora], w16[:, o_gate:],
        _pad_cols(w16[:, o_lora:o_lora + lora_w], pw),
        _pad_cols(w16[:, o_lora + lora_w:o_lora + lora_w + lora_a], pa),
        _pad_cols(w16[:, o_lora + lora_w + lora_a:o_gate], pg + d - n_lora)], axis=1)
    sb_lo = shift_b[:, c_rkv:]
    sb_all = jnp.concatenate([
        jnp.zeros((2, c_sgu), F32), shift_b[:, :c_rkv], jnp.zeros((2, 2 * d), F32),
        _pad_cols(sb_lo[:, :lora_w], pw),
        _pad_cols(sb_lo[:, lora_w:lora_w + lora_a], pa),
        _pad_cols(sb_lo[:, lora_w + lora_a:], pg + d - n_lora)], axis=1)

    g_mix2 = g_mix.reshape(1, d)
    zu, vn, p_main, p_lora = _in_proj(
        x2, g_mix2, w_all, sb_all, sgu_ln_w.reshape(1, d), sgu_ln_b.reshape(1, d), seq,
        tm=1024, n_mix=c_rkv // d, n_lora=n_lora)

    bias_full = jnp.repeat(sgu_b.T, d // SGU_GROUPS, axis=1)
    ya = _sgu(zu, vn, p_main, sgu_w, bias_full, w_proj_a.astype(BF16), ts=512, d=d)

    gw = SCAN_GROUP_HEADS * RWKV_HEAD
    hb = (lax.broadcasted_iota(jnp.int32, (gw, gw), 0) // RWKV_HEAD
          == lax.broadcasted_iota(jnp.int32, (gw, gw), 1) // RWKV_HEAD).astype(BF16)
    vec_prep = jnp.stack([w0, a0, k_k, k_a])
    lo16 = lambda w, rows: _pad_rows(w, rows).astype(BF16)
    k, lw, kk, aa, g = _rwkv_prep(
        p_main, p_lora, lo16(w_lora_w, pw), lo16(a_lora_w, pa), lo16(g_lora_w, pg),
        vec_prep, hb, d, tt=512)

    o = _rwkv_scan(p_main, k, lw, kk, aa, batch, seq, d, bb=4, tblk=256)

    vec_post = jnp.stack([ln_x_w, ln_x_b, r_k, g_ffn])
    h1, f = _rwkv_post(o, k, g, p_main, ya, x2, vec_post, hb, w_proj_b.astype(BF16),
                       w_out.astype(BF16), tm=512, d=d)
    return _ffn(h1, f, w_ffn1.astype(BF16), w_ffn2.astype(BF16), g_out.reshape(1, d),
                tm=1024, tf=1024)


def kernel(x, g_mix, w_in, sgu_ln_w, sgu_ln_b, sgu_w, sgu_b, w_proj_a, shift_b, w_lora_w, w0,
           a_lora_w, a0, g_lora_w, k_k, k_a, r_k, ln_x_w, ln_x_b, w_proj_b, w_out, g_ffn,
           w_ffn1, w_ffn2, g_final):
    batch, seq, d = x.shape
    depth = w_in.shape[0]
    assert depth == 1, "the final RMSNorm is fused into the single layer's ffn call"
    h = x.reshape(batch * seq, d)
    l = 0
    h = _layer(h, batch, seq, g_mix[l], w_in[l], sgu_ln_w[l], sgu_ln_b[l], sgu_w[l], sgu_b[l],
               w_proj_a[l], shift_b[l], w_lora_w[l], w0[l], a_lora_w[l], a0[l], g_lora_w[l],
               k_k[l], k_a[l], r_k[l], ln_x_w[l], ln_x_b[l], w_proj_b[l], w_out[l], g_ffn[l],
               w_ffn1[l], w_ffn2[l], g_final)
    return h.reshape(batch, seq, d)
```

```python
import functools
import math

import jax
import jax.numpy as jnp
from jax import lax
from jax.experimental import pallas as pl
from jax.experimental.pallas import tpu as pltpu

F32 = jnp.float32
BF16 = jnp.bfloat16

SGU_CHUNK = 128
SGU_GROUPS = 8
RWKV_HEAD = 64
NORM_EPS = 1e-6
LN_EPS = 1e-5
GN_EPS = 64e-5

LANES = 128
PREV_ROWS = 16
SCAN_CHUNK = 64
SCAN_GROUP_HEADS = 4
VMEM_LIMIT = 56 * 1024 * 1024
SCAN_HILO = dict(ar=False, bq=False, kq=False, v=False, bk=False, s0=False, a_ab=False,
                 a_ak=False, a_r=False, x=False, p=False, rhs=False, sa=False)


def _cparams(sem):
    return pltpu.CompilerParams(dimension_semantics=sem, vmem_limit_bytes=VMEM_LIMIT)


def _dot(a, b, dims=(((1,), (0,)), ((), ()))):
    return lax.dot_general(a, b, dims, preferred_element_type=F32)


_NT = (((1,), (1,)), ((), ()))
_TN = (((0,), (0,)), ((), ()))


def _split2(x):
    hi = x.astype(BF16)
    lo = (x - hi.astype(F32)).astype(BF16)
    return hi, lo


def _split3(x):
    hi = x.astype(BF16)
    r1 = x - hi.astype(F32)
    mid = r1.astype(BF16)
    lo = (r1 - mid.astype(F32)).astype(BF16)
    return hi, mid, lo


def _split1(x):
    return x.astype(BF16), None


def _dot_hp(a2, b2, dims=(((1,), (0,)), ((), ()))):
    ah, al = a2
    bh, bl = b2
    out = _dot(ah, bh, dims)
    if bl is not None:
        out = out + _dot(ah, bl, dims)
    if al is not None:
        out = out + _dot(al, bh, dims)
    return out


def _dot_x2(x, w_exact):
    hi, lo = _split2(x)
    return _dot(hi, w_exact) + _dot(lo, w_exact)


def _head_sum(x, hb_ref, hilo=True):
    w = hb_ref.shape[0]
    hb = hb_ref[...]
    one = _dot_x2 if hilo else (lambda xs, m: _dot(xs.astype(BF16), m))
    return jnp.concatenate(
        [one(x[:, c * w:(c + 1) * w], hb) for c in range(x.shape[1] // w)], axis=1)


def _rms_rows(x, g):
    ms = jnp.mean(x * x, axis=-1, keepdims=True)
    return (x * lax.rsqrt(ms + NORM_EPS)) * g


def _gelu(x):
    return 0.5 * x * (1.0 + lax.erf(x * (1.0 / math.sqrt(2.0))))


def _shift_mix_tile(a_scr, ap_scr, w_ref, sb_ref, first, cols):
    w = w_ref[:, cols]
    p = _dot(a_scr[...], w)
    pp = _dot(ap_scr[...], w)
    prev_row = jnp.where(first, 0.0, pp[PREV_ROWS - 1:, :])
    row = lax.broadcasted_iota(jnp.int32, p.shape, 0)
    shifted = jnp.where(row == 0, prev_row, pltpu.roll(p, 1, 0))
    return p * sb_ref[0:1, cols] + shifted * sb_ref[1:2, cols]


def _in_proj_kernel(x_ref, xp_ref, g_ref, w_ref, sb_ref, lnw_ref, lnb_ref,
                    zu_ref, vn_ref, p_ref, lo_ref, a_scr, ap_scr, *, tiles_per_seq, n_mix):
    d = x_ref.shape[1]
    n_main = p_ref.shape[1] // d
    first = (pl.program_id(0) % tiles_per_seq) == 0
    tile = lambda t: slice(t * d, (t + 1) * d)

    a_scr[...] = _rms_rows(x_ref[...], g_ref[...]).astype(BF16)
    ap_scr[...] = _rms_rows(xp_ref[...], g_ref[...]).astype(BF16)

    zu_ref[...] = _gelu(_dot(a_scr[...], w_ref[:, tile(0)])).astype(BF16)

    zv = _gelu(_dot(a_scr[...], w_ref[:, tile(1)]))
    mu = jnp.mean(zv, axis=-1, keepdims=True)
    zc = zv - mu
    var = jnp.mean(zc * zc, axis=-1, keepdims=True)
    vn_ref[...] = ((zc * lax.rsqrt(var + LN_EPS)) * lnw_ref[...] + lnb_ref[...]).astype(BF16)

    for t in range(n_main):
        if t < n_mix:
            p = _shift_mix_tile(a_scr, ap_scr, w_ref, sb_ref, first, tile(2 + t))
        else:
            p = _dot(a_scr[...], w_ref[:, tile(2 + t)])
        p_ref[:, tile(t)] = p.astype(p_ref.dtype)

    lo0 = (2 + n_main) * d
    lo_ref[...] = _shift_mix_tile(a_scr, ap_scr, w_ref, sb_ref, first,
                                  slice(lo0, lo0 + lo_ref.shape[1]))


def _in_proj(x2, g, w16, sb, ln_w, ln_b, seq, tm, n_mix, n_lora):
    m, d = x2.shape
    n = w16.shape[1]
    n_main = n // d - 3
    assert seq % tm == 0 and n % d == 0
    const = lambda shape: pl.BlockSpec(shape, lambda i: (0, 0), pipeline_mode=pl.Buffered(1))
    rows = lambda w: pl.BlockSpec((tm, w), lambda i: (i, 0))
    return pl.pallas_call(
        functools.partial(_in_proj_kernel, tiles_per_seq=seq // tm, n_mix=n_mix),
        grid=(m // tm,),
        in_specs=[
            rows(d),
            pl.BlockSpec((PREV_ROWS, d), lambda i: (jnp.maximum(i * (tm // PREV_ROWS) - 1, 0), 0)),
            const((1, d)), const((d, n)), const((2, n)), const((1, d)), const((1, d)),
        ],
        out_specs=[rows(d), rows(d), rows(n_main * d), rows(n_lora)],
        out_shape=[jax.ShapeDtypeStruct((m, d), BF16), jax.ShapeDtypeStruct((m, d), BF16),
                   jax.ShapeDtypeStruct((m, n_main * d), BF16),
                   jax.ShapeDtypeStruct((m, n_lora), F32)],
        scratch_shapes=[pltpu.VMEM((tm, d), BF16), pltpu.VMEM((PREV_ROWS, d), BF16)],
        compiler_params=_cparams(("parallel",)),
        name="in_proj",
    )(x2, x2, g, w16, sb, ln_w, ln_b)


def _sgu_kernel(zu_ref, vn_ref, ga_ref, ws_ref, bias_ref, wpa_ref, o_ref, wm_scr, s_scr):
    ch = SGU_CHUNK
    dg = LANES

    @pl.when(pl.program_id(0) == 0)
    def _():
        row = lax.broadcasted_iota(jnp.int32, (ch, ch), 0)
        col = lax.broadcasted_iota(jnp.int32, (ch, ch), 1)
        for g in range(SGU_GROUPS):
            wm_scr[g] = jnp.where(col <= row, ws_ref[g], 0.0).astype(BF16)

    for c in range(zu_ref.shape[0] // ch):
        rows = pl.ds(c * ch, ch)
        for g in range(SGU_GROUPS):
            cols = slice(g * dg, (g + 1) * dg)
            sv = _dot(wm_scr[g], vn_ref[rows, cols]) + bias_ref[:, cols]
            s_scr[rows, cols] = (zu_ref[rows, cols] * sv).astype(BF16)

    ya = _dot(s_scr[...], wpa_ref[...])
    o_ref[...] = (jax.nn.sigmoid(ga_ref[...].astype(F32)) * ya).astype(o_ref.dtype)


def _sgu(zu, vn, p_main, sgu_w, bias_full, wpa16, ts, d):
    m = zu.shape[0]
    ch = SGU_CHUNK
    tile = pl.BlockSpec((ts, d), lambda i: (i, 0))
    return pl.pallas_call(
        _sgu_kernel,
        grid=(m // ts,),
        in_specs=[
            tile, tile,
            pl.BlockSpec((ts, d), lambda i: (i, 3)),
            pl.BlockSpec((SGU_GROUPS, ch, ch), lambda i: (0, 0, 0)),
            pl.BlockSpec((ch, d), lambda i: (0, 0)),
            pl.BlockSpec((d, d), lambda i: (0, 0)),
        ],
        out_specs=tile,
        out_shape=jax.ShapeDtypeStruct((m, d), BF16),
        scratch_shapes=[pltpu.VMEM((SGU_GROUPS, ch, ch), BF16), pltpu.VMEM((ts, d), BF16)],
        compiler_params=_cparams(("arbitrary",)),
        name="sgu",
    )(zu, vn, p_main, sgu_w, bias_full, wpa16)


def _rwkv_prep_kernel(qk_ref, ql_ref, ww_ref, wa_ref, wg_ref, vec_ref, hb_ref,
                      k_out, lw_out, kk_out, aa_out, g_out):
    k = qk_ref[...].astype(F32)
    lo = ql_ref[...]
    xw = lo[:, 0:LANES]
    xa = lo[:, LANES:2 * LANES]
    xg = lo[:, 2 * LANES:]
    w0 = vec_ref[0:1, :]
    a0 = vec_ref[1:2, :]
    k_k = vec_ref[2:3, :]
    k_a = vec_ref[3:4, :]

    zw = w0 + _dot(jnp.tanh(xw).astype(BF16), ww_ref[...])
    lw_out[...] = (-math.exp(-0.5)) * jax.nn.sigmoid(zw)
    aa = jax.nn.sigmoid(a0 + _dot(xa.astype(BF16), wa_ref[...]))
    g_out[...] = _dot(jax.nn.sigmoid(xg).astype(BF16), wg_ref[...]).astype(g_out.dtype)
    kraw = k * k_k
    ss = _head_sum(kraw * kraw, hb_ref)
    kk_out[...] = (kraw * lax.rsqrt(jnp.maximum(ss, 1e-24))).astype(kk_out.dtype)
    k_out[...] = (k * (1.0 + (aa - 1.0) * k_a)).astype(k_out.dtype)
    aa_out[...] = aa.astype(aa_out.dtype)


def _rwkv_prep(p_main, p_lora, ww, wa, wg, vecs, hb, d, tt):
    m = p_main.shape[0]
    lw = p_lora.shape[1]
    full = lambda a: pl.BlockSpec(a.shape, lambda i: (0,) * a.ndim)
    out_spec = pl.BlockSpec((tt, d), lambda i: (i, 0))
    return pl.pallas_call(
        _rwkv_prep_kernel,
        grid=(m // tt,),
        in_specs=[
            pl.BlockSpec((tt, d), lambda i: (i, 1)),
            pl.BlockSpec((tt, lw), lambda i: (i, 0)),
            full(ww), full(wa), full(wg), full(vecs), full(hb),
        ],
        out_specs=[out_spec] * 5,
        out_shape=[jax.ShapeDtypeStruct((m, d), dt) for dt in (BF16, F32, BF16, BF16, BF16)],
        compiler_params=_cparams(("parallel",)),
        name="rwkv_prep",
    )(p_main, p_lora, ww, wa, wg, vecs, hb)


def _scan_kernel(r_ref, k_ref, v_ref, lw_ref, kk_ref, aa_ref, o_ref, s_scr, *, chunk, hd, gw):
    c_len = chunk
    assert c_len == hd
    bb, tblk, d = r_ref.shape
    n_groups = d // gw
    hpg = gw // hd
    n_double = int(math.log2(c_len)) - 1

    rowi = lax.broadcasted_iota(jnp.int32, (c_len, gw), 0)
    sub = lax.broadcasted_iota(jnp.int32, (c_len, gw), 1) % c_len
    strict = sub < rowi
    incl = sub <= rowi
    eye = jnp.where(sub == rowi, 1.0, 0.0).astype(F32)
    bd_mask = (lax.broadcasted_iota(jnp.int32, (gw, gw), 0) // c_len
               == lax.broadcasted_iota(jnp.int32, (gw, gw), 1) // hd)
    ltri = jnp.where(lax.broadcasted_iota(jnp.int32, (c_len, c_len), 1)
                     <= lax.broadcasted_iota(jnp.int32, (c_len, c_len), 0), 1.0, 0.0).astype(BF16)

    @pl.when(pl.program_id(1) == 0)
    def _():
        s_scr[...] = jnp.zeros_like(s_scr)

    def bd(x16):
        t = jnp.concatenate([x16] * hpg, axis=0)
        return jnp.where(bd_mask, t, jnp.zeros_like(t))

    chains = [(b, g) for b in range(bb) for g in range(n_groups)]
    n = range(len(chains))

    def body(c, carry):
        rows = pl.ds(pl.multiple_of(c * c_len, c_len), c_len)

        def at(ref, i):
            b, g = chains[i]
            return ref.at[b, rows, g * gw:(g + 1) * gw]

        ar, bq, kq, v16, bk, g_tot = [], [], [], [], [], []
        for i in n:
            k = at(k_ref, i)[...].astype(F32)
            lw = at(lw_ref, i)[...]
            kk = at(kk_ref, i)[...].astype(F32)
            b = kk * at(aa_ref, i)[...].astype(F32)
            lh, lm, ll = _split3(lw)
            cum = _dot(ltri, lh) + (_dot(ltri, lm) + _dot(ltri, ll))
            cum_last = cum[c_len - 1:c_len, :]
            g_inv = jnp.exp(-cum)
            g_end = jnp.exp(cum_last - cum)
            aq = -(kk * jnp.exp(cum - lw))
            rq = at(r_ref, i)[...].astype(F32) * jnp.exp(cum)
            ar.append(jnp.concatenate([aq, rq], axis=0).astype(BF16))
            bq.append((b * g_inv).astype(BF16))
            kq.append((k * g_inv).astype(BF16))
            v16.append(at(v_ref, i)[...].astype(BF16))
            bk.append(jnp.concatenate([b * g_end, k * g_end], axis=0).astype(BF16))
            g_tot.append(jnp.exp(cum_last))

        pb = [_dot(ar[i], bd(bq[i]), _NT) for i in n]
        pk = [_dot(ar[i], bd(kq[i]), _NT) for i in n]
        s0 = [s_scr[chains[i]] for i in n]
        ar_s = [_dot(ar[i], s0[i].astype(BF16), _NT) for i in n]
        a_ab = [jnp.where(strict, pb[i][:c_len], 0.0) for i in n]
        a_rb = [jnp.where(incl, pb[i][c_len:], 0.0).astype(BF16) for i in n]
        akv = [_dot(jnp.concatenate([jnp.where(strict, pk[i][:c_len], 0.0),
                                     jnp.where(incl, pk[i][c_len:], 0.0)], axis=0).astype(BF16),
                    bd(v16[i])) for i in n]
        rhs = [ar_s[i][:c_len] + akv[i][:c_len] for i in n]

        p = [_dot(a_ab[i].astype(BF16), bd(a_ab[i].astype(BF16))) for i in n]
        x = [eye + a_ab[i] for i in n]
        for _ in range(n_double - 1):
            xp = [_dot(jnp.concatenate([x[i], p[i]], axis=0).astype(BF16), bd(p[i].astype(BF16)))
                  for i in n]
            x = [x[i] + xp[i][:c_len] for i in n]
            p = [xp[i][c_len:] for i in n]
        x = [x[i] + _dot(x[i].astype(BF16), bd(p[i].astype(BF16))) for i in n]

        sa16 = [_dot(x[i].astype(BF16), bd(rhs[i].astype(BF16))).astype(BF16) for i in n]
        for i in n:
            at(o_ref, i)[...] = ((ar_s[i][c_len:] + akv[i][c_len:])
                                 + _dot(a_rb[i], bd(sa16[i]))).astype(o_ref.dtype)
        for i in n:
            upd = _dot(jnp.concatenate([sa16[i], v16[i]], axis=0), bk[i], _TN)
            s_scr[chains[i]] = jnp.where(bd_mask, s0[i] * g_tot[i] + upd, 0.0)
        return carry

    lax.fori_loop(0, tblk // c_len, body, 0)


def _rwkv_scan(p_main, k, lw, kk, aa, batch, seq, d, bb, tblk):
    gw = SCAN_GROUP_HEADS * RWKV_HEAD
    spec = pl.BlockSpec((bb, tblk, d), lambda b, t: (b, t, 0))
    col = lambda cb: pl.BlockSpec((bb, tblk, d), lambda b, t: (b, t, cb))
    as3 = lambda a: a.reshape(batch, seq, a.shape[-1])
    out = pl.pallas_call(
        functools.partial(_scan_kernel, chunk=SCAN_CHUNK, hd=RWKV_HEAD, gw=gw),
        grid=(batch // bb, seq // tblk),
        in_specs=[col(0), spec, col(2), spec, spec, spec],
        out_specs=spec,
        out_shape=jax.ShapeDtypeStruct((batch, seq, d), BF16),
        scratch_shapes=[pltpu.VMEM((bb, d // gw, gw, gw), F32)],
        compiler_params=_cparams(("parallel", "arbitrary")),
        name="rwkv_scan",
    )(as3(p_main), as3(k), as3(p_main), as3(lw), as3(kk), as3(aa))
    return out.reshape(batch * seq, d)


def _rwkv_post_kernel(o_ref, r_ref, k_ref, v_ref, g_ref, gb_ref, ya_ref, x_ref,
                      vec_ref, hb_ref, wpb_ref, wout_ref, h_ref, f_ref, *, hd):
    lnx_w = vec_ref[0:1, :]
    lnx_b = vec_ref[1:2, :]
    r_k = vec_ref[2:3, :]
    g_ffn = vec_ref[3:4, :]
    o = o_ref[...].astype(F32)
    inv_n = 1.0 / hd
    mu = _head_sum(o, hb_ref, hilo=False) * inv_n
    oc = o - mu
    var = _head_sum(oc * oc, hb_ref, hilo=False) * inv_n
    on = (oc * lax.rsqrt(var + GN_EPS)) * lnx_w + lnx_b
    v = v_ref[...].astype(F32)
    rk = r_ref[...].astype(F32) * k_ref[...].astype(F32)
    bonus = _head_sum(rk * r_k, hb_ref, hilo=False) * v
    yb = _dot(((on + bonus) * g_ref[...]).astype(BF16), wpb_ref[...])
    mixed = ya_ref[...].astype(F32) + jax.nn.sigmoid(gb_ref[...].astype(F32)) * yb
    h = x_ref[...] + _dot(mixed.astype(BF16), wout_ref[...])
    h_ref[...] = h
    f_ref[...] = _rms_rows(h, g_ffn).astype(BF16)


def _rwkv_post(o, k, g, p_main, ya, x2, vecs, hb, wpb16, wout16, tm, d):
    m = x2.shape[0]
    tile = pl.BlockSpec((tm, d), lambda i: (i, 0))
    col = lambda cb: pl.BlockSpec((tm, d), lambda i: (i, cb))
    full = lambda a: pl.BlockSpec(a.shape, lambda i: (0,) * a.ndim)
    return pl.pallas_call(
        functools.partial(_rwkv_post_kernel, hd=RWKV_HEAD),
        grid=(m // tm,),
        in_specs=[tile, col(0), tile, col(2), tile, col(4),
                  tile, tile, full(vecs), full(hb), full(wpb16), full(wout16)],
        out_specs=[tile, tile],
        out_shape=[jax.ShapeDtypeStruct((m, d), F32), jax.ShapeDtypeStruct((m, d), BF16)],
        compiler_params=_cparams(("parallel",)),
        name="rwkv_post",
    )(o, p_main, k, p_main, g, p_main, ya, x2, vecs, hb, wpb16, wout16)


def _ffn_kernel(h_ref, f_ref, w1_ref, w2_ref, gf_ref, o_ref, acc_scr):
    j = pl.program_id(1)
    last = pl.num_programs(1) - 1

    def partial_sum():
        t = jnp.maximum(_dot(f_ref[...], w1_ref[...]), 0.0)
        return _dot((t * t).astype(BF16), w2_ref[...])

    @pl.when(j == 0)
    def _():
        acc_scr[...] = partial_sum()

    @pl.when(jnp.logical_and(j > 0, j < last))
    def _():
        acc_scr[...] += partial_sum()

    @pl.when(j == last)
    def _():
        o_ref[...] = _rms_rows(h_ref[...] + (acc_scr[...] + partial_sum()), gf_ref[...])


def _ffn(h, f, w1_16, w2_16, g_final, tm, tf):
    m, d = h.shape
    dff = w1_16.shape[1]
    assert dff // tf >= 2
    tile = pl.BlockSpec((tm, d), lambda i, j: (i, 0))
    return pl.pallas_call(
        _ffn_kernel,
        grid=(m // tm, dff // tf),
        in_specs=[
            tile, tile,
            pl.BlockSpec((d, tf), lambda i, j: (0, j)),
            pl.BlockSpec((tf, d), lambda i, j: (j, 0)),
            pl.BlockSpec((1, d), lambda i, j: (0, 0)),
        ],
        out_specs=tile,
        out_shape=jax.ShapeDtypeStruct((m, d), F32),
        scratch_shapes=[pltpu.VMEM((tm, d), F32)],
        compiler_params=_cparams(("parallel", "arbitrary")),
        name="ffn",
    )(h, f, w1_16, w2_16, g_final)


def _pad_cols(a, n):
    return jnp.pad(a, ((0, 0), (0, n - a.shape[1])))


def _pad_rows(a, n):
    return jnp.pad(a, ((0, n - a.shape[0]), (0, 0)))


def _layer(x2, batch, seq, g_mix, w_in, sgu_ln_w, sgu_ln_b, sgu_w, sgu_b, w_proj_a, shift_b,
           w_lora_w, w0, a_lora_w, a0, g_lora_w, k_k, k_a, r_k, ln_x_w, ln_x_b, w_proj_b,
           w_out, g_ffn, w_ffn1, w_ffn2, g_out):
    d = x2.shape[1]
    lora_w, lora_a, lora_g = w_lora_w.shape[0], a_lora_w.shape[0], g_lora_w.shape[0]
    c_sgu = 2 * d
    c_rkv = 3 * d
    c_lora = lora_w + lora_a + lora_g
    o_lora = c_sgu + c_rkv
    o_gate = o_lora + c_lora
    pw, pa = LANES, LANES
    pg = -(-lora_g // LANES) * LANES

    n_lora = pw + pa + pg
    w16 = w_in.astype(BF16)
    w_all = jnp.concatenate([
        w16[:, :o_lora], w16[:, o_gate:],
        _pad_cols(w16[:, o_lora:o_lora + lora_w], pw),
        _pad_cols(w16[:, o_lora + lora_w:o_lora + lora_w + lora_a], pa),
        _pad_cols(w16[:, o_lora + lora_w + lora_a:o_gate], pg + d - n_lora)], axis=1)
    sb_lo = shift_b[:, c_rkv:]
    sb_all = jnp.concatenate([
        jnp.zeros((2, c_sgu), F32), shift_b[:, :c_rkv], jnp.zeros((2, 2 * d), F32),
        _pad_cols(sb_lo[:, :lora_w], pw),
        _pad_cols(sb_lo[:, lora_w:lora_w + lora_a], pa),
        _pad_cols(sb_lo[:, lora_w + lora_a:], pg + d - n_lora)], axis=1)

    g_mix2 = g_mix.reshape(1, d)
    zu, vn, p_main, p_lora = _in_proj(
        x2, g_mix2, w_all, sb_all, sgu_ln_w.reshape(1, d), sgu_ln_b.reshape(1, d), seq,
        tm=512, n_mix=c_rkv // d, n_lora=n_lora)

    bias_full = jnp.repeat(sgu_b.T, d // SGU_GROUPS, axis=1)
    ya = _sgu(zu, vn, p_main, sgu_w, bias_full, w_proj_a.astype(BF16), ts=512, d=d)

    gw = SCAN_GROUP_HEADS * RWKV_HEAD
    hb = (lax.broadcasted_iota(jnp.int32, (gw, gw), 0) // RWKV_HEAD
          == lax.broadcasted_iota(jnp.int32, (gw, gw), 1) // RWKV_HEAD).astype(BF16)
    vec_prep = jnp.stack([w0, a0, k_k, k_a])
    lo16 = lambda w, rows: _pad_rows(w, rows).astype(BF16)
    k, lw, kk, aa, g = _rwkv_prep(
        p_main, p_lora, lo16(w_lora_w, pw), lo16(a_lora_w, pa), lo16(g_lora_w, pg),
        vec_prep, hb, d, tt=512)

    o = _rwkv_scan(p_main, k, lw, kk, aa, batch, seq, d, bb=4, tblk=256)

    vec_post = jnp.stack([ln_x_w, ln_x_b, r_k, g_ffn])
    h1, f = _rwkv_post(o, k, g, p_main, ya, x2, vec_post, hb, w_proj_b.astype(BF16),
                       w_out.astype(BF16), tm=512, d=d)
    return _ffn(h1, f, w_ffn1.astype(BF16), w_ffn2.astype(BF16), g_out.reshape(1, d),
                tm=1024, tf=1024)


def kernel(x, g_mix, w_in, sgu_ln_w, sgu_ln_b, sgu_w, sgu_b, w_proj_a, shift_b, w_lora_w, w0,
           a_lora_w, a0, g_lora_w, k_k, k_a, r_k, ln_x_w, ln_x_b, w_proj_b, w_out, g_ffn,
           w_ffn1, w_ffn2, g_final):
    batch, seq, d = x.shape
    depth = w_in.shape[0]
    assert depth == 1, "the final RMSNorm is fused into the single layer's ffn call"
    h = x.reshape(batch * seq, d)
    l = 0
    h = _layer(h, batch, seq, g_mix[l], w_in[l], sgu_ln_w[l], sgu_ln_b[l], sgu_w[l], sgu_b[l],
               w_proj_a[l], shift_b[l], w_lora_w[l], w0[l], a_lora_w[l], a0[l], g_lora_w[l],
               k_k[l], k_a[l], r_k[l], ln_x_w[l], ln_x_b[l], w_proj_b[l], w_out[l], g_ffn[l],
               w_ffn1[l], w_ffn2[l], g_final)
    return h.reshape(batch, seq, d)
```

```python
import functools
import math

import jax
import jax.numpy as jnp
from jax import lax
from jax.experimental import pallas as pl
from jax.experimental.pallas import tpu as pltpu

F32 = jnp.float32
BF16 = jnp.bfloat16

SGU_CHUNK = 128
SGU_GROUPS = 8
RWKV_HEAD = 64
NORM_EPS = 1e-6
LN_EPS = 1e-5
GN_EPS = 64e-5

LANES = 128
PREV_ROWS = 16
SCAN_CHUNK = 64
SCAN_GROUP_HEADS = 4
VMEM_LIMIT = 56 * 1024 * 1024
SCAN_HILO = dict(ar=False, bq=False, kq=False, v=False, bk=False, s0=False, a_ab=False,
                 a_ak=False, a_r=False, x=False, p=False, rhs=False, sa=False)


def _cparams(sem):
    return pltpu.CompilerParams(dimension_semantics=sem, vmem_limit_bytes=VMEM_LIMIT)


def _dot(a, b, dims=(((1,), (0,)), ((), ()))):
    return lax.dot_general(a, b, dims, preferred_element_type=F32)


_NT = (((1,), (1,)), ((), ()))
_TN = (((0,), (0,)), ((), ()))


def _split2(x):
    hi = x.astype(BF16)
    lo = (x - hi.astype(F32)).astype(BF16)
    return hi, lo


def _split3(x):
    hi = x.astype(BF16)
    r1 = x - hi.astype(F32)
    mid = r1.astype(BF16)
    lo = (r1 - mid.astype(F32)).astype(BF16)
    return hi, mid, lo


def _split1(x):
    return x.astype(BF16), None


def _dot_hp(a2, b2, dims=(((1,), (0,)), ((), ()))):
    ah, al = a2
    bh, bl = b2
    out = _dot(ah, bh, dims)
    if bl is not None:
        out = out + _dot(ah, bl, dims)
    if al is not None:
        out = out + _dot(al, bh, dims)
    return out


def _dot_x2(x, w_exact):
    hi, lo = _split2(x)
    return _dot(hi, w_exact) + _dot(lo, w_exact)


def _head_sum(x, hb_ref, hilo=True):
    w = hb_ref.shape[0]
    hb = hb_ref[...]
    one = _dot_x2 if hilo else (lambda xs, m: _dot(xs.astype(BF16), m))
    return jnp.concatenate(
        [one(x[:, c * w:(c + 1) * w], hb) for c in range(x.shape[1] // w)], axis=1)


def _rms_rows(x, g):
    ms = jnp.mean(x * x, axis=-1, keepdims=True)
    return (x * lax.rsqrt(ms + NORM_EPS)) * g


def _gelu(x):
    return 0.5 * x * (1.0 + lax.erf(x * (1.0 / math.sqrt(2.0))))


def _shift_mix_tile(a_scr, ap_scr, w_ref, sb_ref, first, cols):
    w = w_ref[:, cols]
    p = _dot(a_scr[...], w)
    pp = _dot(ap_scr[...], w)
    prev_row = jnp.where(first, 0.0, pp[PREV_ROWS - 1:, :])
    row = lax.broadcasted_iota(jnp.int32, p.shape, 0)
    shifted = jnp.where(row == 0, prev_row, pltpu.roll(p, 1, 0))
    return p * sb_ref[0:1, cols] + shifted * sb_ref[1:2, cols]


def _in_proj_kernel(x_ref, xp_ref, g_ref, w_ref, sb_ref, lnw_ref, lnb_ref,
                    zu_ref, vn_ref, p_ref, lo_ref, a_scr, ap_scr, *, tiles_per_seq, n_mix):
    d = x_ref.shape[1]
    n_main = p_ref.shape[1] // d
    first = (pl.program_id(0) % tiles_per_seq) == 0
    tile = lambda t: slice(t * d, (t + 1) * d)

    a_scr[...] = _rms_rows(x_ref[...], g_ref[...]).astype(BF16)
    ap_scr[...] = _rms_rows(xp_ref[...], g_ref[...]).astype(BF16)

    zu_ref[...] = _gelu(_dot(a_scr[...], w_ref[:, tile(0)])).astype(BF16)

    zv = _gelu(_dot(a_scr[...], w_ref[:, tile(1)]))
    mu = jnp.mean(zv, axis=-1, keepdims=True)
    zc = zv - mu
    var = jnp.mean(zc * zc, axis=-1, keepdims=True)
    vn_ref[...] = ((zc * lax.rsqrt(var + LN_EPS)) * lnw_ref[...] + lnb_ref[...]).astype(BF16)

    for t in range(n_main):
        if t < n_mix:
            p = _shift_mix_tile(a_scr, ap_scr, w_ref, sb_ref, first, tile(2 + t))
        else:
            p = _dot(a_scr[...], w_ref[:, tile(2 + t)])
        p_ref[:, tile(t)] = p.astype(p_ref.dtype)

    lo0 = (2 + n_main) * d
    lo_ref[...] = _shift_mix_tile(a_scr, ap_scr, w_ref, sb_ref, first,
                                  slice(lo0, lo0 + lo_ref.shape[1]))


def _in_proj(x2, g, w16, sb, ln_w, ln_b, seq, tm, n_mix, n_lora):
    m, d = x2.shape
    n = w16.shape[1]
    n_main = n // d - 3
    assert seq % tm == 0 and n % d == 0
    const = lambda shape: pl.BlockSpec(shape, lambda i: (0, 0), pipeline_mode=pl.Buffered(1))
    rows = lambda w: pl.BlockSpec((tm, w), lambda i: (i, 0))
    return pl.pallas_call(
        functools.partial(_in_proj_kernel, tiles_per_seq=seq // tm, n_mix=n_mix),
        grid=(m // tm,),
        in_specs=[
            rows(d),
            pl.BlockSpec((PREV_ROWS, d), lambda i: (jnp.maximum(i * (tm // PREV_ROWS) - 1, 0), 0)),
            const((1, d)), const((d, n)), const((2, n)), const((1, d)), const((1, d)),
        ],
        out_specs=[rows(d), rows(d), rows(n_main * d), rows(n_lora)],
        out_shape=[jax.ShapeDtypeStruct((m, d), BF16), jax.ShapeDtypeStruct((m, d), BF16),
                   jax.ShapeDtypeStruct((m, n_main * d), BF16),
                   jax.ShapeDtypeStruct((m, n_lora), F32)],
        scratch_shapes=[pltpu.VMEM((tm, d), BF16), pltpu.VMEM((PREV_ROWS, d), BF16)],
        compiler_params=_cparams(("parallel",)),
        name="in_proj",
    )(x2, x2, g, w16, sb, ln_w, ln_b)


def _sgu_kernel(zu_ref, vn_ref, ga_ref, ws_ref, bias_ref, wpa_ref, o_ref, wm_scr, s_scr):
    ch = SGU_CHUNK
    dg = LANES

    @pl.when(pl.program_id(0) == 0)
    def _():
        row = lax.broadcasted_iota(jnp.int32, (ch, ch), 0)
        col = lax.broadcasted_iota(jnp.int32, (ch, ch), 1)
        for g in range(SGU_GROUPS):
            wm_scr[g] = jnp.where(col <= row, ws_ref[g], 0.0).astype(BF16)

    for c in range(zu_ref.shape[0] // ch):
        rows = pl.ds(c * ch, ch)
        for g in range(SGU_GROUPS):
            cols = slice(g * dg, (g + 1) * dg)
            sv = _dot(wm_scr[g], vn_ref[rows, cols]) + bias_ref[:, cols]
            s_scr[rows, cols] = (zu_ref[rows, cols] * sv).astype(BF16)

    ya = _dot(s_scr[...], wpa_ref[...])
    o_ref[...] = (jax.nn.sigmoid(ga_ref[...].astype(F32)) * ya).astype(o_ref.dtype)


def _sgu(zu, vn, p_main, sgu_w, bias_full, wpa16, ts, d):
    m = zu.shape[0]
    ch = SGU_CHUNK
    tile = pl.BlockSpec((ts, d), lambda i: (i, 0))
    return pl.pallas_call(
        _sgu_kernel,
        grid=(m // ts,),
        in_specs=[
            tile, tile,
            pl.BlockSpec((ts, d), lambda i: (i, 3)),
            pl.BlockSpec((SGU_GROUPS, ch, ch), lambda i: (0, 0, 0)),
            pl.BlockSpec((ch, d), lambda i: (0, 0)),
            pl.BlockSpec((d, d), lambda i: (0, 0)),
        ],
        out_specs=tile,
        out_shape=jax.ShapeDtypeStruct((m, d), BF16),
        scratch_shapes=[pltpu.VMEM((SGU_GROUPS, ch, ch), BF16), pltpu.VMEM((ts, d), BF16)],
        compiler_params=_cparams(("arbitrary",)),
        name="sgu",
    )(zu, vn, p_main, sgu_w, bias_full, wpa16)


def _rwkv_prep_kernel(qk_ref, ql_ref, ww_ref, wa_ref, wg_ref, vec_ref, hb_ref,
                      k_out, lw_out, kk_out, aa_out, g_out):
    k = qk_ref[...].astype(F32)
    lo = ql_ref[...]
    xw = lo[:, 0:LANES]
    xa = lo[:, LANES:2 * LANES]
    xg = lo[:, 2 * LANES:]
    w0 = vec_ref[0:1, :]
    a0 = vec_ref[1:2, :]
    k_k = vec_ref[2:3, :]
    k_a = vec_ref[3:4, :]

    zw = w0 + _dot(jnp.tanh(xw).astype(BF16), ww_ref[...])
    lw_out[...] = (-math.exp(-0.5)) * jax.nn.sigmoid(zw)
    aa = jax.nn.sigmoid(a0 + _dot(xa.astype(BF16), wa_ref[...]))
    g_out[...] = _dot(jax.nn.sigmoid(xg).astype(BF16), wg_ref[...]).astype(g_out.dtype)
    kraw = k * k_k
    ss = _head_sum(kraw * kraw, hb_ref)
    kk_out[...] = (kraw * lax.rsqrt(jnp.maximum(ss, 1e-24))).astype(kk_out.dtype)
    k_out[...] = (k * (1.0 + (aa - 1.0) * k_a)).astype(k_out.dtype)
    aa_out[...] = aa.astype(aa_out.dtype)


def _rwkv_prep(p_main, p_lora, ww, wa, wg, vecs, hb, d, tt):
    m = p_main.shape[0]
    lw = p_lora.shape[1]
    full = lambda a: pl.BlockSpec(a.shape, lambda i: (0,) * a.ndim)
    out_spec = pl.BlockSpec((tt, d), lambda i: (i, 0))
    return pl.pallas_call(
        _rwkv_prep_kernel,
        grid=(m // tt,),
        in_specs=[
            pl.BlockSpec((tt, d), lambda i: (i, 1)),
            pl.BlockSpec((tt, lw), lambda i: (i, 0)),
            full(ww), full(wa), full(wg), full(vecs), full(hb),
        ],
        out_specs=[out_spec] * 5,
        out_shape=[jax.ShapeDtypeStruct((m, d), dt) for dt in (BF16, F32, BF16, BF16, BF16)],
        compiler_params=_cparams(("parallel",)),
        name="rwkv_prep",
    )(p_main, p_lora, ww, wa, wg, vecs, hb)


def _scan_kernel(r_ref, k_ref, v_ref, lw_ref, kk_ref, aa_ref, o_ref, s_scr, *, chunk, hd, gw):
    c_len = chunk
    assert c_len == hd
    bb, tblk, d = r_ref.shape
    n_groups = d // gw
    hpg = gw // hd
    n_double = int(math.log2(c_len)) - 1

    rowi = lax.broadcasted_iota(jnp.int32, (c_len, gw), 0)
    sub = lax.broadcasted_iota(jnp.int32, (c_len, gw), 1) % c_len
    strict = sub < rowi
    incl = sub <= rowi
    eye = jnp.where(sub == rowi, 1.0, 0.0).astype(F32)
    bd_mask = (lax.broadcasted_iota(jnp.int32, (gw, gw), 0) // c_len
               == lax.broadcasted_iota(jnp.int32, (gw, gw), 1) // hd)
    ltri = jnp.where(lax.broadcasted_iota(jnp.int32, (c_len, c_len), 1)
                     <= lax.broadcasted_iota(jnp.int32, (c_len, c_len), 0), 1.0, 0.0).astype(BF16)

    @pl.when(pl.program_id(1) == 0)
    def _():
        s_scr[...] = jnp.zeros_like(s_scr)

    def bd(x16):
        t = jnp.concatenate([x16] * hpg, axis=0)
        return jnp.where(bd_mask, t, jnp.zeros_like(t))

    chains = [(b, g) for b in range(bb) for g in range(n_groups)]
    n = range(len(chains))

    def body(c, carry):
        rows = pl.ds(pl.multiple_of(c * c_len, c_len), c_len)

        def at(ref, i):
            b, g = chains[i]
            return ref.at[b, rows, g * gw:(g + 1) * gw]

        ar, bq, kq, v16, bk, g_tot = [], [], [], [], [], []
        for i in n:
            k = at(k_ref, i)[...].astype(F32)
            lw = at(lw_ref, i)[...]
            kk = at(kk_ref, i)[...].astype(F32)
            b = kk * at(aa_ref, i)[...].astype(F32)
            lh, lm, ll = _split3(lw)
            cum = _dot(ltri, lh) + (_dot(ltri, lm) + _dot(ltri, ll))
            cum_last = cum[c_len - 1:c_len, :]
            g_inv = jnp.exp(-cum)
            g_end = jnp.exp(cum_last - cum)
            aq = -(kk * jnp.exp(cum - lw))
            rq = at(r_ref, i)[...].astype(F32) * jnp.exp(cum)
            ar.append(jnp.concatenate([aq, rq], axis=0).astype(BF16))
            bq.append((b * g_inv).astype(BF16))
            kq.append((k * g_inv).astype(BF16))
            v16.append(at(v_ref, i)[...].astype(BF16))
            bk.append(jnp.concatenate([b * g_end, k * g_end], axis=0).astype(BF16))
            g_tot.append(jnp.exp(cum_last))

        pb = [_dot(ar[i], bd(bq[i]), _NT) for i in n]
        pk = [_dot(ar[i], bd(kq[i]), _NT) for i in n]
        s0 = [s_scr[chains[i]] for i in n]
        ar_s = [_dot(ar[i], s0[i].astype(BF16), _NT) for i in n]
        a_ab = [jnp.where(strict, pb[i][:c_len], 0.0) for i in n]
        a_rb = [jnp.where(incl, pb[i][c_len:], 0.0).astype(BF16) for i in n]
        akv = [_dot(jnp.concatenate([jnp.where(strict, pk[i][:c_len], 0.0),
                                     jnp.where(incl, pk[i][c_len:], 0.0)], axis=0).astype(BF16),
                    bd(v16[i])) for i in n]
        rhs = [ar_s[i][:c_len] + akv[i][:c_len] for i in n]

        p = [_dot(a_ab[i].astype(BF16), bd(a_ab[i].astype(BF16))) for i in n]
        x = [eye + a_ab[i] for i in n]
        for _ in range(n_double - 1):
            xp = [_dot(jnp.concatenate([x[i], p[i]], axis=0).astype(BF16), bd(p[i].astype(BF16)))
                  for i in n]
            x = [x[i] + xp[i][:c_len] for i in n]
            p = [xp[i][c_len:] for i in n]
        x = [x[i] + _dot(x[i].astype(BF16), bd(p[i].astype(BF16))) for i in n]

        sa16 = [_dot(x[i].astype(BF16), bd(rhs[i].astype(BF16))).astype(BF16) for i in n]
        for i in n:
            at(o_ref, i)[...] = ((ar_s[i][c_len:] + akv[i][c_len:])
                                 + _dot(a_rb[i], bd(sa16[i]))).astype(o_ref.dtype)
        for i in n:
            upd = _dot(jnp.concatenate([sa16[i], v16[i]], axis=0), bk[i], _TN)
            s_scr[chains[i]] = jnp.where(bd_mask, s0[i] * g_tot[i] + upd, 0.0)
        return carry

    lax.fori_loop(0, tblk // c_len, body, 0)


def _rwkv_scan(p_main, k, lw, kk, aa, batch, seq, d, bb, tblk):
    gw = SCAN_GROUP_HEADS * RWKV_HEAD
    spec = pl.BlockSpec((bb, tblk, d), lambda b, t: (b, t, 0))
    col = lambda cb: pl.BlockSpec((bb, tblk, d), lambda b, t: (b, t, cb))
    as3 = lambda a: a.reshape(batch, seq, a.shape[-1])
    out = pl.pallas_call(
        functools.partial(_scan_kernel, chunk=SCAN_CHUNK, hd=RWKV_HEAD, gw=gw),
        grid=(batch // bb, seq // tblk),
        in_specs=[col(0), spec, col(2), spec, spec, spec],
        out_specs=spec,
        out_shape=jax.ShapeDtypeStruct((batch, seq, d), BF16),
        scratch_shapes=[pltpu.VMEM((bb, d // gw, gw, gw), F32)],
        compiler_params=_cparams(("parallel", "arbitrary")),
        name="rwkv_scan",
    )(as3(p_main), as3(k), as3(p_main), as3(lw), as3(kk), as3(aa))
    return out.reshape(batch * seq, d)


def _rwkv_post_kernel(o_ref, r_ref, k_ref, v_ref, g_ref, gb_ref, ya_ref, x_ref,
                      vec_ref, hb_ref, wpb_ref, wout_ref, h_ref, f_ref, *, hd):
    lnx_w = vec_ref[0:1, :]
    lnx_b = vec_ref[1:2, :]
    r_k = vec_ref[2:3, :]
    g_ffn = vec_ref[3:4, :]
    o = o_ref[...].astype(F32)
    inv_n = 1.0 / hd
    mu = _head_sum(o, hb_ref, hilo=False) * inv_n
    oc = o - mu
    var = _head_sum(oc * oc, hb_ref, hilo=False) * inv_n
    on = (oc * lax.rsqrt(var + GN_EPS)) * lnx_w + lnx_b
    v = v_ref[...].astype(F32)
    rk = r_ref[...].astype(F32) * k_ref[...].astype(F32)
    bonus = _head_sum(rk * r_k, hb_ref, hilo=False) * v
    yb = _dot(((on + bonus) * g_ref[...]).astype(BF16), wpb_ref[...])
    mixed = ya_ref[...].astype(F32) + jax.nn.sigmoid(gb_ref[...].astype(F32)) * yb
    h = x_ref[...] + _dot(mixed.astype(BF16), wout_ref[...])
    h_ref[...] = h
    f_ref[...] = _rms_rows(h, g_ffn).astype(BF16)


def _rwkv_post(o, k, g, p_main, ya, x2, vecs, hb, wpb16, wout16, tm, d):
    m = x2.shape[0]
    tile = pl.BlockSpec((tm, d), lambda i: (i, 0))
    col = lambda cb: pl.BlockSpec((tm, d), lambda i: (i, cb))
    full = lambda a: pl.BlockSpec(a.shape, lambda i: (0,) * a.ndim)
    return pl.pallas_call(
        functools.partial(_rwkv_post_kernel, hd=RWKV_HEAD),
        grid=(m // tm,),
        in_specs=[tile, col(0), tile, col(2), tile, col(4),
                  tile, tile, full(vecs), full(hb), full(wpb16), full(wout16)],
        out_specs=[tile, tile],
        out_shape=[jax.ShapeDtypeStruct((m, d), F32), jax.ShapeDtypeStruct((m, d), BF16)],
        compiler_params=_cparams(("parallel",)),
        name="rwkv_post",
    )(o, p_main, k, p_main, g, p_main, ya, x2, vecs, hb, wpb16, wout16)


def _ffn_kernel(h_ref, f_ref, w1_ref, w2_ref, gf_ref, o_ref, *, tf):
    acc = h_ref[...]
    for j in range(w1_ref.shape[1] // tf):
        cols = slice(j * tf, (j + 1) * tf)
        t = jnp.maximum(_dot(f_ref[...], w1_ref[:, cols]), 0.0)
        acc = acc + _dot((t * t).astype(BF16), w2_ref[cols, :])
    o_ref[...] = _rms_rows(acc, gf_ref[...])


def _ffn(h, f, w1_16, w2_16, g_final, tm, tf):
    m, d = h.shape
    dff = w1_16.shape[1]
    assert dff % tf == 0
    tile = pl.BlockSpec((tm, d), lambda i: (i, 0))
    const = lambda shape: pl.BlockSpec(shape, lambda i: (0, 0), pipeline_mode=pl.Buffered(1))
    return pl.pallas_call(
        functools.partial(_ffn_kernel, tf=tf),
        grid=(m // tm,),
        in_specs=[tile, tile, const((d, dff)), const((dff, d)), const((1, d))],
        out_specs=tile,
        out_shape=jax.ShapeDtypeStruct((m, d), F32),
        compiler_params=_cparams(("parallel",)),
        name="ffn",
    )(h, f, w1_16, w2_16, g_final)


def _pad_cols(a, n):
    return jnp.pad(a, ((0, 0), (0, n - a.shape[1])))


def _pad_rows(a, n):
    return jnp.pad(a, ((0, n - a.shape[0]), (0, 0)))


def _layer(x2, batch, seq, g_mix, w_in, sgu_ln_w, sgu_ln_b, sgu_w, sgu_b, w_proj_a, shift_b,
           w_lora_w, w0, a_lora_w, a0, g_lora_w, k_k, k_a, r_k, ln_x_w, ln_x_b, w_proj_b,
           w_out, g_ffn, w_ffn1, w_ffn2, g_out):
    d = x2.shape[1]
    lora_w, lora_a, lora_g = w_lora_w.shape[0], a_lora_w.shape[0], g_lora_w.shape[0]
    c_sgu = 2 * d
    c_rkv = 3 * d
    c_lora = lora_w + lora_a + lora_g
    o_lora = c_sgu + c_rkv
    o_gate = o_lora + c_lora
    pw, pa = LANES, LANES
    pg = -(-lora_g // LANES) * LANES

    n_lora = pw + pa + pg
    w16 = w_in.astype(BF16)
    w_all = jnp.concatenate([
        w16[:, :o_lora], w16[:, o_gate:],
        _pad_cols(w16[:, o_lora:o_lora + lora_w], pw),
        _pad_cols(w16[:, o_lora + lora_w:o_lora + lora_w + lora_a], pa),
        _pad_cols(w16[:, o_lora + lora_w + lora_a:o_gate], pg + d - n_lora)], axis=1)
    sb_lo = shift_b[:, c_rkv:]
    sb_all = jnp.concatenate([
        jnp.zeros((2, c_sgu), F32), shift_b[:, :c_rkv], jnp.zeros((2, 2 * d), F32),
        _pad_cols(sb_lo[:, :lora_w], pw),
        _pad_cols(sb_lo[:, lora_w:lora_w + lora_a], pa),
        _pad_cols(sb_lo[:, lora_w + lora_a:], pg + d - n_lora)], axis=1)

    g_mix2 = g_mix.reshape(1, d)
    zu, vn, p_main, p_lora = _in_proj(
        x2, g_mix2, w_all, sb_all, sgu_ln_w.reshape(1, d), sgu_ln_b.reshape(1, d), seq,
        tm=512, n_mix=c_rkv // d, n_lora=n_lora)

    bias_full = jnp.repeat(sgu_b.T, d // SGU_GROUPS, axis=1)
    ya = _sgu(zu, vn, p_main, sgu_w, bias_full, w_proj_a.astype(BF16), ts=512, d=d)

    gw = SCAN_GROUP_HEADS * RWKV_HEAD
    hb = (lax.broadcasted_iota(jnp.int32, (gw, gw), 0) // RWKV_HEAD
          == lax.broadcasted_iota(jnp.int32, (gw, gw), 1) // RWKV_HEAD).astype(BF16)
    vec_prep = jnp.stack([w0, a0, k_k, k_a])
    lo16 = lambda w, rows: _pad_rows(w, rows).astype(BF16)
    k, lw, kk, aa, g = _rwkv_prep(
        p_main, p_lora, lo16(w_lora_w, pw), lo16(a_lora_w, pa), lo16(g_lora_w, pg),
        vec_prep, hb, d, tt=512)

    o = _rwkv_scan(p_main, k, lw, kk, aa, batch, seq, d, bb=4, tblk=256)

    vec_post = jnp.stack([ln_x_w, ln_x_b, r_k, g_ffn])
    h1, f = _rwkv_post(o, k, g, p_main, ya, x2, vec_post, hb, w_proj_b.astype(BF16),
                       w_out.astype(BF16), tm=512, d=d)
    return _ffn(h1, f, w_ffn1.astype(BF16), w_ffn2.astype(BF16), g_out.reshape(1, d),
                tm=512, tf=1024)


def kernel(x, g_mix, w_in, sgu_ln_w, sgu_ln_b, sgu_w, sgu_b, w_proj_a, shift_b, w_lora_w, w0,
           a_lora_w, a0, g_lora_w, k_k, k_a, r_k, ln_x_w, ln_x_b, w_proj_b, w_out, g_ffn,
           w_ffn1, w_ffn2, g_final):
    batch, seq, d = x.shape
    depth = w_in.shape[0]
    assert depth == 1, "the final RMSNorm is fused into the single layer's ffn call"
    h = x.reshape(batch * seq, d)
    l = 0
    h = _layer(h, batch, seq, g_mix[l], w_in[l], sgu_ln_w[l], sgu_ln_b[l], sgu_w[l], sgu_b[l],
               w_proj_a[l], shift_b[l], w_lora_w[l], w0[l], a_lora_w[l], a0[l], g_lora_w[l],
               k_k[l], k_a[l], r_k[l], ln_x_w[l], ln_x_b[l], w_proj_b[l], w_out[l], g_ffn[l],
               w_ffn1[l], w_ffn2[l], g_final)
    return h.reshape(batch, seq, d)
```

```python
import functools
import math

import jax
import jax.numpy as jnp
from jax import lax
from jax.experimental import pallas as pl
from jax.experimental.pallas import tpu as pltpu

F32 = jnp.float32
BF16 = jnp.bfloat16

SGU_CHUNK = 128
SGU_GROUPS = 8
RWKV_HEAD = 64
NORM_EPS = 1e-6
LN_EPS = 1e-5
GN_EPS = 64e-5

LANES = 128
PREV_ROWS = 16
SCAN_CHUNK = 64
SCAN_GROUP_HEADS = 4
VMEM_LIMIT = 56 * 1024 * 1024
SCAN_HILO = dict(ar=False, bq=False, kq=False, v=False, bk=False, s0=False, a_ab=False,
                 a_ak=False, a_r=False, x=False, p=False, rhs=False, sa=False)


def _cparams(sem):
    return pltpu.CompilerParams(dimension_semantics=sem, vmem_limit_bytes=VMEM_LIMIT)


def _dot(a, b, dims=(((1,), (0,)), ((), ()))):
    return lax.dot_general(a, b, dims, preferred_element_type=F32)


_NT = (((1,), (1,)), ((), ()))
_TN = (((0,), (0,)), ((), ()))


def _split2(x):
    hi = x.astype(BF16)
    lo = (x - hi.astype(F32)).astype(BF16)
    return hi, lo


def _split3(x):
    hi = x.astype(BF16)
    r1 = x - hi.astype(F32)
    mid = r1.astype(BF16)
    lo = (r1 - mid.astype(F32)).astype(BF16)
    return hi, mid, lo


def _split1(x):
    return x.astype(BF16), None


def _dot_hp(a2, b2, dims=(((1,), (0,)), ((), ()))):
    ah, al = a2
    bh, bl = b2
    out = _dot(ah, bh, dims)
    if bl is not None:
        out = out + _dot(ah, bl, dims)
    if al is not None:
        out = out + _dot(al, bh, dims)
    return out


def _dot_x2(x, w_exact):
    hi, lo = _split2(x)
    return _dot(hi, w_exact) + _dot(lo, w_exact)


def _head_sum(x, hb_ref, hilo=True):
    w = hb_ref.shape[0]
    hb = hb_ref[...]
    one = _dot_x2 if hilo else (lambda xs, m: _dot(xs.astype(BF16), m))
    return jnp.concatenate(
        [one(x[:, c * w:(c + 1) * w], hb) for c in range(x.shape[1] // w)], axis=1)


def _rms_rows(x, g):
    ms = jnp.mean(x * x, axis=-1, keepdims=True)
    return (x * lax.rsqrt(ms + NORM_EPS)) * g


def _gelu(x):
    return 0.5 * x * (1.0 + lax.erf(x * (1.0 / math.sqrt(2.0))))


def _shift_mix_tile(a_scr, ap_scr, w_ref, sb_ref, first, cols):
    w = w_ref[:, cols]
    p = _dot(a_scr[...], w)
    pp = _dot(ap_scr[...], w)
    prev_row = jnp.where(first, 0.0, pp[PREV_ROWS - 1:, :])
    row = lax.broadcasted_iota(jnp.int32, p.shape, 0)
    shifted = jnp.where(row == 0, prev_row, pltpu.roll(p, 1, 0))
    return p * sb_ref[0:1, cols] + shifted * sb_ref[1:2, cols]


def _in_proj_kernel(x_ref, xp_ref, g_ref, w_ref, sb_ref, lnw_ref, lnb_ref,
                    zu_ref, vn_ref, p_ref, lo_ref, a_scr, ap_scr, *, tiles_per_seq, n_mix):
    d = x_ref.shape[1]
    n_main = p_ref.shape[1] // d
    first = (pl.program_id(0) % tiles_per_seq) == 0
    tile = lambda t: slice(t * d, (t + 1) * d)

    a_scr[...] = _rms_rows(x_ref[...], g_ref[...]).astype(BF16)
    ap_scr[...] = _rms_rows(xp_ref[...], g_ref[...]).astype(BF16)

    zu_ref[...] = _gelu(_dot(a_scr[...], w_ref[:, tile(0)])).astype(BF16)

    zv = _gelu(_dot(a_scr[...], w_ref[:, tile(1)]))
    mu = jnp.mean(zv, axis=-1, keepdims=True)
    zc = zv - mu
    var = jnp.mean(zc * zc, axis=-1, keepdims=True)
    vn_ref[...] = ((zc * lax.rsqrt(var + LN_EPS)) * lnw_ref[...] + lnb_ref[...]).astype(BF16)

    for t in range(n_main):
        if t < n_mix:
            p = _shift_mix_tile(a_scr, ap_scr, w_ref, sb_ref, first, tile(2 + t))
        else:
            p = _dot(a_scr[...], w_ref[:, tile(2 + t)])
        p_ref[:, tile(t)] = p.astype(p_ref.dtype)

    lo0 = (2 + n_main) * d
    lo_ref[...] = _shift_mix_tile(a_scr, ap_scr, w_ref, sb_ref, first,
                                  slice(lo0, lo0 + lo_ref.shape[1]))


def _in_proj(x2, g, w16, sb, ln_w, ln_b, seq, tm, n_mix, n_lora):
    m, d = x2.shape
    n = w16.shape[1]
    n_main = n // d - 3
    assert seq % tm == 0 and n % d == 0
    const = lambda shape: pl.BlockSpec(shape, lambda i: (0, 0), pipeline_mode=pl.Buffered(1))
    rows = lambda w: pl.BlockSpec((tm, w), lambda i: (i, 0))
    return pl.pallas_call(
        functools.partial(_in_proj_kernel, tiles_per_seq=seq // tm, n_mix=n_mix),
        grid=(m // tm,),
        in_specs=[
            rows(d),
            pl.BlockSpec((PREV_ROWS, d), lambda i: (jnp.maximum(i * (tm // PREV_ROWS) - 1, 0), 0)),
            const((1, d)), const((d, n)), const((2, n)), const((1, d)), const((1, d)),
        ],
        out_specs=[rows(d), rows(d), rows(n_main * d), rows(n_lora)],
        out_shape=[jax.ShapeDtypeStruct((m, d), BF16), jax.ShapeDtypeStruct((m, d), BF16),
                   jax.ShapeDtypeStruct((m, n_main * d), BF16),
                   jax.ShapeDtypeStruct((m, n_lora), F32)],
        scratch_shapes=[pltpu.VMEM((tm, d), BF16), pltpu.VMEM((PREV_ROWS, d), BF16)],
        compiler_params=_cparams(("parallel",)),
        name="in_proj",
    )(x2, x2, g, w16, sb, ln_w, ln_b)


def _sgu_tile(zu_ref, vn_ref, ga_ref, bias_ref, wpa_ref, wm_scr, s_scr):
    ch = SGU_CHUNK
    dg = LANES
    for c in range(zu_ref.shape[0] // ch):
        rows = pl.ds(c * ch, ch)
        for g in range(SGU_GROUPS):
            cols = slice(g * dg, (g + 1) * dg)
            sv = _dot(wm_scr[g], vn_ref[rows, cols]) + bias_ref[:, cols]
            s_scr[rows, cols] = (zu_ref[rows, cols] * sv).astype(BF16)
    return jax.nn.sigmoid(ga_ref[...].astype(F32)) * _dot(s_scr[...], wpa_ref[...])


def _rwkv_prep_kernel(qk_ref, ql_ref, ww_ref, wa_ref, wg_ref, vec_ref, hb_ref,
                      k_out, lw_out, kk_out, aa_out, g_out):
    k = qk_ref[...].astype(F32)
    lo = ql_ref[...]
    xw = lo[:, 0:LANES]
    xa = lo[:, LANES:2 * LANES]
    xg = lo[:, 2 * LANES:]
    w0 = vec_ref[0:1, :]
    a0 = vec_ref[1:2, :]
    k_k = vec_ref[2:3, :]
    k_a = vec_ref[3:4, :]

    zw = w0 + _dot(jnp.tanh(xw).astype(BF16), ww_ref[...])
    lw_out[...] = (-math.exp(-0.5)) * jax.nn.sigmoid(zw)
    aa = jax.nn.sigmoid(a0 + _dot(xa.astype(BF16), wa_ref[...]))
    g_out[...] = _dot(jax.nn.sigmoid(xg).astype(BF16), wg_ref[...]).astype(g_out.dtype)
    kraw = k * k_k
    ss = _head_sum(kraw * kraw, hb_ref)
    kk_out[...] = (kraw * lax.rsqrt(jnp.maximum(ss, 1e-24))).astype(kk_out.dtype)
    k_out[...] = (k * (1.0 + (aa - 1.0) * k_a)).astype(k_out.dtype)
    aa_out[...] = aa.astype(aa_out.dtype)


def _rwkv_prep(p_main, p_lora, ww, wa, wg, vecs, hb, d, tt):
    m = p_main.shape[0]
    lw = p_lora.shape[1]
    full = lambda a: pl.BlockSpec(a.shape, lambda i: (0,) * a.ndim)
    out_spec = pl.BlockSpec((tt, d), lambda i: (i, 0))
    return pl.pallas_call(
        _rwkv_prep_kernel,
        grid=(m // tt,),
        in_specs=[
            pl.BlockSpec((tt, d), lambda i: (i, 1)),
            pl.BlockSpec((tt, lw), lambda i: (i, 0)),
            full(ww), full(wa), full(wg), full(vecs), full(hb),
        ],
        out_specs=[out_spec] * 5,
        out_shape=[jax.ShapeDtypeStruct((m, d), dt) for dt in (BF16, F32, BF16, BF16, BF16)],
        compiler_params=_cparams(("parallel",)),
        name="rwkv_prep",
    )(p_main, p_lora, ww, wa, wg, vecs, hb)


def _scan_kernel(r_ref, k_ref, v_ref, lw_ref, kk_ref, aa_ref, o_ref, s_scr, *, chunk, hd, gw):
    c_len = chunk
    assert c_len == hd
    bb, tblk, d = r_ref.shape
    n_groups = d // gw
    hpg = gw // hd
    n_double = int(math.log2(c_len)) - 1

    rowi = lax.broadcasted_iota(jnp.int32, (c_len, gw), 0)
    sub = lax.broadcasted_iota(jnp.int32, (c_len, gw), 1) % c_len
    strict = sub < rowi
    incl = sub <= rowi
    eye = jnp.where(sub == rowi, 1.0, 0.0).astype(F32)
    bd_mask = (lax.broadcasted_iota(jnp.int32, (gw, gw), 0) // c_len
               == lax.broadcasted_iota(jnp.int32, (gw, gw), 1) // hd)
    ltri = jnp.where(lax.broadcasted_iota(jnp.int32, (c_len, c_len), 1)
                     <= lax.broadcasted_iota(jnp.int32, (c_len, c_len), 0), 1.0, 0.0).astype(BF16)

    @pl.when(pl.program_id(1) == 0)
    def _():
        s_scr[...] = jnp.zeros_like(s_scr)

    def bd(x16):
        t = jnp.concatenate([x16] * hpg, axis=0)
        return jnp.where(bd_mask, t, jnp.zeros_like(t))

    chains = [(b, g) for b in range(bb) for g in range(n_groups)]
    n = range(len(chains))

    def body(c, carry):
        rows = pl.ds(pl.multiple_of(c * c_len, c_len), c_len)

        def at(ref, i):
            b, g = chains[i]
            return ref.at[b, rows, g * gw:(g + 1) * gw]

        ar, bq, kq, v16, bk, g_tot = [], [], [], [], [], []
        for i in n:
            k = at(k_ref, i)[...].astype(F32)
            lw = at(lw_ref, i)[...]
            kk = at(kk_ref, i)[...].astype(F32)
            b = kk * at(aa_ref, i)[...].astype(F32)
            lh, lm, ll = _split3(lw)
            cum = _dot(ltri, lh) + (_dot(ltri, lm) + _dot(ltri, ll))
            cum_last = cum[c_len - 1:c_len, :]
            g_inv = jnp.exp(-cum)
            g_end = jnp.exp(cum_last - cum)
            aq = -(kk * jnp.exp(cum - lw))
            rq = at(r_ref, i)[...].astype(F32) * jnp.exp(cum)
            ar.append(jnp.concatenate([aq, rq], axis=0).astype(BF16))
            bq.append((b * g_inv).astype(BF16))
            kq.append((k * g_inv).astype(BF16))
            v16.append(at(v_ref, i)[...].astype(BF16))
            bk.append(jnp.concatenate([b * g_end, k * g_end], axis=0).astype(BF16))
            g_tot.append(jnp.exp(cum_last))

        pb = [_dot(ar[i], bd(bq[i]), _NT) for i in n]
        pk = [_dot(ar[i], bd(kq[i]), _NT) for i in n]
        s0 = [s_scr[chains[i]] for i in n]
        ar_s = [_dot(ar[i], s0[i].astype(BF16), _NT) for i in n]
        a_ab = [jnp.where(strict, pb[i][:c_len], 0.0) for i in n]
        a_rb = [jnp.where(incl, pb[i][c_len:], 0.0).astype(BF16) for i in n]
        akv = [_dot(jnp.concatenate([jnp.where(strict, pk[i][:c_len], 0.0),
                                     jnp.where(incl, pk[i][c_len:], 0.0)], axis=0).astype(BF16),
                    bd(v16[i])) for i in n]
        rhs = [ar_s[i][:c_len] + akv[i][:c_len] for i in n]

        p = [_dot(a_ab[i].astype(BF16), bd(a_ab[i].astype(BF16))) for i in n]
        x = [eye + a_ab[i] for i in n]
        for _ in range(n_double - 1):
            xp = [_dot(jnp.concatenate([x[i], p[i]], axis=0).astype(BF16), bd(p[i].astype(BF16)))
                  for i in n]
            x = [x[i] + xp[i][:c_len] for i in n]
            p = [xp[i][c_len:] for i in n]
        x = [x[i] + _dot(x[i].astype(BF16), bd(p[i].astype(BF16))) for i in n]

        sa16 = [_dot(x[i].astype(BF16), bd(rhs[i].astype(BF16))).astype(BF16) for i in n]
        for i in n:
            at(o_ref, i)[...] = ((ar_s[i][c_len:] + akv[i][c_len:])
                                 + _dot(a_rb[i], bd(sa16[i]))).astype(o_ref.dtype)
        for i in n:
            upd = _dot(jnp.concatenate([sa16[i], v16[i]], axis=0), bk[i], _TN)
            s_scr[chains[i]] = jnp.where(bd_mask, s0[i] * g_tot[i] + upd, 0.0)
        return carry

    lax.fori_loop(0, tblk // c_len, body, 0)


def _rwkv_scan(p_main, k, lw, kk, aa, batch, seq, d, bb, tblk):
    gw = SCAN_GROUP_HEADS * RWKV_HEAD
    spec = pl.BlockSpec((bb, tblk, d), lambda b, t: (b, t, 0))
    col = lambda cb: pl.BlockSpec((bb, tblk, d), lambda b, t: (b, t, cb))
    as3 = lambda a: a.reshape(batch, seq, a.shape[-1])
    out = pl.pallas_call(
        functools.partial(_scan_kernel, chunk=SCAN_CHUNK, hd=RWKV_HEAD, gw=gw),
        grid=(batch // bb, seq // tblk),
        in_specs=[col(0), spec, col(2), spec, spec, spec],
        out_specs=spec,
        out_shape=jax.ShapeDtypeStruct((batch, seq, d), BF16),
        scratch_shapes=[pltpu.VMEM((bb, d // gw, gw, gw), F32)],
        compiler_params=_cparams(("parallel", "arbitrary")),
        name="rwkv_scan",
    )(as3(p_main), as3(k), as3(p_main), as3(lw), as3(kk), as3(aa))
    return out.reshape(batch * seq, d)


def _mix_kernel(zu_ref, vn_ref, ga_ref, o_ref, r_ref, k_ref, v_ref, g_ref, gb_ref, x_ref,
                ws_ref, bias_ref, vec_ref, hb_ref, wpa_ref, wpb_ref, wout_ref,
                h_ref, f_ref, wm_scr, s_scr, *, hd):
    @pl.when(pl.program_id(0) == 0)
    def _():
        ch = SGU_CHUNK
        row = lax.broadcasted_iota(jnp.int32, (ch, ch), 0)
        col = lax.broadcasted_iota(jnp.int32, (ch, ch), 1)
        for g in range(SGU_GROUPS):
            wm_scr[g] = jnp.where(col <= row, ws_ref[g], 0.0).astype(BF16)

    ya = _sgu_tile(zu_ref, vn_ref, ga_ref, bias_ref, wpa_ref, wm_scr, s_scr)

    lnx_w = vec_ref[0:1, :]
    lnx_b = vec_ref[1:2, :]
    r_k = vec_ref[2:3, :]
    g_ffn = vec_ref[3:4, :]
    o = o_ref[...].astype(F32)
    inv_n = 1.0 / hd
    mu = _head_sum(o, hb_ref, hilo=False) * inv_n
    oc = o - mu
    var = _head_sum(oc * oc, hb_ref, hilo=False) * inv_n
    on = (oc * lax.rsqrt(var + GN_EPS)) * lnx_w + lnx_b
    v = v_ref[...].astype(F32)
    rk = r_ref[...].astype(F32) * k_ref[...].astype(F32)
    bonus = _head_sum(rk * r_k, hb_ref, hilo=False) * v
    yb = _dot(((on + bonus) * g_ref[...]).astype(BF16), wpb_ref[...])
    mixed = ya + jax.nn.sigmoid(gb_ref[...].astype(F32)) * yb
    h = x_ref[...] + _dot(mixed.astype(BF16), wout_ref[...])
    h_ref[...] = h
    f_ref[...] = _rms_rows(h, g_ffn).astype(BF16)


def _mix(zu, vn, o, k, g, p_main, x2, sgu_w, bias_full, vecs, hb, wpa16, wpb16, wout16, tm, d):
    m = x2.shape[0]
    ch = SGU_CHUNK
    tile = pl.BlockSpec((tm, d), lambda i: (i, 0))
    col = lambda cb: pl.BlockSpec((tm, d), lambda i: (i, cb))
    const = lambda a: pl.BlockSpec(a.shape, lambda i: (0,) * a.ndim,
                                   pipeline_mode=pl.Buffered(1))
    return pl.pallas_call(
        functools.partial(_mix_kernel, hd=RWKV_HEAD),
        grid=(m // tm,),
        in_specs=[tile, tile, col(3), tile, col(0), tile, col(2), tile, col(4), tile,
                  const(sgu_w), const(bias_full), const(vecs), const(hb),
                  const(wpa16), const(wpb16), const(wout16)],
        out_specs=[tile, tile],
        out_shape=[jax.ShapeDtypeStruct((m, d), F32), jax.ShapeDtypeStruct((m, d), BF16)],
        scratch_shapes=[pltpu.VMEM((SGU_GROUPS, ch, ch), BF16), pltpu.VMEM((tm, d), BF16)],
        compiler_params=_cparams(("arbitrary",)),
        name="mix",
    )(zu, vn, p_main, o, p_main, k, p_main, g, p_main, x2,
      sgu_w, bias_full, vecs, hb, wpa16, wpb16, wout16)


def _ffn_kernel(h_ref, f_ref, w1_ref, w2_ref, gf_ref, o_ref, *, tf):
    acc = h_ref[...]
    for j in range(w1_ref.shape[1] // tf):
        cols = slice(j * tf, (j + 1) * tf)
        t = jnp.maximum(_dot(f_ref[...], w1_ref[:, cols]), 0.0)
        acc = acc + _dot((t * t).astype(BF16), w2_ref[cols, :])
    o_ref[...] = _rms_rows(acc, gf_ref[...])


def _ffn(h, f, w1_16, w2_16, g_final, tm, tf):
    m, d = h.shape
    dff = w1_16.shape[1]
    assert dff % tf == 0
    tile = pl.BlockSpec((tm, d), lambda i: (i, 0))
    const = lambda shape: pl.BlockSpec(shape, lambda i: (0, 0), pipeline_mode=pl.Buffered(1))
    return pl.pallas_call(
        functools.partial(_ffn_kernel, tf=tf),
        grid=(m // tm,),
        in_specs=[tile, tile, const((d, dff)), const((dff, d)), const((1, d))],
        out_specs=tile,
        out_shape=jax.ShapeDtypeStruct((m, d), F32),
        compiler_params=_cparams(("parallel",)),
        name="ffn",
    )(h, f, w1_16, w2_16, g_final)


def _pad_cols(a, n):
    return jnp.pad(a, ((0, 0), (0, n - a.shape[1])))


def _pad_rows(a, n):
    return jnp.pad(a, ((0, n - a.shape[0]), (0, 0)))


def _layer(x2, batch, seq, g_mix, w_in, sgu_ln_w, sgu_ln_b, sgu_w, sgu_b, w_proj_a, shift_b,
           w_lora_w, w0, a_lora_w, a0, g_lora_w, k_k, k_a, r_k, ln_x_w, ln_x_b, w_proj_b,
           w_out, g_ffn, w_ffn1, w_ffn2, g_out):
    d = x2.shape[1]
    lora_w, lora_a, lora_g = w_lora_w.shape[0], a_lora_w.shape[0], g_lora_w.shape[0]
    c_sgu = 2 * d
    c_rkv = 3 * d
    c_lora = lora_w + lora_a + lora_g
    o_lora = c_sgu + c_rkv
    o_gate = o_lora + c_lora
    pw, pa = LANES, LANES
    pg = -(-lora_g // LANES) * LANES

    n_lora = pw + pa + pg
    w16 = w_in.astype(BF16)
    w_all = jnp.concatenate([
        w16[:, :o_lora], w16[:, o_gate:],
        _pad_cols(w16[:, o_lora:o_lora + lora_w], pw),
        _pad_cols(w16[:, o_lora + lora_w:o_lora + lora_w + lora_a], pa),
        _pad_cols(w16[:, o_lora + lora_w + lora_a:o_gate], pg + d - n_lora)], axis=1)
    sb_lo = shift_b[:, c_rkv:]
    sb_all = jnp.concatenate([
        jnp.zeros((2, c_sgu), F32), shift_b[:, :c_rkv], jnp.zeros((2, 2 * d), F32),
        _pad_cols(sb_lo[:, :lora_w], pw),
        _pad_cols(sb_lo[:, lora_w:lora_w + lora_a], pa),
        _pad_cols(sb_lo[:, lora_w + lora_a:], pg + d - n_lora)], axis=1)

    g_mix2 = g_mix.reshape(1, d)
    zu, vn, p_main, p_lora = _in_proj(
        x2, g_mix2, w_all, sb_all, sgu_ln_w.reshape(1, d), sgu_ln_b.reshape(1, d), seq,
        tm=512, n_mix=c_rkv // d, n_lora=n_lora)

    bias_full = jnp.repeat(sgu_b.T, d // SGU_GROUPS, axis=1)

    gw = SCAN_GROUP_HEADS * RWKV_HEAD
    hb = (lax.broadcasted_iota(jnp.int32, (gw, gw), 0) // RWKV_HEAD
          == lax.broadcasted_iota(jnp.int32, (gw, gw), 1) // RWKV_HEAD).astype(BF16)
    vec_prep = jnp.stack([w0, a0, k_k, k_a])
    lo16 = lambda w, rows: _pad_rows(w, rows).astype(BF16)
    k, lw, kk, aa, g = _rwkv_prep(
        p_main, p_lora, lo16(w_lora_w, pw), lo16(a_lora_w, pa), lo16(g_lora_w, pg),
        vec_prep, hb, d, tt=512)

    o = _rwkv_scan(p_main, k, lw, kk, aa, batch, seq, d, bb=4, tblk=256)

    vec_post = jnp.stack([ln_x_w, ln_x_b, r_k, g_ffn])
    h1, f = _mix(zu, vn, o, k, g, p_main, x2, sgu_w, bias_full, vec_post, hb,
                 w_proj_a.astype(BF16), w_proj_b.astype(BF16), w_out.astype(BF16), tm=512, d=d)
    return _ffn(h1, f, w_ffn1.astype(BF16), w_ffn2.astype(BF16), g_out.reshape(1, d),
                tm=512, tf=1024)


def kernel(x, g_mix, w_in, sgu_ln_w, sgu_ln_b, sgu_w, sgu_b, w_proj_a, shift_b, w_lora_w, w0,
           a_lora_w, a0, g_lora_w, k_k, k_a, r_k, ln_x_w, ln_x_b, w_proj_b, w_out, g_ffn,
           w_ffn1, w_ffn2, g_final):
    batch, seq, d = x.shape
    depth = w_in.shape[0]
    assert depth == 1, "the final RMSNorm is fused into the single layer's ffn call"
    h = x.reshape(batch * seq, d)
    l = 0
    h = _layer(h, batch, seq, g_mix[l], w_in[l], sgu_ln_w[l], sgu_ln_b[l], sgu_w[l], sgu_b[l],
               w_proj_a[l], shift_b[l], w_lora_w[l], w0[l], a_lora_w[l], a0[l], g_lora_w[l],
               k_k[l], k_a[l], r_k[l], ln_x_w[l], ln_x_b[l], w_proj_b[l], w_out[l], g_ffn[l],
               w_ffn1[l], w_ffn2[l], g_final)
    return h.reshape(batch, seq, d)
```

```python
import functools
import math

import jax
import jax.numpy as jnp
from jax import lax
from jax.experimental import pallas as pl
from jax.experimental.pallas import tpu as pltpu

F32 = jnp.float32
BF16 = jnp.bfloat16

SGU_CHUNK = 128
SGU_GROUPS = 8
RWKV_HEAD = 64
NORM_EPS = 1e-6
LN_EPS = 1e-5
GN_EPS = 64e-5

LANES = 128
MXU_WIDTH = 256
PREV_ROWS = 16
SCAN_CHUNK = 64
SCAN_GROUP_HEADS = MXU_WIDTH // RWKV_HEAD
VMEM_LIMIT = 56 * 1024 * 1024


def _cparams(sem):
    return pltpu.CompilerParams(dimension_semantics=sem, vmem_limit_bytes=VMEM_LIMIT)


def _dot(a, b, dims=(((1,), (0,)), ((), ()))):
    return lax.dot_general(a, b, dims, preferred_element_type=F32)


_NT = (((1,), (1,)), ((), ()))
_TN = (((0,), (0,)), ((), ()))


def _split2(x):
    hi = x.astype(BF16)
    lo = (x - hi.astype(F32)).astype(BF16)
    return hi, lo


def _split3(x):
    hi = x.astype(BF16)
    r1 = x - hi.astype(F32)
    mid = r1.astype(BF16)
    lo = (r1 - mid.astype(F32)).astype(BF16)
    return hi, mid, lo


def _dot_x2(x, w_exact):
    hi, lo = _split2(x)
    return _dot(hi, w_exact) + _dot(lo, w_exact)


def _head_sum(x, hb_ref):
    w = hb_ref.shape[0]
    hb = hb_ref[...]
    return jnp.concatenate(
        [_dot(x[:, c * w:(c + 1) * w].astype(BF16), hb) for c in range(x.shape[1] // w)], axis=1)


def _rms_rows(x, g):
    ms = jnp.mean(x * x, axis=-1, keepdims=True)
    return (x * lax.rsqrt(ms + NORM_EPS)) * g


def _gelu(x):
    return 0.5 * x * (1.0 + lax.erf(x * (1.0 / math.sqrt(2.0))))


def _shift_mix_tile(a_scr, ap_scr, w_ref, sb_ref, first, cols):
    w = w_ref[:, cols]
    p = _dot(a_scr[...], w)
    pp = _dot(ap_scr[...], w)
    prev_row = jnp.where(first, 0.0, pp[PREV_ROWS - 1:, :])
    row = lax.broadcasted_iota(jnp.int32, p.shape, 0)
    shifted = jnp.where(row == 0, prev_row, pltpu.roll(p, 1, 0))
    return p * sb_ref[0:1, cols] + shifted * sb_ref[1:2, cols]


def _in_proj_kernel(x_ref, xp_ref, g_ref, w_ref, sb_ref, lnw_ref, lnb_ref,
                    ww_ref, wa_ref, wg_ref, vec_ref, hb_ref,
                    zu_ref, vn_ref, p_ref, lw_ref, kk_ref, aa_ref, gg_ref, a_scr, ap_scr, *,
                    tiles_per_seq, n_mix, key_tile, n_lora):
    d = x_ref.shape[1]
    n_main = p_ref.shape[1] // d
    first = (pl.program_id(0) % tiles_per_seq) == 0
    tile = lambda t: slice(t * d, (t + 1) * d)
    w0, a0, k_k, k_a = (vec_ref[r:r + 1, :] for r in range(4))

    a_scr[...] = _rms_rows(x_ref[...], g_ref[...]).astype(BF16)
    ap_scr[...] = _rms_rows(xp_ref[...], g_ref[...]).astype(BF16)

    def pieces(t):
        return [(slice(s, s + MXU_WIDTH), slice(t * d + s, t * d + s + MXU_WIDTH))
                for s in range(0, d, MXU_WIDTH)]

    lo0 = (2 + n_main) * d
    lo = _shift_mix_tile(a_scr, ap_scr, w_ref, sb_ref, first, slice(lo0, lo0 + n_lora))
    xw = jnp.tanh(lo[:, 0:LANES]).astype(BF16)
    xa = lo[:, LANES:2 * LANES].astype(BF16)
    xg = jax.nn.sigmoid(lo[:, 2 * LANES:]).astype(BF16)
    for c, _ in pieces(0):
        zw = w0[:, c] + _dot(xw, ww_ref[:, c])
        lw_ref[:, c] = (-math.exp(-0.5)) * jax.nn.sigmoid(zw)
        gg_ref[:, c] = _dot(xg, wg_ref[:, c]).astype(gg_ref.dtype)

    for dst, src in pieces(0):
        zu_ref[:, dst] = _gelu(_dot(a_scr[...], w_ref[:, src])).astype(BF16)

    zv = _gelu(_dot(a_scr[...], w_ref[:, tile(1)]))
    mu = jnp.mean(zv, axis=-1, keepdims=True)
    zc = zv - mu
    var = jnp.mean(zc * zc, axis=-1, keepdims=True)
    vn_ref[...] = ((zc * lax.rsqrt(var + LN_EPS)) * lnw_ref[...] + lnb_ref[...]).astype(BF16)

    for t in range(n_main):
        for dst, src in pieces(2 + t):
            if t < n_mix:
                p = _shift_mix_tile(a_scr, ap_scr, w_ref, sb_ref, first, src)
            else:
                p = _dot(a_scr[...], w_ref[:, src])
            if t == key_tile:
                aa = jax.nn.sigmoid(a0[:, dst] + _dot(xa, wa_ref[:, dst]))
                aa_ref[:, dst] = aa.astype(aa_ref.dtype)
                kraw = p * k_k[:, dst]
                ss = _dot_x2(kraw * kraw, hb_ref[...])
                kk_ref[:, dst] = (kraw * lax.rsqrt(jnp.maximum(ss, 1e-24))).astype(kk_ref.dtype)
                p = p * (1.0 + (aa - 1.0) * k_a[:, dst])
            p_ref[:, slice(t * d + dst.start, t * d + dst.stop)] = p.astype(p_ref.dtype)


def _in_proj(x2, g, w16, sb, ln_w, ln_b, ww, wa, wg, vecs, hb, seq, tm, n_mix, key_tile, n_lora):
    m, d = x2.shape
    n = w16.shape[1]
    n_main = n // d - 3
    assert seq % tm == 0 and n % d == 0 and hb.shape[0] == MXU_WIDTH
    const = lambda a: pl.BlockSpec(a.shape, lambda i: (0, 0), pipeline_mode=pl.Buffered(1))
    rows = lambda w: pl.BlockSpec((tm, w), lambda i: (i, 0))
    act = lambda dt: jax.ShapeDtypeStruct((m, d), dt)
    return pl.pallas_call(
        functools.partial(_in_proj_kernel, tiles_per_seq=seq // tm, n_mix=n_mix,
                          key_tile=key_tile, n_lora=n_lora),
        grid=(m // tm,),
        in_specs=[
            rows(d),
            pl.BlockSpec((PREV_ROWS, d), lambda i: (jnp.maximum(i * (tm // PREV_ROWS) - 1, 0), 0)),
            const(g), const(w16), const(sb), const(ln_w), const(ln_b),
            const(ww), const(wa), const(wg), const(vecs), const(hb),
        ],
        out_specs=[rows(d), rows(d), rows(n_main * d), rows(d), rows(d), rows(d), rows(d)],
        out_shape=[act(BF16), act(BF16), jax.ShapeDtypeStruct((m, n_main * d), BF16),
                   act(F32), act(BF16), act(BF16), act(BF16)],
        scratch_shapes=[pltpu.VMEM((tm, d), BF16), pltpu.VMEM((PREV_ROWS, d), BF16)],
        compiler_params=_cparams(("parallel",)),
        name="in_proj",
    )(x2, x2, g, w16, sb, ln_w, ln_b, ww, wa, wg, vecs, hb)


def _sgu_tile(zu_ref, vn_ref, ga_ref, bias_ref, wpa_ref, wm_scr, s_scr):
    ch = SGU_CHUNK
    dg = LANES
    for c in range(zu_ref.shape[0] // ch):
        rows = pl.ds(c * ch, ch)
        for g in range(SGU_GROUPS):
            cols = slice(g * dg, (g + 1) * dg)
            sv = _dot(wm_scr[g], vn_ref[rows, cols]) + bias_ref[:, cols]
            s_scr[rows, cols] = (zu_ref[rows, cols] * sv).astype(BF16)
    return jax.nn.sigmoid(ga_ref[...].astype(F32)) * _dot(s_scr[...], wpa_ref[...])


def _scan_kernel(r_ref, k_ref, v_ref, lw_ref, kk_ref, aa_ref, o_ref, s_scr, *, chunk, hd, gw):
    c_len = chunk
    assert c_len == hd
    bb, tblk, d = r_ref.shape
    n_groups = d // gw
    hpg = gw // hd
    n_double = int(math.log2(c_len)) - 1

    rowi = lax.broadcasted_iota(jnp.int32, (c_len, gw), 0)
    sub = lax.broadcasted_iota(jnp.int32, (c_len, gw), 1) % c_len
    strict = sub < rowi
    incl = sub <= rowi
    eye = jnp.where(sub == rowi, 1.0, 0.0).astype(F32)
    bd_mask = (lax.broadcasted_iota(jnp.int32, (gw, gw), 0) // c_len
               == lax.broadcasted_iota(jnp.int32, (gw, gw), 1) // hd)
    ltri = jnp.where(lax.broadcasted_iota(jnp.int32, (c_len, c_len), 1)
                     <= lax.broadcasted_iota(jnp.int32, (c_len, c_len), 0), 1.0, 0.0).astype(BF16)

    @pl.when(pl.program_id(1) == 0)
    def _():
        s_scr[...] = jnp.zeros_like(s_scr)

    def bd(x16):
        t = jnp.concatenate([x16] * hpg, axis=0)
        return jnp.where(bd_mask, t, jnp.zeros_like(t))

    chains = [(b, g) for b in range(bb) for g in range(n_groups)]
    n = range(len(chains))

    def body(c, carry):
        rows = pl.ds(pl.multiple_of(c * c_len, c_len), c_len)

        def at(ref, i):
            b, g = chains[i]
            return ref.at[b, rows, g * gw:(g + 1) * gw]

        ar, bq, kq, v16, bk, g_tot = [], [], [], [], [], []
        for i in n:
            k = at(k_ref, i)[...].astype(F32)
            lw = at(lw_ref, i)[...]
            kk = at(kk_ref, i)[...].astype(F32)
            b = kk * at(aa_ref, i)[...].astype(F32)
            lh, lm, ll = _split3(lw)
            cum = _dot(ltri, lh) + (_dot(ltri, lm) + _dot(ltri, ll))
            cum_last = cum[c_len - 1:c_len, :]
            g_inv = jnp.exp(-cum)
            g_end = jnp.exp(cum_last - cum)
            aq = -(kk * jnp.exp(cum - lw))
            rq = at(r_ref, i)[...].astype(F32) * jnp.exp(cum)
            ar.append(jnp.concatenate([aq, rq], axis=0).astype(BF16))
            bq.append((b * g_inv).astype(BF16))
            kq.append((k * g_inv).astype(BF16))
            v16.append(at(v_ref, i)[...].astype(BF16))
            bk.append(jnp.concatenate([b * g_end, k * g_end], axis=0).astype(BF16))
            g_tot.append(jnp.exp(cum_last))

        pb = [_dot(ar[i], bd(bq[i]), _NT) for i in n]
        pk = [_dot(ar[i], bd(kq[i]), _NT) for i in n]
        s0 = [s_scr[chains[i]] for i in n]
        ar_s = [_dot(ar[i], s0[i].astype(BF16), _NT) for i in n]
        a_ab = [jnp.where(strict, pb[i][:c_len], 0.0) for i in n]
        a_rb = [jnp.where(incl, pb[i][c_len:], 0.0).astype(BF16) for i in n]
        akv = [_dot(jnp.concatenate([jnp.where(strict, pk[i][:c_len], 0.0),
                                     jnp.where(incl, pk[i][c_len:], 0.0)], axis=0).astype(BF16),
                    bd(v16[i])) for i in n]
        rhs = [ar_s[i][:c_len] + akv[i][:c_len] for i in n]

        p = [_dot(a_ab[i].astype(BF16), bd(a_ab[i].astype(BF16))) for i in n]
        x = [eye + a_ab[i] for i in n]
        for _ in range(n_double - 1):
            xp = [_dot(jnp.concatenate([x[i], p[i]], axis=0).astype(BF16), bd(p[i].astype(BF16)))
                  for i in n]
            x = [x[i] + xp[i][:c_len] for i in n]
            p = [xp[i][c_len:] for i in n]
        x = [x[i] + _dot(x[i].astype(BF16), bd(p[i].astype(BF16))) for i in n]

        sa16 = [_dot(x[i].astype(BF16), bd(rhs[i].astype(BF16))).astype(BF16) for i in n]
        for i in n:
            at(o_ref, i)[...] = ((ar_s[i][c_len:] + akv[i][c_len:])
                                 + _dot(a_rb[i], bd(sa16[i]))).astype(o_ref.dtype)
        for i in n:
            upd = _dot(jnp.concatenate([sa16[i], v16[i]], axis=0), bk[i], _TN)
            s_scr[chains[i]] = jnp.where(bd_mask, s0[i] * g_tot[i] + upd, 0.0)
        return carry

    lax.fori_loop(0, tblk // c_len, body, 0)


def _rwkv_scan(p_main, lw, kk, aa, batch, seq, d, bb, tblk):
    gw = SCAN_GROUP_HEADS * RWKV_HEAD
    spec = pl.BlockSpec((bb, tblk, d), lambda b, t: (b, t, 0))
    col = lambda cb: pl.BlockSpec((bb, tblk, d), lambda b, t: (b, t, cb))
    as3 = lambda a: a.reshape(batch, seq, a.shape[-1])
    out = pl.pallas_call(
        functools.partial(_scan_kernel, chunk=SCAN_CHUNK, hd=RWKV_HEAD, gw=gw),
        grid=(batch // bb, seq // tblk),
        in_specs=[col(0), col(1), col(2), spec, spec, spec],
        out_specs=spec,
        out_shape=jax.ShapeDtypeStruct((batch, seq, d), BF16),
        scratch_shapes=[pltpu.VMEM((bb, d // gw, gw, gw), F32)],
        compiler_params=_cparams(("parallel", "arbitrary")),
        name="rwkv_scan",
    )(as3(p_main), as3(p_main), as3(p_main), as3(lw), as3(kk), as3(aa))
    return out.reshape(batch * seq, d)


def _mix_kernel(zu_ref, vn_ref, ga_ref, o_ref, r_ref, k_ref, v_ref, g_ref, gb_ref, x_ref,
                ws_ref, bias_ref, vec_ref, hb_ref, wpa_ref, wpb_ref, wout_ref,
                h_ref, f_ref, wm_scr, s_scr, *, hd):
    @pl.when(pl.program_id(0) == 0)
    def _():
        ch = SGU_CHUNK
        row = lax.broadcasted_iota(jnp.int32, (ch, ch), 0)
        col = lax.broadcasted_iota(jnp.int32, (ch, ch), 1)
        for g in range(SGU_GROUPS):
            wm_scr[g] = jnp.where(col <= row, ws_ref[g], 0.0).astype(BF16)

    ya = _sgu_tile(zu_ref, vn_ref, ga_ref, bias_ref, wpa_ref, wm_scr, s_scr)

    lnx_w = vec_ref[0:1, :]
    lnx_b = vec_ref[1:2, :]
    r_k = vec_ref[2:3, :]
    g_ffn = vec_ref[3:4, :]
    o = o_ref[...].astype(F32)
    inv_n = 1.0 / hd
    mu = _head_sum(o, hb_ref) * inv_n
    oc = o - mu
    var = _head_sum(oc * oc, hb_ref) * inv_n
    on = (oc * lax.rsqrt(var + GN_EPS)) * lnx_w + lnx_b
    v = v_ref[...].astype(F32)
    rk = r_ref[...].astype(F32) * k_ref[...].astype(F32)
    bonus = _head_sum(rk * r_k, hb_ref) * v
    yb = _dot(((on + bonus) * g_ref[...]).astype(BF16), wpb_ref[...])
    mixed = ya + jax.nn.sigmoid(gb_ref[...].astype(F32)) * yb
    h = x_ref[...] + _dot(mixed.astype(BF16), wout_ref[...])
    h_ref[...] = h
    f_ref[...] = _rms_rows(h, g_ffn).astype(BF16)


def _mix(zu, vn, o, g, p_main, x2, sgu_w, bias_full, vecs, hb, wpa16, wpb16, wout16, tm, d):
    m = x2.shape[0]
    ch = SGU_CHUNK
    tile = pl.BlockSpec((tm, d), lambda i: (i, 0))
    col = lambda cb: pl.BlockSpec((tm, d), lambda i: (i, cb))
    const = lambda a: pl.BlockSpec(a.shape, lambda i: (0,) * a.ndim,
                                   pipeline_mode=pl.Buffered(1))
    return pl.pallas_call(
        functools.partial(_mix_kernel, hd=RWKV_HEAD),
        grid=(m // tm,),
        in_specs=[tile, tile, col(3), tile, col(0), col(1), col(2), tile, col(4), tile,
                  const(sgu_w), const(bias_full), const(vecs), const(hb),
                  const(wpa16), const(wpb16), const(wout16)],
        out_specs=[tile, tile],
        out_shape=[jax.ShapeDtypeStruct((m, d), F32), jax.ShapeDtypeStruct((m, d), BF16)],
        scratch_shapes=[pltpu.VMEM((SGU_GROUPS, ch, ch), BF16), pltpu.VMEM((tm, d), BF16)],
        compiler_params=_cparams(("arbitrary",)),
        name="mix",
    )(zu, vn, p_main, o, p_main, p_main, p_main, g, p_main, x2,
      sgu_w, bias_full, vecs, hb, wpa16, wpb16, wout16)


def _ffn_kernel(h_ref, f_ref, w1_ref, w2_ref, gf_ref, o_ref, *, tf):
    acc = h_ref[...]
    for j in range(w1_ref.shape[1] // tf):
        cols = slice(j * tf, (j + 1) * tf)
        t = jnp.maximum(_dot(f_ref[...], w1_ref[:, cols]), 0.0)
        acc = acc + _dot((t * t).astype(BF16), w2_ref[cols, :])
    o_ref[...] = _rms_rows(acc, gf_ref[...])


def _ffn(h, f, w1_16, w2_16, g_final, tm, tf):
    m, d = h.shape
    dff = w1_16.shape[1]
    assert dff % tf == 0
    tile = pl.BlockSpec((tm, d), lambda i: (i, 0))
    const = lambda shape: pl.BlockSpec(shape, lambda i: (0, 0), pipeline_mode=pl.Buffered(1))
    return pl.pallas_call(
        functools.partial(_ffn_kernel, tf=tf),
        grid=(m // tm,),
        in_specs=[tile, tile, const((d, dff)), const((dff, d)), const((1, d))],
        out_specs=tile,
        out_shape=jax.ShapeDtypeStruct((m, d), F32),
        compiler_params=_cparams(("parallel",)),
        name="ffn",
    )(h, f, w1_16, w2_16, g_final)


def _pad_cols(a, n):
    return jnp.pad(a, ((0, 0), (0, n - a.shape[1])))


def _pad_rows(a, n):
    return jnp.pad(a, ((0, n - a.shape[0]), (0, 0)))


def _layer(x2, batch, seq, g_mix, w_in, sgu_ln_w, sgu_ln_b, sgu_w, sgu_b, w_proj_a, shift_b,
           w_lora_w, w0, a_lora_w, a0, g_lora_w, k_k, k_a, r_k, ln_x_w, ln_x_b, w_proj_b,
           w_out, g_ffn, w_ffn1, w_ffn2, g_out):
    d = x2.shape[1]
    lora_w, lora_a, lora_g = w_lora_w.shape[0], a_lora_w.shape[0], g_lora_w.shape[0]
    c_sgu = 2 * d
    c_rkv = 3 * d
    c_lora = lora_w + lora_a + lora_g
    o_lora = c_sgu + c_rkv
    o_gate = o_lora + c_lora
    pw, pa = LANES, LANES
    pg = -(-lora_g // LANES) * LANES

    n_lora = pw + pa + pg
    w16 = w_in.astype(BF16)
    w_all = jnp.concatenate([
        w16[:, :o_lora], w16[:, o_gate:],
        _pad_cols(w16[:, o_lora:o_lora + lora_w], pw),
        _pad_cols(w16[:, o_lora + lora_w:o_lora + lora_w + lora_a], pa),
        _pad_cols(w16[:, o_lora + lora_w + lora_a:o_gate], pg + d - n_lora)], axis=1)
    sb_lo = shift_b[:, c_rkv:]
    sb_all = jnp.concatenate([
        jnp.zeros((2, c_sgu), F32), shift_b[:, :c_rkv], jnp.zeros((2, 2 * d), F32),
        _pad_cols(sb_lo[:, :lora_w], pw),
        _pad_cols(sb_lo[:, lora_w:lora_w + lora_a], pa),
        _pad_cols(sb_lo[:, lora_w + lora_a:], pg + d - n_lora)], axis=1)

    gw = SCAN_GROUP_HEADS * RWKV_HEAD
    hb = (lax.broadcasted_iota(jnp.int32, (gw, gw), 0) // RWKV_HEAD
          == lax.broadcasted_iota(jnp.int32, (gw, gw), 1) // RWKV_HEAD).astype(BF16)
    vec_prep = jnp.stack([w0, a0, k_k, k_a])
    lo16 = lambda w, rows: _pad_rows(w, rows).astype(BF16)
    zu, vn, p_main, lw, kk, aa, g = _in_proj(
        x2, g_mix.reshape(1, d), w_all, sb_all, sgu_ln_w.reshape(1, d), sgu_ln_b.reshape(1, d),
        lo16(w_lora_w, pw), lo16(a_lora_w, pa), lo16(g_lora_w, pg), vec_prep, hb, seq,
        tm=512, n_mix=c_rkv // d, key_tile=1, n_lora=n_lora)

    bias_full = jnp.repeat(sgu_b.T, d // SGU_GROUPS, axis=1)

    o = _rwkv_scan(p_main, lw, kk, aa, batch, seq, d, bb=4, tblk=256)

    vec_post = jnp.stack([ln_x_w, ln_x_b, r_k, g_ffn])
    h1, f = _mix(zu, vn, o, g, p_main, x2, sgu_w, bias_full, vec_post, hb,
                 w_proj_a.astype(BF16), w_proj_b.astype(BF16), w_out.astype(BF16), tm=512, d=d)
    return _ffn(h1, f, w_ffn1.astype(BF16), w_ffn2.astype(BF16), g_out.reshape(1, d),
                tm=512, tf=1024)


def kernel(x, g_mix, w_in, sgu_ln_w, sgu_ln_b, sgu_w, sgu_b, w_proj_a, shift_b, w_lora_w, w0,
           a_lora_w, a0, g_lora_w, k_k, k_a, r_k, ln_x_w, ln_x_b, w_proj_b, w_out, g_ffn,
           w_ffn1, w_ffn2, g_final):
    batch, seq, d = x.shape
    depth = w_in.shape[0]
    assert depth == 1, "the final RMSNorm is fused into the single layer's ffn call"
    h = x.reshape(batch * seq, d)
    l = 0
    h = _layer(h, batch, seq, g_mix[l], w_in[l], sgu_ln_w[l], sgu_ln_b[l], sgu_w[l], sgu_b[l],
               w_proj_a[l], shift_b[l], w_lora_w[l], w0[l], a_lora_w[l], a0[l], g_lora_w[l],
               k_k[l], k_a[l], r_k[l], ln_x_w[l], ln_x_b[l], w_proj_b[l], w_out[l], g_ffn[l],
               w_ffn1[l], w_ffn2[l], g_final)
    return h.reshape(batch, seq, d)
```

```python
import functools
import math

import jax
import jax.numpy as jnp
from jax import lax
from jax.experimental import pallas as pl
from jax.experimental.pallas import tpu as pltpu

F32 = jnp.float32
BF16 = jnp.bfloat16

SGU_CHUNK = 128
SGU_GROUPS = 8
RWKV_HEAD = 64
NORM_EPS = 1e-6
LN_EPS = 1e-5
GN_EPS = 64e-5

LANES = 128
MXU_WIDTH = 256
PREV_ROWS = 16
NORM_ROW_CHUNKS = 4
SCAN_CHUNK = 64
SCAN_GROUP_HEADS = MXU_WIDTH // RWKV_HEAD
VMEM_LIMIT = 56 * 1024 * 1024


def _cparams(sem):
    return pltpu.CompilerParams(dimension_semantics=sem, vmem_limit_bytes=VMEM_LIMIT)


def _dot(a, b, dims=(((1,), (0,)), ((), ()))):
    return lax.dot_general(a, b, dims, preferred_element_type=F32)


_NT = (((1,), (1,)), ((), ()))
_TN = (((0,), (0,)), ((), ()))


def _split2(x):
    hi = x.astype(BF16)
    lo = (x - hi.astype(F32)).astype(BF16)
    return hi, lo


def _split3(x):
    hi = x.astype(BF16)
    r1 = x - hi.astype(F32)
    mid = r1.astype(BF16)
    lo = (r1 - mid.astype(F32)).astype(BF16)
    return hi, mid, lo


def _dot_x2(x, w_exact):
    hi, lo = _split2(x)
    return _dot(hi, w_exact) + _dot(lo, w_exact)


def _head_sum(x, hb_ref):
    w = hb_ref.shape[0]
    hb = hb_ref[...]
    return jnp.concatenate(
        [_dot(x[:, c * w:(c + 1) * w].astype(BF16), hb) for c in range(x.shape[1] // w)], axis=1)


def _rms_rows(x, g):
    ms = jnp.mean(x * x, axis=-1, keepdims=True)
    return (x * lax.rsqrt(ms + NORM_EPS)) * g


def _gelu(x):
    return 0.5 * x * (1.0 + lax.erf(x * (1.0 / math.sqrt(2.0))))


def _shift_mix_tile(a_scr, ap_scr, w_ref, sb_ref, first, cols, row_chunks=1):
    w = w_ref[:, cols]
    rc = a_scr.shape[0] // row_chunks
    p = jnp.concatenate([_dot(a_scr[c * rc:(c + 1) * rc, :], w) for c in range(row_chunks)],
                        axis=0)
    pp = _dot(ap_scr[...], w)
    prev_row = jnp.where(first, 0.0, pp[PREV_ROWS - 1:, :])
    row = lax.broadcasted_iota(jnp.int32, p.shape, 0)
    shifted = jnp.where(row == 0, prev_row, pltpu.roll(p, 1, 0))
    return p * sb_ref[0:1, cols] + shifted * sb_ref[1:2, cols]


def _in_proj_kernel(x_ref, xp_ref, g_ref, w_ref, sb_ref, lnw_ref, lnb_ref,
                    ww_ref, wa_ref, wg_ref, vec_ref, hb_ref,
                    zu_ref, vn_ref, p_ref, lw_ref, kk_ref, aa_ref, gg_ref, a_scr, ap_scr, *,
                    tiles_per_seq, n_mix, key_tile, n_lora):
    d = x_ref.shape[1]
    n_main = p_ref.shape[1] // d
    first = (pl.program_id(0) % tiles_per_seq) == 0
    tile = lambda t: slice(t * d, (t + 1) * d)
    w0, a0, k_k, k_a = (vec_ref[r:r + 1, :] for r in range(4))

    ap_scr[...] = _rms_rows(xp_ref[...], g_ref[...]).astype(BF16)
    rc = x_ref.shape[0] // NORM_ROW_CHUNKS
    for c in range(NORM_ROW_CHUNKS):
        rows = pl.ds(c * rc, rc)
        a_scr[rows, :] = _rms_rows(x_ref[rows, :], g_ref[...]).astype(BF16)

    def pieces(t):
        return [(slice(s, s + MXU_WIDTH), slice(t * d + s, t * d + s + MXU_WIDTH))
                for s in range(0, d, MXU_WIDTH)]

    lo0 = (2 + n_main) * d
    lo = _shift_mix_tile(a_scr, ap_scr, w_ref, sb_ref, first, slice(lo0, lo0 + n_lora),
                         row_chunks=NORM_ROW_CHUNKS)
    xw = jnp.tanh(lo[:, 0:LANES]).astype(BF16)
    xa = lo[:, LANES:2 * LANES].astype(BF16)
    xg = jax.nn.sigmoid(lo[:, 2 * LANES:]).astype(BF16)
    for c, _ in pieces(0):
        zw = w0[:, c] + _dot(xw, ww_ref[:, c])
        lw_ref[:, c] = (-math.exp(-0.5)) * jax.nn.sigmoid(zw)
        gg_ref[:, c] = _dot(xg, wg_ref[:, c]).astype(gg_ref.dtype)

    def gelu_tile():
        for dst, src in pieces(0):
            zu_ref[:, dst] = _gelu(_dot(a_scr[...], w_ref[:, src])).astype(BF16)

    def layernorm_tile():
        zv = _gelu(_dot(a_scr[...], w_ref[:, tile(1)]))
        mu = jnp.mean(zv, axis=-1, keepdims=True)
        zc = zv - mu
        var = jnp.mean(zc * zc, axis=-1, keepdims=True)
        vn_ref[...] = ((zc * lax.rsqrt(var + LN_EPS)) * lnw_ref[...]
                       + lnb_ref[...]).astype(BF16)

    def main_tile(t):
        for dst, src in pieces(2 + t):
            if t < n_mix:
                p = _shift_mix_tile(a_scr, ap_scr, w_ref, sb_ref, first, src)
            else:
                p = _dot(a_scr[...], w_ref[:, src])
            if t == key_tile:
                aa = jax.nn.sigmoid(a0[:, dst] + _dot(xa, wa_ref[:, dst]))
                aa_ref[:, dst] = aa.astype(aa_ref.dtype)
                kraw = p * k_k[:, dst]
                ss = _dot_x2(kraw * kraw, hb_ref[...])
                kk_ref[:, dst] = (kraw * lax.rsqrt(jnp.maximum(ss, 1e-24))).astype(kk_ref.dtype)
                p = p * (1.0 + (aa - 1.0) * k_a[:, dst])
            p_ref[:, slice(t * d + dst.start, t * d + dst.stop)] = p.astype(p_ref.dtype)

    plain = list(range(n_mix, n_main))
    heavy = [gelu_tile, layernorm_tile] + [functools.partial(main_tile, t) for t in range(n_mix)]
    for step in range(max(len(plain), len(heavy))):
        if step < len(plain):
            main_tile(plain[step])
        if step < len(heavy):
            heavy[step]()


def _in_proj(x2, g, w16, sb, ln_w, ln_b, ww, wa, wg, vecs, hb, seq, tm, n_mix, key_tile, n_lora):
    m, d = x2.shape
    n = w16.shape[1]
    n_main = n // d - 3
    assert seq % tm == 0 and n % d == 0 and hb.shape[0] == MXU_WIDTH
    const = lambda a: pl.BlockSpec(a.shape, lambda i: (0, 0), pipeline_mode=pl.Buffered(1))
    rows = lambda w: pl.BlockSpec((tm, w), lambda i: (i, 0))
    act = lambda dt: jax.ShapeDtypeStruct((m, d), dt)
    return pl.pallas_call(
        functools.partial(_in_proj_kernel, tiles_per_seq=seq // tm, n_mix=n_mix,
                          key_tile=key_tile, n_lora=n_lora),
        grid=(m // tm,),
        in_specs=[
            rows(d),
            pl.BlockSpec((PREV_ROWS, d), lambda i: (jnp.maximum(i * (tm // PREV_ROWS) - 1, 0), 0)),
            const(g), const(w16), const(sb), const(ln_w), const(ln_b),
            const(ww), const(wa), const(wg), const(vecs), const(hb),
        ],
        out_specs=[rows(d), rows(d), rows(n_main * d), rows(d), rows(d), rows(d), rows(d)],
        out_shape=[act(BF16), act(BF16), jax.ShapeDtypeStruct((m, n_main * d), BF16),
                   act(F32), act(BF16), act(BF16), act(BF16)],
        scratch_shapes=[pltpu.VMEM((tm, d), BF16), pltpu.VMEM((PREV_ROWS, d), BF16)],
        compiler_params=_cparams(("parallel",)),
        name="in_proj",
    )(x2, x2, g, w16, sb, ln_w, ln_b, ww, wa, wg, vecs, hb)


def _sgu_tile(zu_ref, vn_ref, ga_ref, bias_ref, wpa_ref, wm_scr, s_scr):
    ch = SGU_CHUNK
    dg = LANES
    for c in range(zu_ref.shape[0] // ch):
        rows = pl.ds(c * ch, ch)
        for g in range(SGU_GROUPS):
            cols = slice(g * dg, (g + 1) * dg)
            sv = _dot(wm_scr[g], vn_ref[rows, cols]) + bias_ref[:, cols]
            s_scr[rows, cols] = (zu_ref[rows, cols] * sv).astype(BF16)
    return jax.nn.sigmoid(ga_ref[...].astype(F32)) * _dot(s_scr[...], wpa_ref[...])


def _scan_kernel(r_ref, k_ref, v_ref, lw_ref, kk_ref, aa_ref, o_ref, s_scr, *, chunk, hd, gw):
    c_len = chunk
    assert c_len == hd
    bb, tblk, d = r_ref.shape
    n_groups = d // gw
    hpg = gw // hd
    n_double = int(math.log2(c_len)) - 1

    rowi = lax.broadcasted_iota(jnp.int32, (c_len, gw), 0)
    sub = lax.broadcasted_iota(jnp.int32, (c_len, gw), 1) % c_len
    strict = sub < rowi
    incl = sub <= rowi
    eye = jnp.where(sub == rowi, 1.0, 0.0).astype(F32)
    bd_mask = (lax.broadcasted_iota(jnp.int32, (gw, gw), 0) // c_len
               == lax.broadcasted_iota(jnp.int32, (gw, gw), 1) // hd)
    ltri = jnp.where(lax.broadcasted_iota(jnp.int32, (c_len, c_len), 1)
                     <= lax.broadcasted_iota(jnp.int32, (c_len, c_len), 0), 1.0, 0.0).astype(BF16)

    @pl.when(pl.program_id(1) == 0)
    def _():
        s_scr[...] = jnp.zeros_like(s_scr)

    def bd(x16):
        t = jnp.concatenate([x16] * hpg, axis=0)
        return jnp.where(bd_mask, t, jnp.zeros_like(t))

    chains = [(b, g) for b in range(bb) for g in range(n_groups)]
    n = range(len(chains))

    def body(c, carry):
        rows = pl.ds(pl.multiple_of(c * c_len, c_len), c_len)

        def at(ref, i):
            b, g = chains[i]
            return ref.at[b, rows, g * gw:(g + 1) * gw]

        ar, bq, kq, v16, bk, g_tot = [], [], [], [], [], []
        for i in n:
            k = at(k_ref, i)[...].astype(F32)
            lw = at(lw_ref, i)[...]
            kk = at(kk_ref, i)[...].astype(F32)
            b = kk * at(aa_ref, i)[...].astype(F32)
            lh, lm, ll = _split3(lw)
            cum = _dot(ltri, lh) + (_dot(ltri, lm) + _dot(ltri, ll))
            cum_last = cum[c_len - 1:c_len, :]
            g_inv = jnp.exp(-cum)
            g_end = jnp.exp(cum_last - cum)
            aq = -(kk * jnp.exp(cum - lw))
            rq = at(r_ref, i)[...].astype(F32) * jnp.exp(cum)
            ar.append(jnp.concatenate([aq, rq], axis=0).astype(BF16))
            bq.append((b * g_inv).astype(BF16))
            kq.append((k * g_inv).astype(BF16))
            v16.append(at(v_ref, i)[...].astype(BF16))
            bk.append(jnp.concatenate([b * g_end, k * g_end], axis=0).astype(BF16))
            g_tot.append(jnp.exp(cum_last))

        pb = [_dot(ar[i], bd(bq[i]), _NT) for i in n]
        pk = [_dot(ar[i], bd(kq[i]), _NT) for i in n]
        s0 = [s_scr[chains[i]] for i in n]
        ar_s = [_dot(ar[i], s0[i].astype(BF16), _NT) for i in n]
        a_ab = [jnp.where(strict, pb[i][:c_len], 0.0) for i in n]
        a_rb = [jnp.where(incl, pb[i][c_len:], 0.0).astype(BF16) for i in n]
        akv = [_dot(jnp.concatenate([jnp.where(strict, pk[i][:c_len], 0.0),
                                     jnp.where(incl, pk[i][c_len:], 0.0)], axis=0).astype(BF16),
                    bd(v16[i])) for i in n]
        rhs = [ar_s[i][:c_len] + akv[i][:c_len] for i in n]

        p = [_dot(a_ab[i].astype(BF16), bd(a_ab[i].astype(BF16))) for i in n]
        x = [eye + a_ab[i] for i in n]
        for _ in range(n_double - 1):
            xp = [_dot(jnp.concatenate([x[i], p[i]], axis=0).astype(BF16), bd(p[i].astype(BF16)))
                  for i in n]
            x = [x[i] + xp[i][:c_len] for i in n]
            p = [xp[i][c_len:] for i in n]
        x = [x[i] + _dot(x[i].astype(BF16), bd(p[i].astype(BF16))) for i in n]

        sa16 = [_dot(x[i].astype(BF16), bd(rhs[i].astype(BF16))).astype(BF16) for i in n]
        for i in n:
            at(o_ref, i)[...] = ((ar_s[i][c_len:] + akv[i][c_len:])
                                 + _dot(a_rb[i], bd(sa16[i]))).astype(o_ref.dtype)
        for i in n:
            upd = _dot(jnp.concatenate([sa16[i], v16[i]], axis=0), bk[i], _TN)
            s_scr[chains[i]] = jnp.where(bd_mask, s0[i] * g_tot[i] + upd, 0.0)
        return carry

    lax.fori_loop(0, tblk // c_len, body, 0)


def _rwkv_scan(p_main, lw, kk, aa, batch, seq, d, bb, tblk):
    gw = SCAN_GROUP_HEADS * RWKV_HEAD
    spec = pl.BlockSpec((bb, tblk, d), lambda b, t: (b, t, 0))
    col = lambda cb: pl.BlockSpec((bb, tblk, d), lambda b, t: (b, t, cb))
    as3 = lambda a: a.reshape(batch, seq, a.shape[-1])
    out = pl.pallas_call(
        functools.partial(_scan_kernel, chunk=SCAN_CHUNK, hd=RWKV_HEAD, gw=gw),
        grid=(batch // bb, seq // tblk),
        in_specs=[col(0), col(1), col(2), spec, spec, spec],
        out_specs=spec,
        out_shape=jax.ShapeDtypeStruct((batch, seq, d), BF16),
        scratch_shapes=[pltpu.VMEM((bb, d // gw, gw, gw), F32)],
        compiler_params=_cparams(("parallel", "arbitrary")),
        name="rwkv_scan",
    )(as3(p_main), as3(p_main), as3(p_main), as3(lw), as3(kk), as3(aa))
    return out.reshape(batch * seq, d)


def _mix_kernel(zu_ref, vn_ref, ga_ref, o_ref, r_ref, k_ref, v_ref, g_ref, gb_ref, x_ref,
                ws_ref, bias_ref, vec_ref, hb_ref, wpa_ref, wpb_ref, wout_ref,
                h_ref, f_ref, wm_scr, s_scr, *, hd):
    @pl.when(pl.program_id(0) == 0)
    def _():
        ch = SGU_CHUNK
        row = lax.broadcasted_iota(jnp.int32, (ch, ch), 0)
        col = lax.broadcasted_iota(jnp.int32, (ch, ch), 1)
        for g in range(SGU_GROUPS):
            wm_scr[g] = jnp.where(col <= row, ws_ref[g], 0.0).astype(BF16)

    ya = _sgu_tile(zu_ref, vn_ref, ga_ref, bias_ref, wpa_ref, wm_scr, s_scr)

    lnx_w = vec_ref[0:1, :]
    lnx_b = vec_ref[1:2, :]
    r_k = vec_ref[2:3, :]
    g_ffn = vec_ref[3:4, :]
    o = o_ref[...].astype(F32)
    inv_n = 1.0 / hd
    mu = _head_sum(o, hb_ref) * inv_n
    oc = o - mu
    var = _head_sum(oc * oc, hb_ref) * inv_n
    on = (oc * lax.rsqrt(var + GN_EPS)) * lnx_w + lnx_b
    v = v_ref[...].astype(F32)
    rk = r_ref[...].astype(F32) * k_ref[...].astype(F32)
    bonus = _head_sum(rk * r_k, hb_ref) * v
    yb = _dot(((on + bonus) * g_ref[...]).astype(BF16), wpb_ref[...])
    mixed = ya + jax.nn.sigmoid(gb_ref[...].astype(F32)) * yb
    h = x_ref[...] + _dot(mixed.astype(BF16), wout_ref[...])
    h_ref[...] = h
    f_ref[...] = _rms_rows(h, g_ffn).astype(BF16)


def _mix(zu, vn, o, g, p_main, x2, sgu_w, bias_full, vecs, hb, wpa16, wpb16, wout16, tm, d):
    m = x2.shape[0]
    ch = SGU_CHUNK
    tile = pl.BlockSpec((tm, d), lambda i: (i, 0))
    col = lambda cb: pl.BlockSpec((tm, d), lambda i: (i, cb))
    const = lambda a: pl.BlockSpec(a.shape, lambda i: (0,) * a.ndim,
                                   pipeline_mode=pl.Buffered(1))
    return pl.pallas_call(
        functools.partial(_mix_kernel, hd=RWKV_HEAD),
        grid=(m // tm,),
        in_specs=[tile, tile, col(3), tile, col(0), col(1), col(2), tile, col(4), tile,
                  const(sgu_w), const(bias_full), const(vecs), const(hb),
                  const(wpa16), const(wpb16), const(wout16)],
        out_specs=[tile, tile],
        out_shape=[jax.ShapeDtypeStruct((m, d), F32), jax.ShapeDtypeStruct((m, d), BF16)],
        scratch_shapes=[pltpu.VMEM((SGU_GROUPS, ch, ch), BF16), pltpu.VMEM((tm, d), BF16)],
        compiler_params=_cparams(("arbitrary",)),
        name="mix",
    )(zu, vn, p_main, o, p_main, p_main, p_main, g, p_main, x2,
      sgu_w, bias_full, vecs, hb, wpa16, wpb16, wout16)


def _ffn_kernel(h_ref, f_ref, w1_ref, w2_ref, gf_ref, o_ref, *, tf):
    acc = h_ref[...]
    for j in range(w1_ref.shape[1] // tf):
        cols = slice(j * tf, (j + 1) * tf)
        t = jnp.maximum(_dot(f_ref[...], w1_ref[:, cols]), 0.0)
        acc = acc + _dot((t * t).astype(BF16), w2_ref[cols, :])
    o_ref[...] = _rms_rows(acc, gf_ref[...])


def _ffn(h, f, w1_16, w2_16, g_final, tm, tf):
    m, d = h.shape
    dff = w1_16.shape[1]
    assert dff % tf == 0
    tile = pl.BlockSpec((tm, d), lambda i: (i, 0))
    const = lambda shape: pl.BlockSpec(shape, lambda i: (0, 0), pipeline_mode=pl.Buffered(1))
    return pl.pallas_call(
        functools.partial(_ffn_kernel, tf=tf),
        grid=(m // tm,),
        in_specs=[tile, tile, const((d, dff)), const((dff, d)), const((1, d))],
        out_specs=tile,
        out_shape=jax.ShapeDtypeStruct((m, d), F32),
        compiler_params=_cparams(("parallel",)),
        name="ffn",
    )(h, f, w1_16, w2_16, g_final)


def _pad_cols(a, n):
    return jnp.pad(a, ((0, 0), (0, n - a.shape[1])))


def _pad_rows(a, n):
    return jnp.pad(a, ((0, n - a.shape[0]), (0, 0)))


def _layer(x2, batch, seq, g_mix, w_in, sgu_ln_w, sgu_ln_b, sgu_w, sgu_b, w_proj_a, shift_b,
           w_lora_w, w0, a_lora_w, a0, g_lora_w, k_k, k_a, r_k, ln_x_w, ln_x_b, w_proj_b,
           w_out, g_ffn, w_ffn1, w_ffn2, g_out):
    d = x2.shape[1]
    lora_w, lora_a, lora_g = w_lora_w.shape[0], a_lora_w.shape[0], g_lora_w.shape[0]
    c_sgu = 2 * d
    c_rkv = 3 * d
    c_lora = lora_w + lora_a + lora_g
    o_lora = c_sgu + c_rkv
    o_gate = o_lora + c_lora
    pw, pa = LANES, LANES
    pg = -(-lora_g // LANES) * LANES

    n_lora = pw + pa + pg
    w16 = w_in.astype(BF16)
    w_all = jnp.concatenate([
        w16[:, :o_lora], w16[:, o_gate:],
        _pad_cols(w16[:, o_lora:o_lora + lora_w], pw),
        _pad_cols(w16[:, o_lora + lora_w:o_lora + lora_w + lora_a], pa),
        _pad_cols(w16[:, o_lora + lora_w + lora_a:o_gate], pg + d - n_lora)], axis=1)
    sb_lo = shift_b[:, c_rkv:]
    sb_all = jnp.concatenate([
        jnp.zeros((2, c_sgu), F32), shift_b[:, :c_rkv], jnp.zeros((2, 2 * d), F32),
        _pad_cols(sb_lo[:, :lora_w], pw),
        _pad_cols(sb_lo[:, lora_w:lora_w + lora_a], pa),
        _pad_cols(sb_lo[:, lora_w + lora_a:], pg + d - n_lora)], axis=1)

    gw = SCAN_GROUP_HEADS * RWKV_HEAD
    hb = (lax.broadcasted_iota(jnp.int32, (gw, gw), 0) // RWKV_HEAD
          == lax.broadcasted_iota(jnp.int32, (gw, gw), 1) // RWKV_HEAD).astype(BF16)
    vec_prep = jnp.stack([w0, a0, k_k, k_a])
    lo16 = lambda w, rows: _pad_rows(w, rows).astype(BF16)
    zu, vn, p_main, lw, kk, aa, g = _in_proj(
        x2, g_mix.reshape(1, d), w_all, sb_all, sgu_ln_w.reshape(1, d), sgu_ln_b.reshape(1, d),
        lo16(w_lora_w, pw), lo16(a_lora_w, pa), lo16(g_lora_w, pg), vec_prep, hb, seq,
        tm=512, n_mix=c_rkv // d, key_tile=1, n_lora=n_lora)

    bias_full = jnp.repeat(sgu_b.T, d // SGU_GROUPS, axis=1)

    o = _rwkv_scan(p_main, lw, kk, aa, batch, seq, d, bb=4, tblk=256)

    vec_post = jnp.stack([ln_x_w, ln_x_b, r_k, g_ffn])
    h1, f = _mix(zu, vn, o, g, p_main, x2, sgu_w, bias_full, vec_post, hb,
                 w_proj_a.astype(BF16), w_proj_b.astype(BF16), w_out.astype(BF16), tm=512, d=d)
    return _ffn(h1, f, w_ffn1.astype(BF16), w_ffn2.astype(BF16), g_out.reshape(1, d),
                tm=512, tf=1024)


def kernel(x, g_mix, w_in, sgu_ln_w, sgu_ln_b, sgu_w, sgu_b, w_proj_a, shift_b, w_lora_w, w0,
           a_lora_w, a0, g_lora_w, k_k, k_a, r_k, ln_x_w, ln_x_b, w_proj_b, w_out, g_ffn,
           w_ffn1, w_ffn2, g_final):
    batch, seq, d = x.shape
    depth = w_in.shape[0]
    assert depth == 1, "the final RMSNorm is fused into the single layer's ffn call"
    h = x.reshape(batch * seq, d)
    l = 0
    h = _layer(h, batch, seq, g_mix[l], w_in[l], sgu_ln_w[l], sgu_ln_b[l], sgu_w[l], sgu_b[l],
               w_proj_a[l], shift_b[l], w_lora_w[l], w0[l], a_lora_w[l], a0[l], g_lora_w[l],
               k_k[l], k_a[l], r_k[l], ln_x_w[l], ln_x_b[l], w_proj_b[l], w_out[l], g_ffn[l],
               w_ffn1[l], w_ffn2[l], g_final)
    return h.reshape(batch, seq, d)
```

```python
import functools
import math

import jax
import jax.numpy as jnp
from jax import lax
from jax.experimental import pallas as pl
from jax.experimental.pallas import tpu as pltpu

F32 = jnp.float32
BF16 = jnp.bfloat16

SGU_CHUNK = 128
SGU_GROUPS = 8
RWKV_HEAD = 64
NORM_EPS = 1e-6
LN_EPS = 1e-5
GN_EPS = 64e-5

LANES = 128
MXU_WIDTH = 256
PREV_ROWS = 16
NORM_ROW_CHUNKS = 4
SCAN_CHUNK = 64
SCAN_GROUP_HEADS = MXU_WIDTH // RWKV_HEAD
VMEM_LIMIT = 56 * 1024 * 1024


def _cparams(sem):
    return pltpu.CompilerParams(dimension_semantics=sem, vmem_limit_bytes=VMEM_LIMIT)


def _dot(a, b, dims=(((1,), (0,)), ((), ()))):
    return lax.dot_general(a, b, dims, preferred_element_type=F32)


_NT = (((1,), (1,)), ((), ()))
_TN = (((0,), (0,)), ((), ()))


def _split2(x):
    hi = x.astype(BF16)
    lo = (x - hi.astype(F32)).astype(BF16)
    return hi, lo


def _split3(x):
    hi = x.astype(BF16)
    r1 = x - hi.astype(F32)
    mid = r1.astype(BF16)
    lo = (r1 - mid.astype(F32)).astype(BF16)
    return hi, mid, lo


def _dot_x2(x, w_exact):
    hi, lo = _split2(x)
    return _dot(hi, w_exact) + _dot(lo, w_exact)


def _head_sum(x, hb_ref):
    w = hb_ref.shape[0]
    hb = hb_ref[...]
    return jnp.concatenate(
        [_dot(x[:, c * w:(c + 1) * w].astype(BF16), hb) for c in range(x.shape[1] // w)], axis=1)


def _rms_rows(x, g):
    ms = jnp.mean(x * x, axis=-1, keepdims=True)
    return (x * lax.rsqrt(ms + NORM_EPS)) * g


def _gelu(x):
    return 0.5 * x * (1.0 + lax.erf(x * (1.0 / math.sqrt(2.0))))


def _shift_mix_tile(a_scr, ap_scr, w_ref, sb_ref, first, cols, row_chunks=1):
    w = w_ref[:, cols]
    rc = a_scr.shape[0] // row_chunks
    p = jnp.concatenate([_dot(a_scr[c * rc:(c + 1) * rc, :], w) for c in range(row_chunks)],
                        axis=0)
    pp = _dot(ap_scr[...], w)
    prev_row = jnp.where(first, 0.0, pp[PREV_ROWS - 1:, :])
    row = lax.broadcasted_iota(jnp.int32, p.shape, 0)
    shifted = jnp.where(row == 0, prev_row, pltpu.roll(p, 1, 0))
    return p * sb_ref[0:1, cols] + shifted * sb_ref[1:2, cols]


def _in_proj_kernel(x_ref, xp_ref, g_ref, w_ref, sb_ref, lnw_ref, lnb_ref,
                    ww_ref, wa_ref, wg_ref, vec_ref, hb_ref,
                    zu_ref, vn_ref, p_ref, lw_ref, kk_ref, aa_ref, gg_ref, a_scr, ap_scr, *,
                    tiles_per_seq, n_mix, key_tile, n_lora):
    d = x_ref.shape[1]
    n_main = p_ref.shape[1] // d
    first = (pl.program_id(0) % tiles_per_seq) == 0
    tile = lambda t: slice(t * d, (t + 1) * d)
    w0, a0, k_k, k_a = (vec_ref[r:r + 1, :] for r in range(4))

    ap_scr[...] = _rms_rows(xp_ref[...], g_ref[...]).astype(BF16)
    rc = x_ref.shape[0] // NORM_ROW_CHUNKS
    for c in range(NORM_ROW_CHUNKS):
        rows = pl.ds(c * rc, rc)
        a_scr[rows, :] = _rms_rows(x_ref[rows, :], g_ref[...]).astype(BF16)

    def pieces(t):
        return [(slice(s, s + MXU_WIDTH), slice(t * d + s, t * d + s + MXU_WIDTH))
                for s in range(0, d, MXU_WIDTH)]

    lo0 = (2 + n_main) * d
    lo = _shift_mix_tile(a_scr, ap_scr, w_ref, sb_ref, first, slice(lo0, lo0 + n_lora),
                         row_chunks=NORM_ROW_CHUNKS)
    xw = jnp.tanh(lo[:, 0:LANES]).astype(BF16)
    xa = lo[:, LANES:2 * LANES].astype(BF16)
    xg = jax.nn.sigmoid(lo[:, 2 * LANES:]).astype(BF16)
    for c, _ in pieces(0):
        zw = w0[:, c] + _dot(xw, ww_ref[:, c])
        lw_ref[:, c] = (-math.exp(-0.5)) * jax.nn.sigmoid(zw)
        gg_ref[:, c] = _dot(xg, wg_ref[:, c]).astype(gg_ref.dtype)

    def gelu_tile():
        for dst, src in pieces(0):
            zu_ref[:, dst] = _gelu(_dot(a_scr[...], w_ref[:, src])).astype(BF16)

    def layernorm_tile():
        zv = _gelu(_dot(a_scr[...], w_ref[:, tile(1)]))
        mu = jnp.mean(zv, axis=-1, keepdims=True)
        zc = zv - mu
        var = jnp.mean(zc * zc, axis=-1, keepdims=True)
        vn_ref[...] = ((zc * lax.rsqrt(var + LN_EPS)) * lnw_ref[...]
                       + lnb_ref[...]).astype(BF16)

    def main_tile(t):
        for dst, src in pieces(2 + t):
            if t < n_mix:
                p = _shift_mix_tile(a_scr, ap_scr, w_ref, sb_ref, first, src)
            else:
                p = _dot(a_scr[...], w_ref[:, src])
            if t == key_tile:
                aa = jax.nn.sigmoid(a0[:, dst] + _dot(xa, wa_ref[:, dst]))
                aa_ref[:, dst] = aa.astype(aa_ref.dtype)
                kraw = p * k_k[:, dst]
                ss = _dot_x2(kraw * kraw, hb_ref[...])
                kk_ref[:, dst] = (kraw * lax.rsqrt(jnp.maximum(ss, 1e-24))).astype(kk_ref.dtype)
                p = p * (1.0 + (aa - 1.0) * k_a[:, dst])
            p_ref[:, slice(t * d + dst.start, t * d + dst.stop)] = p.astype(p_ref.dtype)

    plain = list(range(n_mix, n_main))
    heavy = [gelu_tile, layernorm_tile] + [functools.partial(main_tile, t) for t in range(n_mix)]
    for step in range(max(len(plain), len(heavy))):
        if step < len(plain):
            main_tile(plain[step])
        if step < len(heavy):
            heavy[step]()


def _in_proj(x2, g, w16, sb, ln_w, ln_b, ww, wa, wg, vecs, hb, seq, tm, n_mix, key_tile, n_lora):
    m, d = x2.shape
    n = w16.shape[1]
    n_main = n // d - 3
    assert seq % tm == 0 and n % d == 0 and hb.shape[0] == MXU_WIDTH
    const = lambda a: pl.BlockSpec(a.shape, lambda i: (0, 0), pipeline_mode=pl.Buffered(1))
    rows = lambda w: pl.BlockSpec((tm, w), lambda i: (i, 0))
    act = lambda dt: jax.ShapeDtypeStruct((m, d), dt)
    return pl.pallas_call(
        functools.partial(_in_proj_kernel, tiles_per_seq=seq // tm, n_mix=n_mix,
                          key_tile=key_tile, n_lora=n_lora),
        grid=(m // tm,),
        in_specs=[
            rows(d),
            pl.BlockSpec((PREV_ROWS, d), lambda i: (jnp.maximum(i * (tm // PREV_ROWS) - 1, 0), 0)),
            const(g), const(w16), const(sb), const(ln_w), const(ln_b),
            const(ww), const(wa), const(wg), const(vecs), const(hb),
        ],
        out_specs=[rows(d), rows(d), rows(n_main * d), rows(d), rows(d), rows(d), rows(d)],
        out_shape=[act(BF16), act(BF16), jax.ShapeDtypeStruct((m, n_main * d), BF16),
                   act(F32), act(BF16), act(BF16), act(BF16)],
        scratch_shapes=[pltpu.VMEM((tm, d), BF16), pltpu.VMEM((PREV_ROWS, d), BF16)],
        compiler_params=_cparams(("parallel",)),
        name="in_proj",
    )(x2, x2, g, w16, sb, ln_w, ln_b, ww, wa, wg, vecs, hb)


def _sgu_tile(zu_ref, vn_ref, ga_ref, bias_ref, wpa_ref, wm_scr, s_scr):
    ch = SGU_CHUNK
    dg = LANES
    for c in range(zu_ref.shape[0] // ch):
        rows = pl.ds(c * ch, ch)
        for g in range(SGU_GROUPS):
            cols = slice(g * dg, (g + 1) * dg)
            sv = _dot(wm_scr[g], vn_ref[rows, cols]) + bias_ref[:, cols]
            s_scr[rows, cols] = (zu_ref[rows, cols] * sv).astype(BF16)
    return jax.nn.sigmoid(ga_ref[...].astype(F32)) * _dot(s_scr[...], wpa_ref[...])


def _scan_kernel(r_ref, k_ref, v_ref, lw_ref, kk_ref, aa_ref, o_ref, s_scr, *, chunk, hd, gw):
    c_len = chunk
    assert c_len == hd
    bb, tblk, d = r_ref.shape
    n_groups = d // gw
    hpg = gw // hd
    n_double = int(math.log2(c_len)) - 1

    rowi = lax.broadcasted_iota(jnp.int32, (c_len, gw), 0)
    sub = lax.broadcasted_iota(jnp.int32, (c_len, gw), 1) % c_len
    strict = sub < rowi
    incl = sub <= rowi
    eye = jnp.where(sub == rowi, 1.0, 0.0).astype(F32)
    bd_mask = (lax.broadcasted_iota(jnp.int32, (gw, gw), 0) // c_len
               == lax.broadcasted_iota(jnp.int32, (gw, gw), 1) // hd)
    ltri = jnp.where(lax.broadcasted_iota(jnp.int32, (c_len, c_len), 1)
                     <= lax.broadcasted_iota(jnp.int32, (c_len, c_len), 0), 1.0, 0.0).astype(BF16)

    @pl.when(pl.program_id(1) == 0)
    def _():
        s_scr[...] = jnp.zeros_like(s_scr)

    def bd(x16):
        t = jnp.concatenate([x16] * hpg, axis=0)
        return jnp.where(bd_mask, t, jnp.zeros_like(t))

    chains = [(b, g) for b in range(bb) for g in range(n_groups)]
    n = range(len(chains))

    def body(c, carry):
        rows = pl.ds(pl.multiple_of(c * c_len, c_len), c_len)

        def at(ref, i):
            b, g = chains[i]
            return ref.at[b, rows, g * gw:(g + 1) * gw]

        ar, bq, kq, v16, bk, g_tot = [], [], [], [], [], []
        for i in n:
            k = at(k_ref, i)[...].astype(F32)
            lw = at(lw_ref, i)[...]
            kk = at(kk_ref, i)[...].astype(F32)
            b = kk * at(aa_ref, i)[...].astype(F32)
            lh, lm, ll = _split3(lw)
            cum = _dot(ltri, lh) + (_dot(ltri, lm) + _dot(ltri, ll))
            cum_last = cum[c_len - 1:c_len, :]
            g_inv = jnp.exp(-cum)
            g_end = jnp.exp(cum_last - cum)
            aq = -(kk * jnp.exp(cum - lw))
            rq = at(r_ref, i)[...].astype(F32) * jnp.exp(cum)
            ar.append(jnp.concatenate([aq, rq], axis=0).astype(BF16))
            bq.append((b * g_inv).astype(BF16))
            kq.append((k * g_inv).astype(BF16))
            v16.append(at(v_ref, i)[...].astype(BF16))
            bk.append(jnp.concatenate([b * g_end, k * g_end], axis=0).astype(BF16))
            g_tot.append(jnp.exp(cum_last))

        pb = [_dot(ar[i], bd(bq[i]), _NT) for i in n]
        pk = [_dot(ar[i], bd(kq[i]), _NT) for i in n]
        s0 = [s_scr[chains[i]] for i in n]
        ar_s = [_dot(ar[i], s0[i].astype(BF16), _NT) for i in n]
        a_ab = [jnp.where(strict, pb[i][:c_len], 0.0) for i in n]
        a_rb = [jnp.where(incl, pb[i][c_len:], 0.0).astype(BF16) for i in n]
        akv = [_dot(jnp.concatenate([jnp.where(strict, pk[i][:c_len], 0.0),
                                     jnp.where(incl, pk[i][c_len:], 0.0)], axis=0).astype(BF16),
                    bd(v16[i])) for i in n]
        rhs = [ar_s[i][:c_len] + akv[i][:c_len] for i in n]

        p = [_dot(a_ab[i].astype(BF16), bd(a_ab[i].astype(BF16))) for i in n]
        x = [eye + a_ab[i] for i in n]
        for _ in range(n_double - 1):
            xp = [_dot(jnp.concatenate([x[i], p[i]], axis=0).astype(BF16), bd(p[i].astype(BF16)))
                  for i in n]
            x = [x[i] + xp[i][:c_len] for i in n]
            p = [xp[i][c_len:] for i in n]
        x = [x[i] + _dot(x[i].astype(BF16), bd(p[i].astype(BF16))) for i in n]

        sa16 = [_dot(x[i].astype(BF16), bd(rhs[i].astype(BF16))).astype(BF16) for i in n]
        for i in n:
            at(o_ref, i)[...] = ((ar_s[i][c_len:] + akv[i][c_len:])
                                 + _dot(a_rb[i], bd(sa16[i]))).astype(o_ref.dtype)
        for i in n:
            upd = _dot(jnp.concatenate([sa16[i], v16[i]], axis=0), bk[i], _TN)
            s_scr[chains[i]] = jnp.where(bd_mask, s0[i] * g_tot[i] + upd, 0.0)
        return carry

    lax.fori_loop(0, tblk // c_len, body, 0)


def _rwkv_scan(p_main, lw, kk, aa, batch, seq, d, bb, tblk):
    gw = SCAN_GROUP_HEADS * RWKV_HEAD
    spec = pl.BlockSpec((bb, tblk, d), lambda b, t: (b, t, 0))
    col = lambda cb: pl.BlockSpec((bb, tblk, d), lambda b, t: (b, t, cb))
    as3 = lambda a: a.reshape(batch, seq, a.shape[-1])
    out = pl.pallas_call(
        functools.partial(_scan_kernel, chunk=SCAN_CHUNK, hd=RWKV_HEAD, gw=gw),
        grid=(batch // bb, seq // tblk),
        in_specs=[col(0), col(1), col(2), spec, spec, spec],
        out_specs=spec,
        out_shape=jax.ShapeDtypeStruct((batch, seq, d), BF16),
        scratch_shapes=[pltpu.VMEM((bb, d // gw, gw, gw), F32)],
        compiler_params=_cparams(("parallel", "arbitrary")),
        name="rwkv_scan",
    )(as3(p_main), as3(p_main), as3(p_main), as3(lw), as3(kk), as3(aa))
    return out.reshape(batch * seq, d)


def _mix_kernel(zu_ref, vn_ref, ga_ref, o_ref, r_ref, k_ref, v_ref, g_ref, gb_ref, x_ref,
                ws_ref, bias_ref, vec_ref, hb_ref, wpa_ref, wpb_ref, wout_ref,
                h_ref, f_ref, wm_scr, s_scr, *, hd):
    @pl.when(pl.program_id(0) == 0)
    def _():
        ch = SGU_CHUNK
        row = lax.broadcasted_iota(jnp.int32, (ch, ch), 0)
        col = lax.broadcasted_iota(jnp.int32, (ch, ch), 1)
        for g in range(SGU_GROUPS):
            wm_scr[g] = jnp.where(col <= row, ws_ref[g], 0.0).astype(BF16)

    ya = _sgu_tile(zu_ref, vn_ref, ga_ref, bias_ref, wpa_ref, wm_scr, s_scr)

    lnx_w = vec_ref[0:1, :]
    lnx_b = vec_ref[1:2, :]
    r_k = vec_ref[2:3, :]
    g_ffn = vec_ref[3:4, :]
    o = o_ref[...].astype(F32)
    inv_n = 1.0 / hd
    mu = _head_sum(o, hb_ref) * inv_n
    oc = o - mu
    var = _head_sum(oc * oc, hb_ref) * inv_n
    on = (oc * lax.rsqrt(var + GN_EPS)) * lnx_w + lnx_b
    v = v_ref[...].astype(F32)
    rk = r_ref[...].astype(F32) * k_ref[...].astype(F32)
    bonus = _head_sum(rk * r_k, hb_ref) * v
    yb = _dot(((on + bonus) * g_ref[...]).astype(BF16), wpb_ref[...])
    mixed = ya + jax.nn.sigmoid(gb_ref[...].astype(F32)) * yb
    h = x_ref[...] + _dot(mixed.astype(BF16), wout_ref[...])
    h_ref[...] = h
    f_ref[...] = _rms_rows(h, g_ffn).astype(BF16)


def _mix(zu, vn, o, g, p_main, x2, sgu_w, bias_full, vecs, hb, wpa16, wpb16, wout16, tm, d):
    m = x2.shape[0]
    ch = SGU_CHUNK
    tile = pl.BlockSpec((tm, d), lambda i: (i, 0))
    col = lambda cb: pl.BlockSpec((tm, d), lambda i: (i, cb))
    const = lambda a: pl.BlockSpec(a.shape, lambda i: (0,) * a.ndim,
                                   pipeline_mode=pl.Buffered(1))
    return pl.pallas_call(
        functools.partial(_mix_kernel, hd=RWKV_HEAD),
        grid=(m // tm,),
        in_specs=[tile, tile, col(3), tile, col(0), col(1), col(2), tile, col(4), tile,
                  const(sgu_w), const(bias_full), const(vecs), const(hb),
                  const(wpa16), const(wpb16), const(wout16)],
        out_specs=[tile, tile],
        out_shape=[jax.ShapeDtypeStruct((m, d), F32), jax.ShapeDtypeStruct((m, d), BF16)],
        scratch_shapes=[pltpu.VMEM((SGU_GROUPS, ch, ch), BF16), pltpu.VMEM((tm, d), BF16)],
        compiler_params=_cparams(("arbitrary",)),
        name="mix",
    )(zu, vn, p_main, o, p_main, p_main, p_main, g, p_main, x2,
      sgu_w, bias_full, vecs, hb, wpa16, wpb16, wout16)


def _ffn_kernel(h_ref, f_ref, w1_ref, w2_ref, gf_ref, o_ref, *, tf):
    acc = h_ref[...]
    for j in range(w1_ref.shape[1] // tf):
        cols = slice(j * tf, (j + 1) * tf)
        t = jnp.maximum(_dot(f_ref[...], w1_ref[:, cols]), 0.0)
        acc = acc + _dot((t * t).astype(BF16), w2_ref[cols, :])
    o_ref[...] = _rms_rows(acc, gf_ref[...])


def _ffn(h, f, w1_16, w2_16, g_final, tm, tf):
    m, d = h.shape
    dff = w1_16.shape[1]
    assert dff % tf == 0
    tile = pl.BlockSpec((tm, d), lambda i: (i, 0))
    const = lambda shape: pl.BlockSpec(shape, lambda i: (0, 0), pipeline_mode=pl.Buffered(1))
    return pl.pallas_call(
        functools.partial(_ffn_kernel, tf=tf),
        grid=(m // tm,),
        in_specs=[tile, tile, const((d, dff)), const((dff, d)), const((1, d))],
        out_specs=tile,
        out_shape=jax.ShapeDtypeStruct((m, d), F32),
        compiler_params=_cparams(("parallel",)),
        name="ffn",
    )(h, f, w1_16, w2_16, g_final)


def _pad_cols(a, n):
    return jnp.pad(a, ((0, 0), (0, n - a.shape[1])))


def _pad_rows(a, n):
    return jnp.pad(a, ((0, n - a.shape[0]), (0, 0)))


def _layer(x2, batch, seq, g_mix, w_in, sgu_ln_w, sgu_ln_b, sgu_w, sgu_b, w_proj_a, shift_b,
           w_lora_w, w0, a_lora_w, a0, g_lora_w, k_k, k_a, r_k, ln_x_w, ln_x_b, w_proj_b,
           w_out, g_ffn, w_ffn1, w_ffn2, g_out):
    d = x2.shape[1]
    lora_w, lora_a, lora_g = w_lora_w.shape[0], a_lora_w.shape[0], g_lora_w.shape[0]
    c_sgu = 2 * d
    c_rkv = 3 * d
    c_lora = lora_w + lora_a + lora_g
    o_lora = c_sgu + c_rkv
    o_gate = o_lora + c_lora
    pw, pa = LANES, LANES
    pg = -(-lora_g // LANES) * LANES

    n_lora = pw + pa + pg
    w16 = w_in.astype(BF16)
    w_all = jnp.concatenate([
        w16[:, :o_lora], w16[:, o_gate:],
        _pad_cols(w16[:, o_lora:o_lora + lora_w], pw),
        _pad_cols(w16[:, o_lora + lora_w:o_lora + lora_w + lora_a], pa),
        _pad_cols(w16[:, o_lora + lora_w + lora_a:o_gate], pg + d - n_lora)], axis=1)
    sb_lo = shift_b[:, c_rkv:]
    sb_all = jnp.concatenate([
        jnp.zeros((2, c_sgu), F32), shift_b[:, :c_rkv], jnp.zeros((2, 2 * d), F32),
        _pad_cols(sb_lo[:, :lora_w], pw),
        _pad_cols(sb_lo[:, lora_w:lora_w + lora_a], pa),
        _pad_cols(sb_lo[:, lora_w + lora_a:], pg + d - n_lora)], axis=1)

    gw = SCAN_GROUP_HEADS * RWKV_HEAD
    hb = (lax.broadcasted_iota(jnp.int32, (gw, gw), 0) // RWKV_HEAD
          == lax.broadcasted_iota(jnp.int32, (gw, gw), 1) // RWKV_HEAD).astype(BF16)
    vec_prep = jnp.stack([w0, a0, k_k, k_a])
    lo16 = lambda w, rows: _pad_rows(w, rows).astype(BF16)
    zu, vn, p_main, lw, kk, aa, g = _in_proj(
        x2, g_mix.reshape(1, d), w_all, sb_all, sgu_ln_w.reshape(1, d), sgu_ln_b.reshape(1, d),
        lo16(w_lora_w, pw), lo16(a_lora_w, pa), lo16(g_lora_w, pg), vec_prep, hb, seq,
        tm=512, n_mix=c_rkv // d, key_tile=1, n_lora=n_lora)

    bias_full = jnp.repeat(sgu_b.T, d // SGU_GROUPS, axis=1)

    o = _rwkv_scan(p_main, lw, kk, aa, batch, seq, d, bb=4, tblk=256)

    vec_post = jnp.stack([ln_x_w, ln_x_b, r_k, g_ffn])
    h1, f = _mix(zu, vn, o, g, p_main, x2, sgu_w, bias_full, vec_post, hb,
                 w_proj_a.astype(BF16), w_proj_b.astype(BF16), w_out.astype(BF16), tm=512, d=d)
    return _ffn(h1, f, w_ffn1.astype(BF16), w_ffn2.astype(BF16), g_out.reshape(1, d),
                tm=1024, tf=1024)


def kernel(x, g_mix, w_in, sgu_ln_w, sgu_ln_b, sgu_w, sgu_b, w_proj_a, shift_b, w_lora_w, w0,
           a_lora_w, a0, g_lora_w, k_k, k_a, r_k, ln_x_w, ln_x_b, w_proj_b, w_out, g_ffn,
           w_ffn1, w_ffn2, g_final):
    batch, seq, d = x.shape
    depth = w_in.shape[0]
    assert depth == 1, "the final RMSNorm is fused into the single layer's ffn call"
    h = x.reshape(batch * seq, d)
    l = 0
    h = _layer(h, batch, seq, g_mix[l], w_in[l], sgu_ln_w[l], sgu_ln_b[l], sgu_w[l], sgu_b[l],
               w_proj_a[l], shift_b[l], w_lora_w[l], w0[l], a_lora_w[l], a0[l], g_lora_w[l],
               k_k[l], k_a[l], r_k[l], ln_x_w[l], ln_x_b[l], w_proj_b[l], w_out[l], g_ffn[l],
               w_ffn1[l], w_ffn2[l], g_final)
    return h.reshape(batch, seq, d)
```

```python
import functools
import math

import jax
import jax.numpy as jnp
from jax import lax
from jax.experimental import pallas as pl
from jax.experimental.pallas import tpu as pltpu

F32 = jnp.float32
BF16 = jnp.bfloat16

SGU_CHUNK = 128
SGU_GROUPS = 8
RWKV_HEAD = 64
NORM_EPS = 1e-6
LN_EPS = 1e-5
GN_EPS = 64e-5

LANES = 128
MXU_WIDTH = 256
PREV_ROWS = 16
NORM_ROW_CHUNKS = 4
SCAN_CHUNK = 64
SCAN_GROUP_HEADS = MXU_WIDTH // RWKV_HEAD
VMEM_LIMIT = 56 * 1024 * 1024

IN_PROJ_ROWS = 512
SCAN_SEQS = 4
SCAN_ROWS = 256
MIX_ROWS = 512
FFN_ROWS = 1024
FFN_COLS = 1024


def _cparams(sem):
    return pltpu.CompilerParams(dimension_semantics=sem, vmem_limit_bytes=VMEM_LIMIT)


def _dot(a, b, dims=(((1,), (0,)), ((), ()))):
    return lax.dot_general(a, b, dims, preferred_element_type=F32)


_NT = (((1,), (1,)), ((), ()))
_TN = (((0,), (0,)), ((), ()))


def _split2(x):
    hi = x.astype(BF16)
    lo = (x - hi.astype(F32)).astype(BF16)
    return hi, lo


def _split3(x):
    hi = x.astype(BF16)
    r1 = x - hi.astype(F32)
    mid = r1.astype(BF16)
    lo = (r1 - mid.astype(F32)).astype(BF16)
    return hi, mid, lo


def _dot_x2(x, w_exact):
    hi, lo = _split2(x)
    return _dot(hi, w_exact) + _dot(lo, w_exact)


def _head_sum(x, hb_ref):
    w = hb_ref.shape[0]
    hb = hb_ref[...]
    return jnp.concatenate(
        [_dot(x[:, c * w:(c + 1) * w].astype(BF16), hb) for c in range(x.shape[1] // w)], axis=1)


def _rms_rows(x, g):
    ms = jnp.mean(x * x, axis=-1, keepdims=True)
    return (x * lax.rsqrt(ms + NORM_EPS)) * g


def _gelu(x):
    return 0.5 * x * (1.0 + lax.erf(x * (1.0 / math.sqrt(2.0))))


def _shift_mix_tile(a_scr, ap_scr, w_ref, sb_ref, first, cols, row_chunks=1):
    w = w_ref[:, cols]
    rc = a_scr.shape[0] // row_chunks
    p = jnp.concatenate([_dot(a_scr[c * rc:(c + 1) * rc, :], w) for c in range(row_chunks)],
                        axis=0)
    pp = _dot(ap_scr[...], w)
    prev_row = jnp.where(first, 0.0, pp[PREV_ROWS - 1:, :])
    row = lax.broadcasted_iota(jnp.int32, p.shape, 0)
    shifted = jnp.where(row == 0, prev_row, pltpu.roll(p, 1, 0))
    return p * sb_ref[0:1, cols] + shifted * sb_ref[1:2, cols]


def _in_proj_kernel(x_ref, xp_ref, g_ref, w_ref, sb_ref, lnw_ref, lnb_ref,
                    ww_ref, wa_ref, wg_ref, vec_ref, hb_ref,
                    zu_ref, vn_ref, p_ref, lw_ref, kk_ref, aa_ref, gg_ref, a_scr, ap_scr, *,
                    tiles_per_seq, n_mix, key_tile, n_lora):
    d = x_ref.shape[1]
    n_main = p_ref.shape[1] // d
    first = (pl.program_id(0) % tiles_per_seq) == 0
    tile = lambda t: slice(t * d, (t + 1) * d)
    w0, a0, k_k, k_a = (vec_ref[r:r + 1, :] for r in range(4))

    ap_scr[...] = _rms_rows(xp_ref[...], g_ref[...]).astype(BF16)
    rc = x_ref.shape[0] // NORM_ROW_CHUNKS
    for c in range(NORM_ROW_CHUNKS):
        rows = pl.ds(c * rc, rc)
        a_scr[rows, :] = _rms_rows(x_ref[rows, :], g_ref[...]).astype(BF16)

    def pieces(t):
        return [(slice(s, s + MXU_WIDTH), slice(t * d + s, t * d + s + MXU_WIDTH))
                for s in range(0, d, MXU_WIDTH)]

    lo0 = (2 + n_main) * d
    lo = _shift_mix_tile(a_scr, ap_scr, w_ref, sb_ref, first, slice(lo0, lo0 + n_lora),
                         row_chunks=NORM_ROW_CHUNKS)
    xw = jnp.tanh(lo[:, 0:LANES]).astype(BF16)
    xa = lo[:, LANES:2 * LANES].astype(BF16)
    xg = jax.nn.sigmoid(lo[:, 2 * LANES:]).astype(BF16)
    for c, _ in pieces(0):
        zw = w0[:, c] + _dot(xw, ww_ref[:, c])
        lw_ref[:, c] = (-math.exp(-0.5)) * jax.nn.sigmoid(zw)
        gg_ref[:, c] = _dot(xg, wg_ref[:, c]).astype(gg_ref.dtype)

    def gelu_tile():
        for dst, src in pieces(0):
            zu_ref[:, dst] = _gelu(_dot(a_scr[...], w_ref[:, src])).astype(BF16)

    def layernorm_tile():
        zv = _gelu(_dot(a_scr[...], w_ref[:, tile(1)]))
        mu = jnp.mean(zv, axis=-1, keepdims=True)
        zc = zv - mu
        var = jnp.mean(zc * zc, axis=-1, keepdims=True)
        vn_ref[...] = ((zc * lax.rsqrt(var + LN_EPS)) * lnw_ref[...]
                       + lnb_ref[...]).astype(BF16)

    def main_tile(t):
        for dst, src in pieces(2 + t):
            if t < n_mix:
                p = _shift_mix_tile(a_scr, ap_scr, w_ref, sb_ref, first, src)
            else:
                p = _dot(a_scr[...], w_ref[:, src])
            if t == key_tile:
                aa = jax.nn.sigmoid(a0[:, dst] + _dot(xa, wa_ref[:, dst]))
                aa_ref[:, dst] = aa.astype(aa_ref.dtype)
                kraw = p * k_k[:, dst]
                ss = _dot_x2(kraw * kraw, hb_ref[...])
                kk_ref[:, dst] = (kraw * lax.rsqrt(jnp.maximum(ss, 1e-24))).astype(kk_ref.dtype)
                p = p * (1.0 + (aa - 1.0) * k_a[:, dst])
            p_ref[:, slice(t * d + dst.start, t * d + dst.stop)] = p.astype(p_ref.dtype)

    plain = list(range(n_mix, n_main))
    heavy = [gelu_tile, layernorm_tile] + [functools.partial(main_tile, t) for t in range(n_mix)]
    for step in range(max(len(plain), len(heavy))):
        if step < len(plain):
            main_tile(plain[step])
        if step < len(heavy):
            heavy[step]()


def _in_proj(x2, g, w16, sb, ln_w, ln_b, ww, wa, wg, vecs, hb, seq, tm, n_mix, key_tile, n_lora):
    m, d = x2.shape
    n = w16.shape[1]
    n_main = n // d - 3
    assert seq % tm == 0 and n % d == 0 and hb.shape[0] == MXU_WIDTH
    const = lambda a: pl.BlockSpec(a.shape, lambda i: (0, 0), pipeline_mode=pl.Buffered(1))
    rows = lambda w: pl.BlockSpec((tm, w), lambda i: (i, 0))
    act = lambda dt: jax.ShapeDtypeStruct((m, d), dt)
    return pl.pallas_call(
        functools.partial(_in_proj_kernel, tiles_per_seq=seq // tm, n_mix=n_mix,
                          key_tile=key_tile, n_lora=n_lora),
        grid=(m // tm,),
        in_specs=[
            rows(d),
            pl.BlockSpec((PREV_ROWS, d), lambda i: (jnp.maximum(i * (tm // PREV_ROWS) - 1, 0), 0)),
            const(g), const(w16), const(sb), const(ln_w), const(ln_b),
            const(ww), const(wa), const(wg), const(vecs), const(hb),
        ],
        out_specs=[rows(d), rows(d), rows(n_main * d), rows(d), rows(d), rows(d), rows(d)],
        out_shape=[act(BF16), act(BF16), jax.ShapeDtypeStruct((m, n_main * d), BF16),
                   act(F32), act(BF16), act(BF16), act(BF16)],
        scratch_shapes=[pltpu.VMEM((tm, d), BF16), pltpu.VMEM((PREV_ROWS, d), BF16)],
        compiler_params=_cparams(("parallel",)),
        name="in_proj",
    )(x2, x2, g, w16, sb, ln_w, ln_b, ww, wa, wg, vecs, hb)


def _sgu_tile(zu_ref, vn_ref, ga_ref, bias_ref, wpa_ref, wm_scr, s_scr):
    ch = SGU_CHUNK
    dg = LANES
    for c in range(zu_ref.shape[0] // ch):
        rows = pl.ds(c * ch, ch)
        for g in range(SGU_GROUPS):
            cols = slice(g * dg, (g + 1) * dg)
            sv = _dot(wm_scr[g], vn_ref[rows, cols]) + bias_ref[:, cols]
            s_scr[rows, cols] = (zu_ref[rows, cols] * sv).astype(BF16)
    return jax.nn.sigmoid(ga_ref[...].astype(F32)) * _dot(s_scr[...], wpa_ref[...])


def _scan_kernel(r_ref, k_ref, v_ref, lw_ref, kk_ref, aa_ref, o_ref, s_scr, *, chunk, hd, gw):
    c_len = chunk
    assert c_len == hd
    bb, tblk, d = r_ref.shape
    n_groups = d // gw
    hpg = gw // hd
    n_double = int(math.log2(c_len)) - 1

    rowi = lax.broadcasted_iota(jnp.int32, (c_len, gw), 0)
    sub = lax.broadcasted_iota(jnp.int32, (c_len, gw), 1) % c_len
    strict = sub < rowi
    incl = sub <= rowi
    eye = jnp.where(sub == rowi, 1.0, 0.0).astype(F32)
    bd_mask = (lax.broadcasted_iota(jnp.int32, (gw, gw), 0) // c_len
               == lax.broadcasted_iota(jnp.int32, (gw, gw), 1) // hd)
    ltri = jnp.where(lax.broadcasted_iota(jnp.int32, (c_len, c_len), 1)
                     <= lax.broadcasted_iota(jnp.int32, (c_len, c_len), 0), 1.0, 0.0).astype(BF16)

    @pl.when(pl.program_id(1) == 0)
    def _():
        s_scr[...] = jnp.zeros_like(s_scr)

    def bd(x16):
        t = jnp.concatenate([x16] * hpg, axis=0)
        return jnp.where(bd_mask, t, jnp.zeros_like(t))

    chains = [(b, g) for b in range(bb) for g in range(n_groups)]
    n = range(len(chains))

    def body(c, carry):
        rows = pl.ds(pl.multiple_of(c * c_len, c_len), c_len)

        def at(ref, i):
            b, g = chains[i]
            return ref.at[b, rows, g * gw:(g + 1) * gw]

        ar, bq, kq, v16, bk, g_tot = [], [], [], [], [], []
        for i in n:
            k = at(k_ref, i)[...].astype(F32)
            lw = at(lw_ref, i)[...]
            kk = at(kk_ref, i)[...].astype(F32)
            b = kk * at(aa_ref, i)[...].astype(F32)
            lh, lm, ll = _split3(lw)
            cum = _dot(ltri, lh) + (_dot(ltri, lm) + _dot(ltri, ll))
            cum_last = cum[c_len - 1:c_len, :]
            g_inv = jnp.exp(-cum)
            g_end = jnp.exp(cum_last - cum)
            aq = -(kk * jnp.exp(cum - lw))
            rq = at(r_ref, i)[...].astype(F32) * jnp.exp(cum)
            ar.append(jnp.concatenate([aq, rq], axis=0).astype(BF16))
            bq.append((b * g_inv).astype(BF16))
            kq.append((k * g_inv).astype(BF16))
            v16.append(at(v_ref, i)[...].astype(BF16))
            bk.append(jnp.concatenate([b * g_end, k * g_end], axis=0).astype(BF16))
            g_tot.append(jnp.exp(cum_last))

        pb = [_dot(ar[i], bd(bq[i]), _NT) for i in n]
        pk = [_dot(ar[i], bd(kq[i]), _NT) for i in n]
        s0 = [s_scr[chains[i]] for i in n]
        ar_s = [_dot(ar[i], s0[i].astype(BF16), _NT) for i in n]
        a_ab = [jnp.where(strict, pb[i][:c_len], 0.0) for i in n]
        a_rb = [jnp.where(incl, pb[i][c_len:], 0.0).astype(BF16) for i in n]
        akv = [_dot(jnp.concatenate([jnp.where(strict, pk[i][:c_len], 0.0),
                                     jnp.where(incl, pk[i][c_len:], 0.0)], axis=0).astype(BF16),
                    bd(v16[i])) for i in n]
        rhs = [ar_s[i][:c_len] + akv[i][:c_len] for i in n]

        p = [_dot(a_ab[i].astype(BF16), bd(a_ab[i].astype(BF16))) for i in n]
        x = [eye + a_ab[i] for i in n]
        for _ in range(n_double - 1):
            xp = [_dot(jnp.concatenate([x[i], p[i]], axis=0).astype(BF16), bd(p[i].astype(BF16)))
                  for i in n]
            x = [x[i] + xp[i][:c_len] for i in n]
            p = [xp[i][c_len:] for i in n]
        x = [x[i] + _dot(x[i].astype(BF16), bd(p[i].astype(BF16))) for i in n]

        sa16 = [_dot(x[i].astype(BF16), bd(rhs[i].astype(BF16))).astype(BF16) for i in n]
        for i in n:
            at(o_ref, i)[...] = ((ar_s[i][c_len:] + akv[i][c_len:])
                                 + _dot(a_rb[i], bd(sa16[i]))).astype(o_ref.dtype)
        for i in n:
            upd = _dot(jnp.concatenate([sa16[i], v16[i]], axis=0), bk[i], _TN)
            s_scr[chains[i]] = jnp.where(bd_mask, s0[i] * g_tot[i] + upd, 0.0)
        return carry

    lax.fori_loop(0, tblk // c_len, body, 0)


def _rwkv_scan(p_main, lw, kk, aa, batch, seq, d, bb, tblk):
    gw = SCAN_GROUP_HEADS * RWKV_HEAD
    spec = pl.BlockSpec((bb, tblk, d), lambda b, t: (b, t, 0))
    col = lambda cb: pl.BlockSpec((bb, tblk, d), lambda b, t: (b, t, cb))
    as3 = lambda a: a.reshape(batch, seq, a.shape[-1])
    out = pl.pallas_call(
        functools.partial(_scan_kernel, chunk=SCAN_CHUNK, hd=RWKV_HEAD, gw=gw),
        grid=(batch // bb, seq // tblk),
        in_specs=[col(0), col(1), col(2), spec, spec, spec],
        out_specs=spec,
        out_shape=jax.ShapeDtypeStruct((batch, seq, d), BF16),
        scratch_shapes=[pltpu.VMEM((bb, d // gw, gw, gw), F32)],
        compiler_params=_cparams(("parallel", "arbitrary")),
        name="rwkv_scan",
    )(as3(p_main), as3(p_main), as3(p_main), as3(lw), as3(kk), as3(aa))
    return out.reshape(batch * seq, d)


def _mix_kernel(zu_ref, vn_ref, ga_ref, o_ref, r_ref, k_ref, v_ref, g_ref, gb_ref, x_ref,
                ws_ref, bias_ref, vec_ref, hb_ref, wpa_ref, wpb_ref, wout_ref,
                h_ref, f_ref, wm_scr, s_scr, *, hd):
    @pl.when(pl.program_id(0) == 0)
    def _():
        ch = SGU_CHUNK
        row = lax.broadcasted_iota(jnp.int32, (ch, ch), 0)
        col = lax.broadcasted_iota(jnp.int32, (ch, ch), 1)
        for g in range(SGU_GROUPS):
            wm_scr[g] = jnp.where(col <= row, ws_ref[g], 0.0).astype(BF16)

    ya = _sgu_tile(zu_ref, vn_ref, ga_ref, bias_ref, wpa_ref, wm_scr, s_scr)

    lnx_w = vec_ref[0:1, :]
    lnx_b = vec_ref[1:2, :]
    r_k = vec_ref[2:3, :]
    g_ffn = vec_ref[3:4, :]
    o = o_ref[...].astype(F32)
    inv_n = 1.0 / hd
    mu = _head_sum(o, hb_ref) * inv_n
    oc = o - mu
    var = _head_sum(oc * oc, hb_ref) * inv_n
    on = (oc * lax.rsqrt(var + GN_EPS)) * lnx_w + lnx_b
    v = v_ref[...].astype(F32)
    rk = r_ref[...].astype(F32) * k_ref[...].astype(F32)
    bonus = _head_sum(rk * r_k, hb_ref) * v
    yb = _dot(((on + bonus) * g_ref[...]).astype(BF16), wpb_ref[...])
    mixed = ya + jax.nn.sigmoid(gb_ref[...].astype(F32)) * yb
    h = x_ref[...] + _dot(mixed.astype(BF16), wout_ref[...])
    h_ref[...] = h
    f_ref[...] = _rms_rows(h, g_ffn).astype(BF16)


def _mix(zu, vn, o, g, p_main, x2, sgu_w, bias_full, vecs, hb, wpa16, wpb16, wout16, tm, d):
    m = x2.shape[0]
    ch = SGU_CHUNK
    tile = pl.BlockSpec((tm, d), lambda i: (i, 0))
    col = lambda cb: pl.BlockSpec((tm, d), lambda i: (i, cb))
    const = lambda a: pl.BlockSpec(a.shape, lambda i: (0,) * a.ndim,
                                   pipeline_mode=pl.Buffered(1))
    return pl.pallas_call(
        functools.partial(_mix_kernel, hd=RWKV_HEAD),
        grid=(m // tm,),
        in_specs=[tile, tile, col(3), tile, col(0), col(1), col(2), tile, col(4), tile,
                  const(sgu_w), const(bias_full), const(vecs), const(hb),
                  const(wpa16), const(wpb16), const(wout16)],
        out_specs=[tile, tile],
        out_shape=[jax.ShapeDtypeStruct((m, d), F32), jax.ShapeDtypeStruct((m, d), BF16)],
        scratch_shapes=[pltpu.VMEM((SGU_GROUPS, ch, ch), BF16), pltpu.VMEM((tm, d), BF16)],
        compiler_params=_cparams(("arbitrary",)),
        name="mix",
    )(zu, vn, p_main, o, p_main, p_main, p_main, g, p_main, x2,
      sgu_w, bias_full, vecs, hb, wpa16, wpb16, wout16)


def _ffn_kernel(h_ref, f_ref, w1_ref, w2_ref, gf_ref, o_ref, *, tf):
    acc = h_ref[...]
    for j in range(w1_ref.shape[1] // tf):
        cols = slice(j * tf, (j + 1) * tf)
        t = jnp.maximum(_dot(f_ref[...], w1_ref[:, cols]), 0.0)
        acc = acc + _dot((t * t).astype(BF16), w2_ref[cols, :])
    o_ref[...] = _rms_rows(acc, gf_ref[...])


def _ffn(h, f, w1_16, w2_16, g_final, tm, tf):
    m, d = h.shape
    dff = w1_16.shape[1]
    assert dff % tf == 0
    tile = pl.BlockSpec((tm, d), lambda i: (i, 0))
    const = lambda shape: pl.BlockSpec(shape, lambda i: (0, 0), pipeline_mode=pl.Buffered(1))
    return pl.pallas_call(
        functools.partial(_ffn_kernel, tf=tf),
        grid=(m // tm,),
        in_specs=[tile, tile, const((d, dff)), const((dff, d)), const((1, d))],
        out_specs=tile,
        out_shape=jax.ShapeDtypeStruct((m, d), F32),
        compiler_params=_cparams(("parallel",)),
        name="ffn",
    )(h, f, w1_16, w2_16, g_final)


def _pad_cols(a, n):
    return jnp.pad(a, ((0, 0), (0, n - a.shape[1])))


def _pad_rows(a, n):
    return jnp.pad(a, ((0, n - a.shape[0]), (0, 0)))


def _layer(x2, batch, seq, g_mix, w_in, sgu_ln_w, sgu_ln_b, sgu_w, sgu_b, w_proj_a, shift_b,
           w_lora_w, w0, a_lora_w, a0, g_lora_w, k_k, k_a, r_k, ln_x_w, ln_x_b, w_proj_b,
           w_out, g_ffn, w_ffn1, w_ffn2, g_out):
    d = x2.shape[1]
    lora_w, lora_a, lora_g = w_lora_w.shape[0], a_lora_w.shape[0], g_lora_w.shape[0]
    c_sgu = 2 * d
    c_rkv = 3 * d
    c_lora = lora_w + lora_a + lora_g
    o_lora = c_sgu + c_rkv
    o_gate = o_lora + c_lora
    pw, pa = LANES, LANES
    pg = -(-lora_g // LANES) * LANES

    n_lora = pw + pa + pg
    w16 = w_in.astype(BF16)
    w_all = jnp.concatenate([
        w16[:, :o_lora], w16[:, o_gate:],
        _pad_cols(w16[:, o_lora:o_lora + lora_w], pw),
        _pad_cols(w16[:, o_lora + lora_w:o_lora + lora_w + lora_a], pa),
        _pad_cols(w16[:, o_lora + lora_w + lora_a:o_gate], pg + d - n_lora)], axis=1)
    sb_lo = shift_b[:, c_rkv:]
    sb_all = jnp.concatenate([
        jnp.zeros((2, c_sgu), F32), shift_b[:, :c_rkv], jnp.zeros((2, 2 * d), F32),
        _pad_cols(sb_lo[:, :lora_w], pw),
        _pad_cols(sb_lo[:, lora_w:lora_w + lora_a], pa),
        _pad_cols(sb_lo[:, lora_w + lora_a:], pg + d - n_lora)], axis=1)

    gw = SCAN_GROUP_HEADS * RWKV_HEAD
    hb = (lax.broadcasted_iota(jnp.int32, (gw, gw), 0) // RWKV_HEAD
          == lax.broadcasted_iota(jnp.int32, (gw, gw), 1) // RWKV_HEAD).astype(BF16)
    vec_prep = jnp.stack([w0, a0, k_k, k_a])
    lo16 = lambda w, rows: _pad_rows(w, rows).astype(BF16)
    zu, vn, p_main, lw, kk, aa, g = _in_proj(
        x2, g_mix.reshape(1, d), w_all, sb_all, sgu_ln_w.reshape(1, d), sgu_ln_b.reshape(1, d),
        lo16(w_lora_w, pw), lo16(a_lora_w, pa), lo16(g_lora_w, pg), vec_prep, hb, seq,
        tm=IN_PROJ_ROWS, n_mix=c_rkv // d, key_tile=1, n_lora=n_lora)

    bias_full = jnp.repeat(sgu_b.T, d // SGU_GROUPS, axis=1)

    o = _rwkv_scan(p_main, lw, kk, aa, batch, seq, d, bb=SCAN_SEQS, tblk=SCAN_ROWS)

    vec_post = jnp.stack([ln_x_w, ln_x_b, r_k, g_ffn])
    h1, f = _mix(zu, vn, o, g, p_main, x2, sgu_w, bias_full, vec_post, hb,
                 w_proj_a.astype(BF16), w_proj_b.astype(BF16), w_out.astype(BF16),
                 tm=MIX_ROWS, d=d)
    return _ffn(h1, f, w_ffn1.astype(BF16), w_ffn2.astype(BF16), g_out.reshape(1, d),
                tm=FFN_ROWS, tf=FFN_COLS)


def kernel(x, g_mix, w_in, sgu_ln_w, sgu_ln_b, sgu_w, sgu_b, w_proj_a, shift_b, w_lora_w, w0,
           a_lora_w, a0, g_lora_w, k_k, k_a, r_k, ln_x_w, ln_x_b, w_proj_b, w_out, g_ffn,
           w_ffn1, w_ffn2, g_final):
    batch, seq, d = x.shape
    depth = w_in.shape[0]
    assert depth == 1, "the final RMSNorm is fused into the single layer's ffn call"
    h = x.reshape(batch * seq, d)
    l = 0
    h = _layer(h, batch, seq, g_mix[l], w_in[l], sgu_ln_w[l], sgu_ln_b[l], sgu_w[l], sgu_b[l],
               w_proj_a[l], shift_b[l], w_lora_w[l], w0[l], a_lora_w[l], a0[l], g_lora_w[l],
               k_k[l], k_a[l], r_k[l], ln_x_w[l], ln_x_b[l], w_proj_b[l], w_out[l], g_ffn[l],
               w_ffn1[l], w_ffn2[l], g_final)
    return h.reshape(batch, seq, d)
```

```python
import functools
import math

import jax
import jax.numpy as jnp
from jax import lax
from jax.experimental import pallas as pl
from jax.experimental.pallas import tpu as pltpu

F32 = jnp.float32
BF16 = jnp.bfloat16

SGU_CHUNK = 128
SGU_GROUPS = 8
RWKV_HEAD = 64
NORM_EPS = 1e-6
LN_EPS = 1e-5
GN_EPS = 64e-5

LANES = 128
MXU_WIDTH = 256
PREV_ROWS = 16
NORM_ROW_CHUNKS = 4
SCAN_CHUNK = 64
SCAN_GROUP_HEADS = MXU_WIDTH // RWKV_HEAD
VMEM_LIMIT = 56 * 1024 * 1024

IN_PROJ_ROWS = 512
SCAN_SEQS = 4
SCAN_ROWS = 256
MIX_ROWS = 512
FFN_ROWS = 1024
FFN_COLS = 1024


def _cparams(sem):
    return pltpu.CompilerParams(dimension_semantics=sem, vmem_limit_bytes=VMEM_LIMIT)


def _dot(a, b, dims=(((1,), (0,)), ((), ()))):
    return lax.dot_general(a, b, dims, preferred_element_type=F32)


_NT = (((1,), (1,)), ((), ()))
_TN = (((0,), (0,)), ((), ()))


def _split3(x):
    hi = x.astype(BF16)
    r1 = x - hi.astype(F32)
    mid = r1.astype(BF16)
    lo = (r1 - mid.astype(F32)).astype(BF16)
    return hi, mid, lo


def _head_sum(x, hb_ref):
    w = hb_ref.shape[0]
    hb = hb_ref[...]
    return jnp.concatenate(
        [_dot(x[:, c * w:(c + 1) * w].astype(BF16), hb) for c in range(x.shape[1] // w)], axis=1)


def _rms_rows(x, g):
    ms = jnp.mean(x * x, axis=-1, keepdims=True)
    return (x * lax.rsqrt(ms + NORM_EPS)) * g


def _gelu(x):
    return 0.5 * x * (1.0 + lax.erf(x * (1.0 / math.sqrt(2.0))))


def _shift_mix_tile(a_scr, ap_scr, w_ref, sb_ref, first, cols, row_chunks=1):
    w = w_ref[:, cols]
    rc = a_scr.shape[0] // row_chunks
    p = jnp.concatenate([_dot(a_scr[c * rc:(c + 1) * rc, :], w) for c in range(row_chunks)],
                        axis=0)
    pp = _dot(ap_scr[...], w)
    prev_row = jnp.where(first, 0.0, pp[PREV_ROWS - 1:, :])
    row = lax.broadcasted_iota(jnp.int32, p.shape, 0)
    shifted = jnp.where(row == 0, prev_row, pltpu.roll(p, 1, 0))
    return p * sb_ref[0:1, cols] + shifted * sb_ref[1:2, cols]


def _in_proj_kernel(x_ref, xp_ref, g_ref, w_ref, sb_ref, lnw_ref, lnb_ref,
                    ww_ref, wa_ref, wg_ref, vec_ref, hb_ref,
                    zu_ref, vn_ref, p_ref, lw_ref, kk_ref, aa_ref, gg_ref, a_scr, ap_scr, *,
                    tiles_per_seq, n_mix, key_tile, n_lora):
    d = x_ref.shape[1]
    n_main = p_ref.shape[1] // d
    first = (pl.program_id(0) % tiles_per_seq) == 0
    tile = lambda t: slice(t * d, (t + 1) * d)
    w0, a0, k_k, k_a = (vec_ref[r:r + 1, :] for r in range(4))

    ap_scr[...] = _rms_rows(xp_ref[...], g_ref[...]).astype(BF16)
    rc = x_ref.shape[0] // NORM_ROW_CHUNKS
    for c in range(NORM_ROW_CHUNKS):
        rows = pl.ds(c * rc, rc)
        a_scr[rows, :] = _rms_rows(x_ref[rows, :], g_ref[...]).astype(BF16)

    def pieces(t):
        return [(slice(s, s + MXU_WIDTH), slice(t * d + s, t * d + s + MXU_WIDTH))
                for s in range(0, d, MXU_WIDTH)]

    lo0 = (2 + n_main) * d
    lo = _shift_mix_tile(a_scr, ap_scr, w_ref, sb_ref, first, slice(lo0, lo0 + n_lora),
                         row_chunks=NORM_ROW_CHUNKS)
    xw = jnp.tanh(lo[:, 0:LANES]).astype(BF16)
    xa = lo[:, LANES:2 * LANES].astype(BF16)
    xg = jax.nn.sigmoid(lo[:, 2 * LANES:]).astype(BF16)
    for c, _ in pieces(0):
        zw = w0[:, c] + _dot(xw, ww_ref[:, c])
        lw_ref[:, c] = (-math.exp(-0.5)) * jax.nn.sigmoid(zw)
        gg_ref[:, c] = _dot(xg, wg_ref[:, c]).astype(gg_ref.dtype)

    def gelu_tile():
        for dst, src in pieces(0):
            zu_ref[:, dst] = _gelu(_dot(a_scr[...], w_ref[:, src])).astype(BF16)

    def layernorm_tile():
        zv = _gelu(_dot(a_scr[...], w_ref[:, tile(1)]))
        mu = jnp.mean(zv, axis=-1, keepdims=True)
        zc = zv - mu
        var = jnp.mean(zc * zc, axis=-1, keepdims=True)
        vn_ref[...] = ((zc * lax.rsqrt(var + LN_EPS)) * lnw_ref[...]
                       + lnb_ref[...]).astype(BF16)

    def main_tile(t):
        for dst, src in pieces(2 + t):
            if t < n_mix:
                p = _shift_mix_tile(a_scr, ap_scr, w_ref, sb_ref, first, src)
            else:
                p = _dot(a_scr[...], w_ref[:, src])
            if t == key_tile:
                aa = jax.nn.sigmoid(a0[:, dst] + _dot(xa, wa_ref[:, dst]))
                aa_ref[:, dst] = aa.astype(aa_ref.dtype)
                kraw = p * k_k[:, dst]
                ss = _dot((kraw * kraw).astype(BF16), hb_ref[...])
                kk_ref[:, dst] = (kraw * lax.rsqrt(jnp.maximum(ss, 1e-24))).astype(kk_ref.dtype)
                p = p * ((1.0 - k_a[:, dst]) + aa * k_a[:, dst])
            p_ref[:, slice(t * d + dst.start, t * d + dst.stop)] = p.astype(p_ref.dtype)

    plain = list(range(n_mix, n_main))
    heavy = [gelu_tile, layernorm_tile] + [functools.partial(main_tile, t) for t in range(n_mix)]
    for step in range(max(len(plain), len(heavy))):
        if step < len(plain):
            main_tile(plain[step])
        if step < len(heavy):
            heavy[step]()


def _in_proj(x2, g, w16, sb, ln_w, ln_b, ww, wa, wg, vecs, hb, seq, tm, n_mix, key_tile, n_lora):
    m, d = x2.shape
    n = w16.shape[1]
    n_main = n // d - 3
    assert seq % tm == 0 and n % d == 0 and hb.shape[0] == MXU_WIDTH
    const = lambda a: pl.BlockSpec(a.shape, lambda i: (0, 0), pipeline_mode=pl.Buffered(1))
    rows = lambda w: pl.BlockSpec((tm, w), lambda i: (i, 0))
    act = lambda dt: jax.ShapeDtypeStruct((m, d), dt)
    return pl.pallas_call(
        functools.partial(_in_proj_kernel, tiles_per_seq=seq // tm, n_mix=n_mix,
                          key_tile=key_tile, n_lora=n_lora),
        grid=(m // tm,),
        in_specs=[
            rows(d),
            pl.BlockSpec((PREV_ROWS, d), lambda i: (jnp.maximum(i * (tm // PREV_ROWS) - 1, 0), 0)),
            const(g), const(w16), const(sb), const(ln_w), const(ln_b),
            const(ww), const(wa), const(wg), const(vecs), const(hb),
        ],
        out_specs=[rows(d), rows(d), rows(n_main * d), rows(d), rows(d), rows(d), rows(d)],
        out_shape=[act(BF16), act(BF16), jax.ShapeDtypeStruct((m, n_main * d), BF16),
                   act(F32), act(BF16), act(BF16), act(BF16)],
        scratch_shapes=[pltpu.VMEM((tm, d), BF16), pltpu.VMEM((PREV_ROWS, d), BF16)],
        compiler_params=_cparams(("parallel",)),
        name="in_proj",
    )(x2, x2, g, w16, sb, ln_w, ln_b, ww, wa, wg, vecs, hb)


def _sgu_tile(zu_ref, vn_ref, ga_ref, bias_ref, wpa_ref, wm_scr, s_scr):
    ch = SGU_CHUNK
    dg = LANES
    for c in range(zu_ref.shape[0] // ch):
        rows = pl.ds(c * ch, ch)
        for g in range(SGU_GROUPS):
            cols = slice(g * dg, (g + 1) * dg)
            sv = _dot(wm_scr[g], vn_ref[rows, cols]) + bias_ref[:, cols]
            s_scr[rows, cols] = (zu_ref[rows, cols] * sv).astype(BF16)
    return jax.nn.sigmoid(ga_ref[...].astype(F32)) * _dot(s_scr[...], wpa_ref[...])


def _scan_kernel(r_ref, k_ref, v_ref, lw_ref, kk_ref, aa_ref, o_ref, s_scr, *, chunk, hd, gw):
    c_len = chunk
    assert c_len == hd
    bb, tblk, d = r_ref.shape
    n_groups = d // gw
    hpg = gw // hd
    n_double = int(math.log2(c_len)) - 1

    rowi = lax.broadcasted_iota(jnp.int32, (c_len, gw), 0)
    sub = lax.broadcasted_iota(jnp.int32, (c_len, gw), 1) % c_len
    strict = sub < rowi
    incl = sub <= rowi
    eye = jnp.where(sub == rowi, 1.0, 0.0).astype(F32)
    bd_mask = (lax.broadcasted_iota(jnp.int32, (gw, gw), 0) // c_len
               == lax.broadcasted_iota(jnp.int32, (gw, gw), 1) // hd)
    ltri = jnp.where(lax.broadcasted_iota(jnp.int32, (c_len, c_len), 1)
                     <= lax.broadcasted_iota(jnp.int32, (c_len, c_len), 0), 1.0, 0.0).astype(BF16)

    @pl.when(pl.program_id(1) == 0)
    def _():
        s_scr[...] = jnp.zeros_like(s_scr)

    def bd(x16):
        t = jnp.concatenate([x16] * hpg, axis=0)
        return jnp.where(bd_mask, t, jnp.zeros_like(t))

    chains = [(b, g) for b in range(bb) for g in range(n_groups)]
    n = range(len(chains))

    def body(c, carry):
        rows = pl.ds(pl.multiple_of(c * c_len, c_len), c_len)

        def at(ref, i):
            b, g = chains[i]
            return ref.at[b, rows, g * gw:(g + 1) * gw]

        ar, bq, kq, v16, bk, g_tot = [], [], [], [], [], []
        for i in n:
            k = at(k_ref, i)[...].astype(F32)
            lw = at(lw_ref, i)[...]
            kk = at(kk_ref, i)[...].astype(F32)
            b = kk * at(aa_ref, i)[...].astype(F32)
            lh, lm, ll = _split3(lw)
            cum = _dot(ltri, lh) + (_dot(ltri, lm) + _dot(ltri, ll))
            cum_last = cum[c_len - 1:c_len, :]
            g_inv = jnp.exp(-cum)
            g_end = jnp.exp(cum_last - cum)
            aq = -(kk * jnp.exp(cum - lw))
            rq = at(r_ref, i)[...].astype(F32) * jnp.exp(cum)
            ar.append(jnp.concatenate([aq, rq], axis=0).astype(BF16))
            bq.append((b * g_inv).astype(BF16))
            kq.append((k * g_inv).astype(BF16))
            v16.append(at(v_ref, i)[...].astype(BF16))
            bk.append(jnp.concatenate([b * g_end, k * g_end], axis=0).astype(BF16))
            g_tot.append(jnp.exp(cum_last))

        pb = [_dot(ar[i], bd(bq[i]), _NT) for i in n]
        pk = [_dot(ar[i], bd(kq[i]), _NT) for i in n]
        s0 = [s_scr[chains[i]] for i in n]
        ar_s = [_dot(ar[i], s0[i].astype(BF16), _NT) for i in n]
        a_ab = [jnp.where(strict, pb[i][:c_len], 0.0) for i in n]
        a_rb = [jnp.where(incl, pb[i][c_len:], 0.0).astype(BF16) for i in n]
        akv = [_dot(jnp.concatenate([jnp.where(strict, pk[i][:c_len], 0.0),
                                     jnp.where(incl, pk[i][c_len:], 0.0)], axis=0).astype(BF16),
                    bd(v16[i])) for i in n]
        rhs = [ar_s[i][:c_len] + akv[i][:c_len] for i in n]

        p = [_dot(a_ab[i].astype(BF16), bd(a_ab[i].astype(BF16))) for i in n]
        x = [eye + a_ab[i] for i in n]
        for _ in range(n_double - 1):
            xp = [_dot(jnp.concatenate([x[i], p[i]], axis=0).astype(BF16), bd(p[i].astype(BF16)))
                  for i in n]
            x = [x[i] + xp[i][:c_len] for i in n]
            p = [xp[i][c_len:] for i in n]
        x = [x[i] + _dot(x[i].astype(BF16), bd(p[i].astype(BF16))) for i in n]

        sa16 = [_dot(x[i].astype(BF16), bd(rhs[i].astype(BF16))).astype(BF16) for i in n]
        for i in n:
            at(o_ref, i)[...] = ((ar_s[i][c_len:] + akv[i][c_len:])
                                 + _dot(a_rb[i], bd(sa16[i]))).astype(o_ref.dtype)
        for i in n:
            upd = _dot(jnp.concatenate([sa16[i], v16[i]], axis=0), bk[i], _TN)
            s_scr[chains[i]] = jnp.where(bd_mask, s0[i] * g_tot[i] + upd, 0.0)
        return carry

    lax.fori_loop(0, tblk // c_len, body, 0)


def _rwkv_scan(p_main, lw, kk, aa, batch, seq, d, bb, tblk):
    gw = SCAN_GROUP_HEADS * RWKV_HEAD
    spec = pl.BlockSpec((bb, tblk, d), lambda b, t: (b, t, 0))
    col = lambda cb: pl.BlockSpec((bb, tblk, d), lambda b, t: (b, t, cb))
    as3 = lambda a: a.reshape(batch, seq, a.shape[-1])
    out = pl.pallas_call(
        functools.partial(_scan_kernel, chunk=SCAN_CHUNK, hd=RWKV_HEAD, gw=gw),
        grid=(batch // bb, seq // tblk),
        in_specs=[col(0), col(1), col(2), spec, spec, spec],
        out_specs=spec,
        out_shape=jax.ShapeDtypeStruct((batch, seq, d), BF16),
        scratch_shapes=[pltpu.VMEM((bb, d // gw, gw, gw), F32)],
        compiler_params=_cparams(("parallel", "arbitrary")),
        name="rwkv_scan",
    )(as3(p_main), as3(p_main), as3(p_main), as3(lw), as3(kk), as3(aa))
    return out.reshape(batch * seq, d)


def _mix_kernel(zu_ref, vn_ref, ga_ref, o_ref, r_ref, k_ref, v_ref, g_ref, gb_ref, x_ref,
                ws_ref, bias_ref, vec_ref, hb_ref, wpa_ref, wpb_ref, wout_ref,
                h_ref, f_ref, wm_scr, s_scr, *, hd):
    @pl.when(pl.program_id(0) == 0)
    def _():
        ch = SGU_CHUNK
        row = lax.broadcasted_iota(jnp.int32, (ch, ch), 0)
        col = lax.broadcasted_iota(jnp.int32, (ch, ch), 1)
        for g in range(SGU_GROUPS):
            wm_scr[g] = jnp.where(col <= row, ws_ref[g], 0.0).astype(BF16)

    ya = _sgu_tile(zu_ref, vn_ref, ga_ref, bias_ref, wpa_ref, wm_scr, s_scr)

    lnx_w = vec_ref[0:1, :]
    lnx_b = vec_ref[1:2, :]
    r_k = vec_ref[2:3, :]
    g_ffn = vec_ref[3:4, :]
    o = o_ref[...].astype(F32)
    inv_n = 1.0 / hd
    mu = _head_sum(o, hb_ref) * inv_n
    oc = o - mu
    var = _head_sum(oc * oc, hb_ref) * inv_n
    on = (oc * lax.rsqrt(var + GN_EPS)) * lnx_w + lnx_b
    v = v_ref[...].astype(F32)
    rk = r_ref[...].astype(F32) * k_ref[...].astype(F32)
    bonus = _head_sum(rk * r_k, hb_ref) * v
    yb = _dot(((on + bonus) * g_ref[...]).astype(BF16), wpb_ref[...])
    mixed = ya + jax.nn.sigmoid(gb_ref[...].astype(F32)) * yb
    h = x_ref[...] + _dot(mixed.astype(BF16), wout_ref[...])
    h_ref[...] = h
    f_ref[...] = _rms_rows(h, g_ffn).astype(BF16)


def _mix(zu, vn, o, g, p_main, x2, sgu_w, bias_full, vecs, hb, wpa16, wpb16, wout16, tm, d):
    m = x2.shape[0]
    ch = SGU_CHUNK
    tile = pl.BlockSpec((tm, d), lambda i: (i, 0))
    col = lambda cb: pl.BlockSpec((tm, d), lambda i: (i, cb))
    const = lambda a: pl.BlockSpec(a.shape, lambda i: (0,) * a.ndim,
                                   pipeline_mode=pl.Buffered(1))
    return pl.pallas_call(
        functools.partial(_mix_kernel, hd=RWKV_HEAD),
        grid=(m // tm,),
        in_specs=[tile, tile, col(3), tile, col(0), col(1), col(2), tile, col(4), tile,
                  const(sgu_w), const(bias_full), const(vecs), const(hb),
                  const(wpa16), const(wpb16), const(wout16)],
        out_specs=[tile, tile],
        out_shape=[jax.ShapeDtypeStruct((m, d), F32), jax.ShapeDtypeStruct((m, d), BF16)],
        scratch_shapes=[pltpu.VMEM((SGU_GROUPS, ch, ch), BF16), pltpu.VMEM((tm, d), BF16)],
        compiler_params=_cparams(("arbitrary",)),
        name="mix",
    )(zu, vn, p_main, o, p_main, p_main, p_main, g, p_main, x2,
      sgu_w, bias_full, vecs, hb, wpa16, wpb16, wout16)


def _ffn_kernel(h_ref, f_ref, w1_ref, w2_ref, gf_ref, o_ref, *, tf):
    acc = h_ref[...]
    for j in range(w1_ref.shape[1] // tf):
        cols = slice(j * tf, (j + 1) * tf)
        t = jnp.maximum(_dot(f_ref[...], w1_ref[:, cols]), 0.0)
        acc = acc + _dot((t * t).astype(BF16), w2_ref[cols, :])
    o_ref[...] = _rms_rows(acc, gf_ref[...])


def _ffn(h, f, w1_16, w2_16, g_final, tm, tf):
    m, d = h.shape
    dff = w1_16.shape[1]
    assert dff % tf == 0
    tile = pl.BlockSpec((tm, d), lambda i: (i, 0))
    const = lambda shape: pl.BlockSpec(shape, lambda i: (0, 0), pipeline_mode=pl.Buffered(1))
    return pl.pallas_call(
        functools.partial(_ffn_kernel, tf=tf),
        grid=(m // tm,),
        in_specs=[tile, tile, const((d, dff)), const((dff, d)), const((1, d))],
        out_specs=tile,
        out_shape=jax.ShapeDtypeStruct((m, d), F32),
        compiler_params=_cparams(("parallel",)),
        name="ffn",
    )(h, f, w1_16, w2_16, g_final)


def _pad_cols(a, n):
    return jnp.pad(a, ((0, 0), (0, n - a.shape[1])))


def _pad_rows(a, n):
    return jnp.pad(a, ((0, n - a.shape[0]), (0, 0)))


def _layer(x2, batch, seq, g_mix, w_in, sgu_ln_w, sgu_ln_b, sgu_w, sgu_b, w_proj_a, shift_b,
           w_lora_w, w0, a_lora_w, a0, g_lora_w, k_k, k_a, r_k, ln_x_w, ln_x_b, w_proj_b,
           w_out, g_ffn, w_ffn1, w_ffn2, g_out):
    d = x2.shape[1]
    lora_w, lora_a, lora_g = w_lora_w.shape[0], a_lora_w.shape[0], g_lora_w.shape[0]
    c_sgu = 2 * d
    c_rkv = 3 * d
    c_lora = lora_w + lora_a + lora_g
    o_lora = c_sgu + c_rkv
    o_gate = o_lora + c_lora
    pw, pa = LANES, LANES
    pg = -(-lora_g // LANES) * LANES

    n_lora = pw + pa + pg
    w16 = w_in.astype(BF16)
    w_all = jnp.concatenate([
        w16[:, :o_lora], w16[:, o_gate:],
        _pad_cols(w16[:, o_lora:o_lora + lora_w], pw),
        _pad_cols(w16[:, o_lora + lora_w:o_lora + lora_w + lora_a], pa),
        _pad_cols(w16[:, o_lora + lora_w + lora_a:o_gate], pg + d - n_lora)], axis=1)
    sb_lo = shift_b[:, c_rkv:]
    sb_all = jnp.concatenate([
        jnp.zeros((2, c_sgu), F32), shift_b[:, :c_rkv], jnp.zeros((2, 2 * d), F32),
        _pad_cols(sb_lo[:, :lora_w], pw),
        _pad_cols(sb_lo[:, lora_w:lora_w + lora_a], pa),
        _pad_cols(sb_lo[:, lora_w + lora_a:], pg + d - n_lora)], axis=1)

    gw = SCAN_GROUP_HEADS * RWKV_HEAD
    hb = (lax.broadcasted_iota(jnp.int32, (gw, gw), 0) // RWKV_HEAD
          == lax.broadcasted_iota(jnp.int32, (gw, gw), 1) // RWKV_HEAD).astype(BF16)
    vec_prep = jnp.stack([w0, a0, k_k, k_a])
    lo16 = lambda w, rows: _pad_rows(w, rows).astype(BF16)
    zu, vn, p_main, lw, kk, aa, g = _in_proj(
        x2, g_mix.reshape(1, d), w_all, sb_all, sgu_ln_w.reshape(1, d), sgu_ln_b.reshape(1, d),
        lo16(w_lora_w, pw), lo16(a_lora_w, pa), lo16(g_lora_w, pg), vec_prep, hb, seq,
        tm=IN_PROJ_ROWS, n_mix=c_rkv // d, key_tile=1, n_lora=n_lora)

    bias_full = jnp.repeat(sgu_b.T, d // SGU_GROUPS, axis=1)

    o = _rwkv_scan(p_main, lw, kk, aa, batch, seq, d, bb=SCAN_SEQS, tblk=SCAN_ROWS)

    vec_post = jnp.stack([ln_x_w, ln_x_b, r_k, g_ffn])
    h1, f = _mix(zu, vn, o, g, p_main, x2, sgu_w, bias_full, vec_post, hb,
                 w_proj_a.astype(BF16), w_proj_b.astype(BF16), w_out.astype(BF16),
                 tm=MIX_ROWS, d=d)
    return _ffn(h1, f, w_ffn1.astype(BF16), w_ffn2.astype(BF16), g_out.reshape(1, d),
                tm=FFN_ROWS, tf=FFN_COLS)


def kernel(x, g_mix, w_in, sgu_ln_w, sgu_ln_b, sgu_w, sgu_b, w_proj_a, shift_b, w_lora_w, w0,
           a_lora_w, a0, g_lora_w, k_k, k_a, r_k, ln_x_w, ln_x_b, w_proj_b, w_out, g_ffn,
           w_ffn1, w_ffn2, g_final):
    batch, seq, d = x.shape
    depth = w_in.shape[0]
    assert depth == 1, "the final RMSNorm is fused into the single layer's ffn call"
    h = x.reshape(batch * seq, d)
    l = 0
    h = _layer(h, batch, seq, g_mix[l], w_in[l], sgu_ln_w[l], sgu_ln_b[l], sgu_w[l], sgu_b[l],
               w_proj_a[l], shift_b[l], w_lora_w[l], w0[l], a_lora_w[l], a0[l], g_lora_w[l],
               k_k[l], k_a[l], r_k[l], ln_x_w[l], ln_x_b[l], w_proj_b[l], w_out[l], g_ffn[l],
               w_ffn1[l], w_ffn2[l], g_final)
    return h.reshape(batch, seq, d)
```

```python
import functools
import math

import jax
import jax.numpy as jnp
from jax import lax
from jax.experimental import pallas as pl
from jax.experimental.pallas import tpu as pltpu

F32 = jnp.float32
BF16 = jnp.bfloat16

SGU_CHUNK = 128
SGU_GROUPS = 8
RWKV_HEAD = 64
NORM_EPS = 1e-6
LN_EPS = 1e-5
GN_EPS = 64e-5

LANES = 128
MXU_WIDTH = 256
PREV_ROWS = 16
NORM_ROW_CHUNKS = 4
SCAN_CHUNK = 64
SCAN_GROUP_HEADS = MXU_WIDTH // RWKV_HEAD
VMEM_LIMIT = 56 * 1024 * 1024

IN_PROJ_ROWS = 512
SCAN_SEQS = 4
SCAN_ROWS = 256
MIX_ROWS = 512
FFN_ROWS = 1024
FFN_COLS = 1024


def _cparams(sem):
    return pltpu.CompilerParams(dimension_semantics=sem, vmem_limit_bytes=VMEM_LIMIT)


def _dot(a, b, dims=(((1,), (0,)), ((), ()))):
    return lax.dot_general(a, b, dims, preferred_element_type=F32)


_NT = (((1,), (1,)), ((), ()))
_TN = (((0,), (0,)), ((), ()))


def _split2(x):
    hi = x.astype(BF16)
    lo = (x - hi.astype(F32)).astype(BF16)
    return hi, lo


def _head_sum(x, hb_ref):
    w = hb_ref.shape[0]
    hb = hb_ref[...]
    return jnp.concatenate(
        [_dot(x[:, c * w:(c + 1) * w].astype(BF16), hb) for c in range(x.shape[1] // w)], axis=1)


def _rms_rows(x, g):
    ms = jnp.mean(x * x, axis=-1, keepdims=True)
    return (x * lax.rsqrt(ms + NORM_EPS)) * g


def _gelu(x):
    return 0.5 * x * (1.0 + lax.erf(x * (1.0 / math.sqrt(2.0))))


def _shift_mix_tile(a_scr, ap_scr, w_ref, sb_ref, first, cols, row_chunks=1):
    w = w_ref[:, cols]
    rc = a_scr.shape[0] // row_chunks
    p = jnp.concatenate([_dot(a_scr[c * rc:(c + 1) * rc, :], w) for c in range(row_chunks)],
                        axis=0)
    pp = _dot(ap_scr[...], w)
    prev_row = jnp.where(first, 0.0, pp[PREV_ROWS - 1:, :])
    row = lax.broadcasted_iota(jnp.int32, p.shape, 0)
    shifted = jnp.where(row == 0, prev_row, pltpu.roll(p, 1, 0))
    return p * sb_ref[0:1, cols] + shifted * sb_ref[1:2, cols]


def _in_proj_kernel(x_ref, xp_ref, g_ref, w_ref, sb_ref, lnw_ref, lnb_ref,
                    ww_ref, wa_ref, wg_ref, vec_ref, hb_ref,
                    zu_ref, vn_ref, p_ref, lw_ref, kk_ref, aa_ref, gg_ref, a_scr, ap_scr, *,
                    tiles_per_seq, n_mix, key_tile, n_lora):
    d = x_ref.shape[1]
    n_main = p_ref.shape[1] // d
    first = (pl.program_id(0) % tiles_per_seq) == 0
    tile = lambda t: slice(t * d, (t + 1) * d)
    w0, a0, k_k, k_a = (vec_ref[r:r + 1, :] for r in range(4))

    ap_scr[...] = _rms_rows(xp_ref[...], g_ref[...]).astype(BF16)
    rc = x_ref.shape[0] // NORM_ROW_CHUNKS
    for c in range(NORM_ROW_CHUNKS):
        rows = pl.ds(c * rc, rc)
        a_scr[rows, :] = _rms_rows(x_ref[rows, :], g_ref[...]).astype(BF16)

    def pieces(t):
        return [(slice(s, s + MXU_WIDTH), slice(t * d + s, t * d + s + MXU_WIDTH))
                for s in range(0, d, MXU_WIDTH)]

    lo0 = (2 + n_main) * d
    lo = _shift_mix_tile(a_scr, ap_scr, w_ref, sb_ref, first, slice(lo0, lo0 + n_lora),
                         row_chunks=NORM_ROW_CHUNKS)
    xw = jnp.tanh(lo[:, 0:LANES]).astype(BF16)
    xa = lo[:, LANES:2 * LANES].astype(BF16)
    xg = jax.nn.sigmoid(lo[:, 2 * LANES:]).astype(BF16)
    for c, _ in pieces(0):
        zw = w0[:, c] + _dot(xw, ww_ref[:, c])
        lw_ref[:, c] = (-math.exp(-0.5)) * jax.nn.sigmoid(zw)
        gg_ref[:, c] = _dot(xg, wg_ref[:, c]).astype(gg_ref.dtype)

    def gelu_tile():
        for dst, src in pieces(0):
            zu_ref[:, dst] = _gelu(_dot(a_scr[...], w_ref[:, src])).astype(BF16)

    def layernorm_tile():
        zv = _gelu(_dot(a_scr[...], w_ref[:, tile(1)]))
        mu = jnp.mean(zv, axis=-1, keepdims=True)
        zc = zv - mu
        var = jnp.mean(zc * zc, axis=-1, keepdims=True)
        vn_ref[...] = ((zc * lax.rsqrt(var + LN_EPS)) * lnw_ref[...]
                       + lnb_ref[...]).astype(BF16)

    def main_tile(t):
        for dst, src in pieces(2 + t):
            if t < n_mix:
                p = _shift_mix_tile(a_scr, ap_scr, w_ref, sb_ref, first, src)
            else:
                p = _dot(a_scr[...], w_ref[:, src])
            if t == key_tile:
                aa = jax.nn.sigmoid(a0[:, dst] + _dot(xa, wa_ref[:, dst]))
                aa_ref[:, dst] = aa.astype(aa_ref.dtype)
                kraw = p * k_k[:, dst]
                ss = _dot((kraw * kraw).astype(BF16), hb_ref[...])
                kk_ref[:, dst] = (kraw * lax.rsqrt(jnp.maximum(ss, 1e-24))).astype(kk_ref.dtype)
                p = p * ((1.0 - k_a[:, dst]) + aa * k_a[:, dst])
            p_ref[:, slice(t * d + dst.start, t * d + dst.stop)] = p.astype(p_ref.dtype)

    plain = list(range(n_mix, n_main))
    heavy = [gelu_tile, layernorm_tile] + [functools.partial(main_tile, t) for t in range(n_mix)]
    for step in range(max(len(plain), len(heavy))):
        if step < len(plain):
            main_tile(plain[step])
        if step < len(heavy):
            heavy[step]()


def _in_proj(x2, g, w16, sb, ln_w, ln_b, ww, wa, wg, vecs, hb, seq, tm, n_mix, key_tile, n_lora):
    m, d = x2.shape
    n = w16.shape[1]
    n_main = n // d - 3
    assert seq % tm == 0 and n % d == 0 and hb.shape[0] == MXU_WIDTH
    const = lambda a: pl.BlockSpec(a.shape, lambda i: (0, 0), pipeline_mode=pl.Buffered(1))
    rows = lambda w: pl.BlockSpec((tm, w), lambda i: (i, 0))
    act = lambda dt: jax.ShapeDtypeStruct((m, d), dt)
    return pl.pallas_call(
        functools.partial(_in_proj_kernel, tiles_per_seq=seq // tm, n_mix=n_mix,
                          key_tile=key_tile, n_lora=n_lora),
        grid=(m // tm,),
        in_specs=[
            rows(d),
            pl.BlockSpec((PREV_ROWS, d), lambda i: (jnp.maximum(i * (tm // PREV_ROWS) - 1, 0), 0)),
            const(g), const(w16), const(sb), const(ln_w), const(ln_b),
            const(ww), const(wa), const(wg), const(vecs), const(hb),
        ],
        out_specs=[rows(d), rows(d), rows(n_main * d), rows(d), rows(d), rows(d), rows(d)],
        out_shape=[act(BF16), act(BF16), jax.ShapeDtypeStruct((m, n_main * d), BF16),
                   act(F32), act(BF16), act(BF16), act(BF16)],
        scratch_shapes=[pltpu.VMEM((tm, d), BF16), pltpu.VMEM((PREV_ROWS, d), BF16)],
        compiler_params=_cparams(("parallel",)),
        name="in_proj",
    )(x2, x2, g, w16, sb, ln_w, ln_b, ww, wa, wg, vecs, hb)


def _sgu_tile(zu_ref, vn_ref, ga_ref, bias_ref, wpa_ref, wm_scr, s_scr):
    ch = SGU_CHUNK
    dg = LANES
    for c in range(zu_ref.shape[0] // ch):
        rows = pl.ds(c * ch, ch)
        for g in range(SGU_GROUPS):
            cols = slice(g * dg, (g + 1) * dg)
            sv = _dot(wm_scr[g], vn_ref[rows, cols]) + bias_ref[:, cols]
            s_scr[rows, cols] = (zu_ref[rows, cols] * sv).astype(BF16)
    return jax.nn.sigmoid(ga_ref[...].astype(F32)) * _dot(s_scr[...], wpa_ref[...])


def _scan_kernel(r_ref, k_ref, v_ref, lw_ref, kk_ref, aa_ref, o_ref, s_scr, *, chunk, hd, gw):
    c_len = chunk
    assert c_len == hd
    bb, tblk, d = r_ref.shape
    n_groups = d // gw
    hpg = gw // hd
    n_double = int(math.log2(c_len)) - 1

    rowi = lax.broadcasted_iota(jnp.int32, (c_len, gw), 0)
    sub = lax.broadcasted_iota(jnp.int32, (c_len, gw), 1) % c_len
    strict = sub < rowi
    incl = sub <= rowi
    eye = jnp.where(sub == rowi, 1.0, 0.0).astype(F32)
    bd_mask = (lax.broadcasted_iota(jnp.int32, (gw, gw), 0) // c_len
               == lax.broadcasted_iota(jnp.int32, (gw, gw), 1) // hd)
    ltri = jnp.where(lax.broadcasted_iota(jnp.int32, (c_len, c_len), 1)
                     <= lax.broadcasted_iota(jnp.int32, (c_len, c_len), 0), 1.0, 0.0).astype(BF16)

    @pl.when(pl.program_id(1) == 0)
    def _():
        s_scr[...] = jnp.zeros_like(s_scr)

    def bd(x16):
        t = jnp.concatenate([x16] * hpg, axis=0)
        return jnp.where(bd_mask, t, jnp.zeros_like(t))

    chains = [(b, g) for b in range(bb) for g in range(n_groups)]
    n = range(len(chains))

    def body(c, carry):
        rows = pl.ds(pl.multiple_of(c * c_len, c_len), c_len)

        def at(ref, i):
            b, g = chains[i]
            return ref.at[b, rows, g * gw:(g + 1) * gw]

        ar, bq, kq, v16, bk, g_tot = [], [], [], [], [], []
        for i in n:
            k = at(k_ref, i)[...].astype(F32)
            lw = at(lw_ref, i)[...]
            kk = at(kk_ref, i)[...].astype(F32)
            b = kk * at(aa_ref, i)[...].astype(F32)
            lh, ll = _split2(lw)
            cum = _dot(ltri, lh) + _dot(ltri, ll)
            cum_last = cum[c_len - 1:c_len, :]
            g_inv = jnp.exp(-cum)
            g_end = jnp.exp(cum_last - cum)
            aq = -(kk * jnp.exp(cum - lw))
            rq = at(r_ref, i)[...].astype(F32) * jnp.exp(cum)
            ar.append(jnp.concatenate([aq, rq], axis=0).astype(BF16))
            bq.append((b * g_inv).astype(BF16))
            kq.append((k * g_inv).astype(BF16))
            v16.append(at(v_ref, i)[...].astype(BF16))
            bk.append(jnp.concatenate([b * g_end, k * g_end], axis=0).astype(BF16))
            g_tot.append(jnp.exp(cum_last))

        pb = [_dot(ar[i], bd(bq[i]), _NT) for i in n]
        pk = [_dot(ar[i], bd(kq[i]), _NT) for i in n]
        s0 = [s_scr[chains[i]] for i in n]
        ar_s = [_dot(ar[i], s0[i].astype(BF16), _NT) for i in n]
        a_ab = [jnp.where(strict, pb[i][:c_len], 0.0) for i in n]
        a_rb = [jnp.where(incl, pb[i][c_len:], 0.0).astype(BF16) for i in n]
        akv = [_dot(jnp.concatenate([jnp.where(strict, pk[i][:c_len], 0.0),
                                     jnp.where(incl, pk[i][c_len:], 0.0)], axis=0).astype(BF16),
                    bd(v16[i])) for i in n]
        rhs = [ar_s[i][:c_len] + akv[i][:c_len] for i in n]

        p = [_dot(a_ab[i].astype(BF16), bd(a_ab[i].astype(BF16))) for i in n]
        x = [eye + a_ab[i] for i in n]
        for _ in range(n_double - 1):
            xp = [_dot(jnp.concatenate([x[i], p[i]], axis=0).astype(BF16), bd(p[i].astype(BF16)))
                  for i in n]
            x = [x[i] + xp[i][:c_len] for i in n]
            p = [xp[i][c_len:] for i in n]
        x = [x[i] + _dot(x[i].astype(BF16), bd(p[i].astype(BF16))) for i in n]

        sa16 = [_dot(x[i].astype(BF16), bd(rhs[i].astype(BF16))).astype(BF16) for i in n]
        for i in n:
            at(o_ref, i)[...] = ((ar_s[i][c_len:] + akv[i][c_len:])
                                 + _dot(a_rb[i], bd(sa16[i]))).astype(o_ref.dtype)
        for i in n:
            upd = _dot(jnp.concatenate([sa16[i], v16[i]], axis=0), bk[i], _TN)
            s_scr[chains[i]] = jnp.where(bd_mask, s0[i] * g_tot[i] + upd, 0.0)
        return carry

    lax.fori_loop(0, tblk // c_len, body, 0)


def _rwkv_scan(p_main, lw, kk, aa, batch, seq, d, bb, tblk):
    gw = SCAN_GROUP_HEADS * RWKV_HEAD
    spec = pl.BlockSpec((bb, tblk, d), lambda b, t: (b, t, 0))
    col = lambda cb: pl.BlockSpec((bb, tblk, d), lambda b, t: (b, t, cb))
    as3 = lambda a: a.reshape(batch, seq, a.shape[-1])
    out = pl.pallas_call(
        functools.partial(_scan_kernel, chunk=SCAN_CHUNK, hd=RWKV_HEAD, gw=gw),
        grid=(batch // bb, seq // tblk),
        in_specs=[col(0), col(1), col(2), spec, spec, spec],
        out_specs=spec,
        out_shape=jax.ShapeDtypeStruct((batch, seq, d), BF16),
        scratch_shapes=[pltpu.VMEM((bb, d // gw, gw, gw), F32)],
        compiler_params=_cparams(("parallel", "arbitrary")),
        name="rwkv_scan",
    )(as3(p_main), as3(p_main), as3(p_main), as3(lw), as3(kk), as3(aa))
    return out.reshape(batch * seq, d)


def _mix_kernel(zu_ref, vn_ref, ga_ref, o_ref, r_ref, k_ref, v_ref, g_ref, gb_ref, x_ref,
                ws_ref, bias_ref, vec_ref, hb_ref, wpa_ref, wpb_ref, wout_ref,
                h_ref, f_ref, wm_scr, s_scr, *, hd):
    @pl.when(pl.program_id(0) == 0)
    def _():
        ch = SGU_CHUNK
        row = lax.broadcasted_iota(jnp.int32, (ch, ch), 0)
        col = lax.broadcasted_iota(jnp.int32, (ch, ch), 1)
        for g in range(SGU_GROUPS):
            wm_scr[g] = jnp.where(col <= row, ws_ref[g], 0.0).astype(BF16)

    ya = _sgu_tile(zu_ref, vn_ref, ga_ref, bias_ref, wpa_ref, wm_scr, s_scr)

    lnx_w = vec_ref[0:1, :]
    lnx_b = vec_ref[1:2, :]
    r_k = vec_ref[2:3, :]
    g_ffn = vec_ref[3:4, :]
    o = o_ref[...].astype(F32)
    inv_n = 1.0 / hd
    mu = _head_sum(o, hb_ref) * inv_n
    oc = o - mu
    var = _head_sum(oc * oc, hb_ref) * inv_n
    on = (oc * lax.rsqrt(var + GN_EPS)) * lnx_w + lnx_b
    v = v_ref[...].astype(F32)
    rk = r_ref[...].astype(F32) * k_ref[...].astype(F32)
    bonus = _head_sum(rk * r_k, hb_ref) * v
    yb = _dot(((on + bonus) * g_ref[...]).astype(BF16), wpb_ref[...])
    mixed = ya + jax.nn.sigmoid(gb_ref[...].astype(F32)) * yb
    h = x_ref[...] + _dot(mixed.astype(BF16), wout_ref[...])
    h_ref[...] = h
    f_ref[...] = _rms_rows(h, g_ffn).astype(BF16)


def _mix(zu, vn, o, g, p_main, x2, sgu_w, bias_full, vecs, hb, wpa16, wpb16, wout16, tm, d):
    m = x2.shape[0]
    ch = SGU_CHUNK
    tile = pl.BlockSpec((tm, d), lambda i: (i, 0))
    col = lambda cb: pl.BlockSpec((tm, d), lambda i: (i, cb))
    const = lambda a: pl.BlockSpec(a.shape, lambda i: (0,) * a.ndim,
                                   pipeline_mode=pl.Buffered(1))
    return pl.pallas_call(
        functools.partial(_mix_kernel, hd=RWKV_HEAD),
        grid=(m // tm,),
        in_specs=[tile, tile, col(3), tile, col(0), col(1), col(2), tile, col(4), tile,
                  const(sgu_w), const(bias_full), const(vecs), const(hb),
                  const(wpa16), const(wpb16), const(wout16)],
        out_specs=[tile, tile],
        out_shape=[jax.ShapeDtypeStruct((m, d), F32), jax.ShapeDtypeStruct((m, d), BF16)],
        scratch_shapes=[pltpu.VMEM((SGU_GROUPS, ch, ch), BF16), pltpu.VMEM((tm, d), BF16)],
        compiler_params=_cparams(("arbitrary",)),
        name="mix",
    )(zu, vn, p_main, o, p_main, p_main, p_main, g, p_main, x2,
      sgu_w, bias_full, vecs, hb, wpa16, wpb16, wout16)


def _ffn_kernel(h_ref, f_ref, w1_ref, w2_ref, gf_ref, o_ref, *, tf):
    acc = h_ref[...]
    for j in range(w1_ref.shape[1] // tf):
        cols = slice(j * tf, (j + 1) * tf)
        t = jnp.maximum(_dot(f_ref[...], w1_ref[:, cols]), 0.0)
        acc = acc + _dot((t * t).astype(BF16), w2_ref[cols, :])
    o_ref[...] = _rms_rows(acc, gf_ref[...])


def _ffn(h, f, w1_16, w2_16, g_final, tm, tf):
    m, d = h.shape
    dff = w1_16.shape[1]
    assert dff % tf == 0
    tile = pl.BlockSpec((tm, d), lambda i: (i, 0))
    const = lambda shape: pl.BlockSpec(shape, lambda i: (0, 0), pipeline_mode=pl.Buffered(1))
    return pl.pallas_call(
        functools.partial(_ffn_kernel, tf=tf),
        grid=(m // tm,),
        in_specs=[tile, tile, const((d, dff)), const((dff, d)), const((1, d))],
        out_specs=tile,
        out_shape=jax.ShapeDtypeStruct((m, d), F32),
        compiler_params=_cparams(("parallel",)),
        name="ffn",
    )(h, f, w1_16, w2_16, g_final)


def _pad_cols(a, n):
    return jnp.pad(a, ((0, 0), (0, n - a.shape[1])))


def _pad_rows(a, n):
    return jnp.pad(a, ((0, n - a.shape[0]), (0, 0)))


def _layer(x2, batch, seq, g_mix, w_in, sgu_ln_w, sgu_ln_b, sgu_w, sgu_b, w_proj_a, shift_b,
           w_lora_w, w0, a_lora_w, a0, g_lora_w, k_k, k_a, r_k, ln_x_w, ln_x_b, w_proj_b,
           w_out, g_ffn, w_ffn1, w_ffn2, g_out):
    d = x2.shape[1]
    lora_w, lora_a, lora_g = w_lora_w.shape[0], a_lora_w.shape[0], g_lora_w.shape[0]
    c_sgu = 2 * d
    c_rkv = 3 * d
    c_lora = lora_w + lora_a + lora_g
    o_lora = c_sgu + c_rkv
    o_gate = o_lora + c_lora
    pw, pa = LANES, LANES
    pg = -(-lora_g // LANES) * LANES

    n_lora = pw + pa + pg
    w16 = w_in.astype(BF16)
    w_all = jnp.concatenate([
        w16[:, :o_lora], w16[:, o_gate:],
        _pad_cols(w16[:, o_lora:o_lora + lora_w], pw),
        _pad_cols(w16[:, o_lora + lora_w:o_lora + lora_w + lora_a], pa),
        _pad_cols(w16[:, o_lora + lora_w + lora_a:o_gate], pg + d - n_lora)], axis=1)
    sb_lo = shift_b[:, c_rkv:]
    sb_all = jnp.concatenate([
        jnp.zeros((2, c_sgu), F32), shift_b[:, :c_rkv], jnp.zeros((2, 2 * d), F32),
        _pad_cols(sb_lo[:, :lora_w], pw),
        _pad_cols(sb_lo[:, lora_w:lora_w + lora_a], pa),
        _pad_cols(sb_lo[:, lora_w + lora_a:], pg + d - n_lora)], axis=1)

    gw = SCAN_GROUP_HEADS * RWKV_HEAD
    hb = (lax.broadcasted_iota(jnp.int32, (gw, gw), 0) // RWKV_HEAD
          == lax.broadcasted_iota(jnp.int32, (gw, gw), 1) // RWKV_HEAD).astype(BF16)
    vec_prep = jnp.stack([w0, a0, k_k, k_a])
    lo16 = lambda w, rows: _pad_rows(w, rows).astype(BF16)
    zu, vn, p_main, lw, kk, aa, g = _in_proj(
        x2, g_mix.reshape(1, d), w_all, sb_all, sgu_ln_w.reshape(1, d), sgu_ln_b.reshape(1, d),
        lo16(w_lora_w, pw), lo16(a_lora_w, pa), lo16(g_lora_w, pg), vec_prep, hb, seq,
        tm=IN_PROJ_ROWS, n_mix=c_rkv // d, key_tile=1, n_lora=n_lora)

    bias_full = jnp.repeat(sgu_b.T, d // SGU_GROUPS, axis=1)

    o = _rwkv_scan(p_main, lw, kk, aa, batch, seq, d, bb=SCAN_SEQS, tblk=SCAN_ROWS)

    vec_post = jnp.stack([ln_x_w, ln_x_b, r_k, g_ffn])
    h1, f = _mix(zu, vn, o, g, p_main, x2, sgu_w, bias_full, vec_post, hb,
                 w_proj_a.astype(BF16), w_proj_b.astype(BF16), w_out.astype(BF16),
                 tm=MIX_ROWS, d=d)
    return _ffn(h1, f, w_ffn1.astype(BF16), w_ffn2.astype(BF16), g_out.reshape(1, d),
                tm=FFN_ROWS, tf=FFN_COLS)


def kernel(x, g_mix, w_in, sgu_ln_w, sgu_ln_b, sgu_w, sgu_b, w_proj_a, shift_b, w_lora_w, w0,
           a_lora_w, a0, g_lora_w, k_k, k_a, r_k, ln_x_w, ln_x_b, w_proj_b, w_out, g_ffn,
           w_ffn1, w_ffn2, g_final):
    batch, seq, d = x.shape
    depth = w_in.shape[0]
    assert depth == 1, "the final RMSNorm is fused into the single layer's ffn call"
    h = x.reshape(batch * seq, d)
    l = 0
    h = _layer(h, batch, seq, g_mix[l], w_in[l], sgu_ln_w[l], sgu_ln_b[l], sgu_w[l], sgu_b[l],
               w_proj_a[l], shift_b[l], w_lora_w[l], w0[l], a_lora_w[l], a0[l], g_lora_w[l],
               k_k[l], k_a[l], r_k[l], ln_x_w[l], ln_x_b[l], w_proj_b[l], w_out[l], g_ffn[l],
               w_ffn1[l], w_ffn2[l], g_final)
    return h.reshape(batch, seq, d)
```

```python
import functools
import math

import jax
import jax.numpy as jnp
from jax import lax
from jax.experimental import pallas as pl
from jax.experimental.pallas import tpu as pltpu

F32 = jnp.float32
BF16 = jnp.bfloat16

SGU_CHUNK = 128
SGU_GROUPS = 8
RWKV_HEAD = 64
NORM_EPS = 1e-6
LN_EPS = 1e-5
GN_EPS = 64e-5

LANES = 128
MXU_WIDTH = 256
PREV_ROWS = 16
SCAN_CHUNK = 64
SCAN_GROUP_HEADS = MXU_WIDTH // RWKV_HEAD
VMEM_LIMIT = 56 * 1024 * 1024

IN_PROJ_ROWS = 512
SCAN_SEQS = 4
SCAN_ROWS = 256
MIX_ROWS = 512
FFN_ROWS = 1024
FFN_COLS = 1024


def _cparams(sem):
    return pltpu.CompilerParams(dimension_semantics=sem, vmem_limit_bytes=VMEM_LIMIT)


def _dot(a, b, dims=(((1,), (0,)), ((), ()))):
    return lax.dot_general(a, b, dims, preferred_element_type=F32)


_NT = (((1,), (1,)), ((), ()))
_TN = (((0,), (0,)), ((), ()))


def _split2(x):
    hi = x.astype(BF16)
    lo = (x - hi.astype(F32)).astype(BF16)
    return hi, lo


def _head_sum(x, hb_ref):
    w = hb_ref.shape[0]
    hb = hb_ref[...]
    return jnp.concatenate(
        [_dot(x[:, c * w:(c + 1) * w].astype(BF16), hb) for c in range(x.shape[1] // w)], axis=1)


def _rms_rows(x, g):
    ms = jnp.mean(x * x, axis=-1, keepdims=True)
    return (x * lax.rsqrt(ms + NORM_EPS)) * g


def _gelu(x):
    return 0.5 * x * (1.0 + lax.erf(x * (1.0 / math.sqrt(2.0))))


def _shift_mix_tile(a_scr, w_ref, sb_ref, first, cols):
    pa = _dot(a_scr[...], w_ref[:, cols])
    p = pa[PREV_ROWS:]
    prev_row = jnp.where(first, 0.0, pa[PREV_ROWS - 1:PREV_ROWS, :])
    row = lax.broadcasted_iota(jnp.int32, p.shape, 0)
    shifted = jnp.where(row == 0, prev_row, pltpu.roll(p, 1, 0))
    return p * sb_ref[0:1, cols] + shifted * sb_ref[1:2, cols]


def _in_proj_kernel(x_ref, xp_ref, g_ref, w_ref, sb_ref, lnw_ref, lnb_ref,
                    ww_ref, wa_ref, wg_ref, vec_ref, hb_ref,
                    zu_ref, vn_ref, p_ref, lw_ref, kk_ref, aa_ref, gg_ref, a_scr, *,
                    tiles_per_seq, n_mix, key_tile, n_lora):
    d = x_ref.shape[1]
    n_main = p_ref.shape[1] // d
    first = (pl.program_id(0) % tiles_per_seq) == 0
    tile = lambda t: slice(t * d, (t + 1) * d)
    w0, a0, k_k, k_a = (vec_ref[r:r + 1, :] for r in range(4))

    tm = x_ref.shape[0]
    a_scr[0:PREV_ROWS, :] = _rms_rows(xp_ref[...], g_ref[...]).astype(BF16)
    a_scr[PREV_ROWS:, :] = _rms_rows(x_ref[...], g_ref[...]).astype(BF16)
    a_tile = a_scr.at[pl.ds(PREV_ROWS, tm), :]

    def pieces(t):
        return [(slice(s, s + MXU_WIDTH), slice(t * d + s, t * d + s + MXU_WIDTH))
                for s in range(0, d, MXU_WIDTH)]

    lo0 = (2 + n_main) * d
    lo = _shift_mix_tile(a_scr, w_ref, sb_ref, first, slice(lo0, lo0 + n_lora))
    xw = jnp.tanh(lo[:, 0:LANES]).astype(BF16)
    xa = lo[:, LANES:2 * LANES].astype(BF16)
    xg = jax.nn.sigmoid(lo[:, 2 * LANES:]).astype(BF16)
    for c, _ in pieces(0):
        zw = w0[:, c] + _dot(xw, ww_ref[:, c])
        lw_ref[:, c] = (-math.exp(-0.5)) * jax.nn.sigmoid(zw)
        gg_ref[:, c] = _dot(xg, wg_ref[:, c]).astype(gg_ref.dtype)

    def gelu_tile():
        for dst, src in pieces(0):
            zu_ref[:, dst] = _gelu(_dot(a_tile[...], w_ref[:, src])).astype(BF16)

    def layernorm_tile():
        zv = _gelu(_dot(a_tile[...], w_ref[:, tile(1)]))
        mu = jnp.mean(zv, axis=-1, keepdims=True)
        zc = zv - mu
        var = jnp.mean(zc * zc, axis=-1, keepdims=True)
        vn_ref[...] = ((zc * lax.rsqrt(var + LN_EPS)) * lnw_ref[...]
                       + lnb_ref[...]).astype(BF16)

    def main_tile(t):
        for dst, src in pieces(2 + t):
            if t < n_mix:
                p = _shift_mix_tile(a_scr, w_ref, sb_ref, first, src)
            else:
                p = _dot(a_tile[...], w_ref[:, src])
            if t == key_tile:
                aa = jax.nn.sigmoid(a0[:, dst] + _dot(xa, wa_ref[:, dst]))
                aa_ref[:, dst] = aa.astype(aa_ref.dtype)
                kraw = p * k_k[:, dst]
                ss = _dot((kraw * kraw).astype(BF16), hb_ref[...])
                kk_ref[:, dst] = (kraw * lax.rsqrt(jnp.maximum(ss, 1e-24))).astype(kk_ref.dtype)
                p = p * ((1.0 - k_a[:, dst]) + aa * k_a[:, dst])
            p_ref[:, slice(t * d + dst.start, t * d + dst.stop)] = p.astype(p_ref.dtype)

    plain = list(range(n_mix, n_main))
    heavy = [gelu_tile, layernorm_tile] + [functools.partial(main_tile, t) for t in range(n_mix)]
    for step in range(max(len(plain), len(heavy))):
        if step < len(plain):
            main_tile(plain[step])
        if step < len(heavy):
            heavy[step]()


def _in_proj(x2, g, w16, sb, ln_w, ln_b, ww, wa, wg, vecs, hb, seq, tm, n_mix, key_tile, n_lora):
    m, d = x2.shape
    n = w16.shape[1]
    n_main = n // d - 3
    assert seq % tm == 0 and n % d == 0 and hb.shape[0] == MXU_WIDTH
    const = lambda a: pl.BlockSpec(a.shape, lambda i: (0, 0), pipeline_mode=pl.Buffered(1))
    rows = lambda w: pl.BlockSpec((tm, w), lambda i: (i, 0))
    act = lambda dt: jax.ShapeDtypeStruct((m, d), dt)
    return pl.pallas_call(
        functools.partial(_in_proj_kernel, tiles_per_seq=seq // tm, n_mix=n_mix,
                          key_tile=key_tile, n_lora=n_lora),
        grid=(m // tm,),
        in_specs=[
            rows(d),
            pl.BlockSpec((PREV_ROWS, d), lambda i: (jnp.maximum(i * (tm // PREV_ROWS) - 1, 0), 0)),
            const(g), const(w16), const(sb), const(ln_w), const(ln_b),
            const(ww), const(wa), const(wg), const(vecs), const(hb),
        ],
        out_specs=[rows(d), rows(d), rows(n_main * d), rows(d), rows(d), rows(d), rows(d)],
        out_shape=[act(BF16), act(BF16), jax.ShapeDtypeStruct((m, n_main * d), BF16),
                   act(F32), act(BF16), act(BF16), act(BF16)],
        scratch_shapes=[pltpu.VMEM((PREV_ROWS + tm, d), BF16)],
        compiler_params=_cparams(("parallel",)),
        name="in_proj",
    )(x2, x2, g, w16, sb, ln_w, ln_b, ww, wa, wg, vecs, hb)


def _sgu_tile(zu_ref, vn_ref, ga_ref, bias_ref, wpa_ref, wm_scr, s_scr):
    ch = SGU_CHUNK
    dg = LANES
    for c in range(zu_ref.shape[0] // ch):
        rows = pl.ds(c * ch, ch)
        for g in range(SGU_GROUPS):
            cols = slice(g * dg, (g + 1) * dg)
            sv = _dot(wm_scr[g], vn_ref[rows, cols]) + bias_ref[:, cols]
            s_scr[rows, cols] = (zu_ref[rows, cols] * sv).astype(BF16)
    return jax.nn.sigmoid(ga_ref[...].astype(F32)) * _dot(s_scr[...], wpa_ref[...])


def _scan_kernel(r_ref, k_ref, v_ref, lw_ref, kk_ref, aa_ref, o_ref, s_scr, *, chunk, hd, gw):
    c_len = chunk
    assert c_len == hd
    bb, tblk, d = r_ref.shape
    n_groups = d // gw
    hpg = gw // hd
    n_double = int(math.log2(c_len)) - 1

    rowi = lax.broadcasted_iota(jnp.int32, (c_len, gw), 0)
    sub = lax.broadcasted_iota(jnp.int32, (c_len, gw), 1) % c_len
    strict = sub < rowi
    incl = sub <= rowi
    eye = jnp.where(sub == rowi, 1.0, 0.0).astype(F32)
    bd_mask = (lax.broadcasted_iota(jnp.int32, (gw, gw), 0) // c_len
               == lax.broadcasted_iota(jnp.int32, (gw, gw), 1) // hd)
    ltri = jnp.where(lax.broadcasted_iota(jnp.int32, (c_len, c_len), 1)
                     <= lax.broadcasted_iota(jnp.int32, (c_len, c_len), 0), 1.0, 0.0).astype(BF16)

    @pl.when(pl.program_id(1) == 0)
    def _():
        s_scr[...] = jnp.zeros_like(s_scr)

    def bd(x16):
        t = jnp.concatenate([x16] * hpg, axis=0)
        return jnp.where(bd_mask, t, jnp.zeros_like(t))

    chains = [(b, g) for b in range(bb) for g in range(n_groups)]
    n = range(len(chains))

    def body(c, carry):
        rows = pl.ds(pl.multiple_of(c * c_len, c_len), c_len)

        def at(ref, i):
            b, g = chains[i]
            return ref.at[b, rows, g * gw:(g + 1) * gw]

        ar, bq, kq, v16, bk, g_tot = [], [], [], [], [], []
        for i in n:
            k = at(k_ref, i)[...].astype(F32)
            lw = at(lw_ref, i)[...]
            kk = at(kk_ref, i)[...].astype(F32)
            b = kk * at(aa_ref, i)[...].astype(F32)
            lh, ll = _split2(lw)
            cum = _dot(ltri, lh) + _dot(ltri, ll)
            cum_last = cum[c_len - 1:c_len, :]
            g_inv = jnp.exp(-cum)
            g_end = jnp.exp(cum_last - cum)
            aq = -(kk * jnp.exp(cum - lw))
            rq = at(r_ref, i)[...].astype(F32) * jnp.exp(cum)
            ar.append(jnp.concatenate([aq, rq], axis=0).astype(BF16))
            bq.append((b * g_inv).astype(BF16))
            kq.append((k * g_inv).astype(BF16))
            v16.append(at(v_ref, i)[...].astype(BF16))
            bk.append(jnp.concatenate([b * g_end, k * g_end], axis=0).astype(BF16))
            g_tot.append(jnp.exp(cum_last))

        pb = [_dot(ar[i], bd(bq[i]), _NT) for i in n]
        pk = [_dot(ar[i], bd(kq[i]), _NT) for i in n]
        s0 = [s_scr[chains[i]] for i in n]
        ar_s = [_dot(ar[i], s0[i].astype(BF16), _NT) for i in n]
        a_ab = [jnp.where(strict, pb[i][:c_len], 0.0) for i in n]
        a_rb = [jnp.where(incl, pb[i][c_len:], 0.0).astype(BF16) for i in n]
        akv = [_dot(jnp.concatenate([jnp.where(strict, pk[i][:c_len], 0.0),
                                     jnp.where(incl, pk[i][c_len:], 0.0)], axis=0).astype(BF16),
                    bd(v16[i])) for i in n]
        rhs = [ar_s[i][:c_len] + akv[i][:c_len] for i in n]

        p = [_dot(a_ab[i].astype(BF16), bd(a_ab[i].astype(BF16))) for i in n]
        x = [eye + a_ab[i] for i in n]
        for _ in range(n_double - 1):
            xp = [_dot(jnp.concatenate([x[i], p[i]], axis=0).astype(BF16), bd(p[i].astype(BF16)))
                  for i in n]
            x = [x[i] + xp[i][:c_len] for i in n]
            p = [xp[i][c_len:] for i in n]
        x = [x[i] + _dot(x[i].astype(BF16), bd(p[i].astype(BF16))) for i in n]

        sa16 = [_dot(x[i].astype(BF16), bd(rhs[i].astype(BF16))).astype(BF16) for i in n]
        for i in n:
            at(o_ref, i)[...] = ((ar_s[i][c_len:] + akv[i][c_len:])
                                 + _dot(a_rb[i], bd(sa16[i]))).astype(o_ref.dtype)
        for i in n:
            upd = _dot(jnp.concatenate([sa16[i], v16[i]], axis=0), bk[i], _TN)
            s_scr[chains[i]] = jnp.where(bd_mask, s0[i] * g_tot[i] + upd, 0.0)
        return carry

    lax.fori_loop(0, tblk // c_len, body, 0)


def _rwkv_scan(p_main, lw, kk, aa, batch, seq, d, bb, tblk):
    gw = SCAN_GROUP_HEADS * RWKV_HEAD
    spec = pl.BlockSpec((bb, tblk, d), lambda b, t: (b, t, 0))
    col = lambda cb: pl.BlockSpec((bb, tblk, d), lambda b, t: (b, t, cb))
    as3 = lambda a: a.reshape(batch, seq, a.shape[-1])
    out = pl.pallas_call(
        functools.partial(_scan_kernel, chunk=SCAN_CHUNK, hd=RWKV_HEAD, gw=gw),
        grid=(batch // bb, seq // tblk),
        in_specs=[col(0), col(1), col(2), spec, spec, spec],
        out_specs=spec,
        out_shape=jax.ShapeDtypeStruct((batch, seq, d), BF16),
        scratch_shapes=[pltpu.VMEM((bb, d // gw, gw, gw), F32)],
        compiler_params=_cparams(("parallel", "arbitrary")),
        name="rwkv_scan",
    )(as3(p_main), as3(p_main), as3(p_main), as3(lw), as3(kk), as3(aa))
    return out.reshape(batch * seq, d)


def _mix_kernel(zu_ref, vn_ref, ga_ref, o_ref, r_ref, k_ref, v_ref, g_ref, gb_ref, x_ref,
                ws_ref, bias_ref, vec_ref, hb_ref, wpa_ref, wpb_ref, wout_ref,
                h_ref, f_ref, wm_scr, s_scr, *, hd):
    @pl.when(pl.program_id(0) == 0)
    def _():
        ch = SGU_CHUNK
        row = lax.broadcasted_iota(jnp.int32, (ch, ch), 0)
        col = lax.broadcasted_iota(jnp.int32, (ch, ch), 1)
        for g in range(SGU_GROUPS):
            wm_scr[g] = jnp.where(col <= row, ws_ref[g], 0.0).astype(BF16)

    ya = _sgu_tile(zu_ref, vn_ref, ga_ref, bias_ref, wpa_ref, wm_scr, s_scr)

    lnx_w = vec_ref[0:1, :]
    lnx_b = vec_ref[1:2, :]
    r_k = vec_ref[2:3, :]
    g_ffn = vec_ref[3:4, :]
    o = o_ref[...].astype(F32)
    inv_n = 1.0 / hd
    mu = _head_sum(o, hb_ref) * inv_n
    oc = o - mu
    var = _head_sum(oc * oc, hb_ref) * inv_n
    on = (oc * lax.rsqrt(var + GN_EPS)) * lnx_w + lnx_b
    v = v_ref[...].astype(F32)
    rk = r_ref[...].astype(F32) * k_ref[...].astype(F32)
    bonus = _head_sum(rk * r_k, hb_ref) * v
    yb = _dot(((on + bonus) * g_ref[...]).astype(BF16), wpb_ref[...])
    mixed = ya + jax.nn.sigmoid(gb_ref[...].astype(F32)) * yb
    h = x_ref[...] + _dot(mixed.astype(BF16), wout_ref[...])
    h_ref[...] = h
    f_ref[...] = _rms_rows(h, g_ffn).astype(BF16)


def _mix(zu, vn, o, g, p_main, x2, sgu_w, bias_full, vecs, hb, wpa16, wpb16, wout16, tm, d):
    m = x2.shape[0]
    ch = SGU_CHUNK
    tile = pl.BlockSpec((tm, d), lambda i: (i, 0))
    col = lambda cb: pl.BlockSpec((tm, d), lambda i: (i, cb))
    const = lambda a: pl.BlockSpec(a.shape, lambda i: (0,) * a.ndim,
                                   pipeline_mode=pl.Buffered(1))
    return pl.pallas_call(
        functools.partial(_mix_kernel, hd=RWKV_HEAD),
        grid=(m // tm,),
        in_specs=[tile, tile, col(3), tile, col(0), col(1), col(2), tile, col(4), tile,
                  const(sgu_w), const(bias_full), const(vecs), const(hb),
                  const(wpa16), const(wpb16), const(wout16)],
        out_specs=[tile, tile],
        out_shape=[jax.ShapeDtypeStruct((m, d), F32), jax.ShapeDtypeStruct((m, d), BF16)],
        scratch_shapes=[pltpu.VMEM((SGU_GROUPS, ch, ch), BF16), pltpu.VMEM((tm, d), BF16)],
        compiler_params=_cparams(("arbitrary",)),
        name="mix",
    )(zu, vn, p_main, o, p_main, p_main, p_main, g, p_main, x2,
      sgu_w, bias_full, vecs, hb, wpa16, wpb16, wout16)


def _ffn_kernel(h_ref, f_ref, w1_ref, w2_ref, gf_ref, o_ref, *, tf):
    acc = h_ref[...]
    for j in range(w1_ref.shape[1] // tf):
        cols = slice(j * tf, (j + 1) * tf)
        t = jnp.maximum(_dot(f_ref[...], w1_ref[:, cols]), 0.0)
        acc = acc + _dot((t * t).astype(BF16), w2_ref[cols, :])
    o_ref[...] = _rms_rows(acc, gf_ref[...])


def _ffn(h, f, w1_16, w2_16, g_final, tm, tf):
    m, d = h.shape
    dff = w1_16.shape[1]
    assert dff % tf == 0
    tile = pl.BlockSpec((tm, d), lambda i: (i, 0))
    const = lambda shape: pl.BlockSpec(shape, lambda i: (0, 0), pipeline_mode=pl.Buffered(1))
    return pl.pallas_call(
        functools.partial(_ffn_kernel, tf=tf),
        grid=(m // tm,),
        in_specs=[tile, tile, const((d, dff)), const((dff, d)), const((1, d))],
        out_specs=tile,
        out_shape=jax.ShapeDtypeStruct((m, d), F32),
        compiler_params=_cparams(("parallel",)),
        name="ffn",
    )(h, f, w1_16, w2_16, g_final)


def _pad_cols(a, n):
    return jnp.pad(a, ((0, 0), (0, n - a.shape[1])))


def _pad_rows(a, n):
    return jnp.pad(a, ((0, n - a.shape[0]), (0, 0)))


def _layer(x2, batch, seq, g_mix, w_in, sgu_ln_w, sgu_ln_b, sgu_w, sgu_b, w_proj_a, shift_b,
           w_lora_w, w0, a_lora_w, a0, g_lora_w, k_k, k_a, r_k, ln_x_w, ln_x_b, w_proj_b,
           w_out, g_ffn, w_ffn1, w_ffn2, g_out):
    d = x2.shape[1]
    lora_w, lora_a, lora_g = w_lora_w.shape[0], a_lora_w.shape[0], g_lora_w.shape[0]
    c_sgu = 2 * d
    c_rkv = 3 * d
    c_lora = lora_w + lora_a + lora_g
    o_lora = c_sgu + c_rkv
    o_gate = o_lora + c_lora
    pw, pa = LANES, LANES
    pg = -(-lora_g // LANES) * LANES

    n_lora = pw + pa + pg
    w16 = w_in.astype(BF16)
    w_all = jnp.concatenate([
        w16[:, :o_lora], w16[:, o_gate:],
        _pad_cols(w16[:, o_lora:o_lora + lora_w], pw),
        _pad_cols(w16[:, o_lora + lora_w:o_lora + lora_w + lora_a], pa),
        _pad_cols(w16[:, o_lora + lora_w + lora_a:o_gate], pg + d - n_lora)], axis=1)
    sb_lo = shift_b[:, c_rkv:]
    sb_all = jnp.concatenate([
        jnp.zeros((2, c_sgu), F32), shift_b[:, :c_rkv], jnp.zeros((2, 2 * d), F32),
        _pad_cols(sb_lo[:, :lora_w], pw),
        _pad_cols(sb_lo[:, lora_w:lora_w + lora_a], pa),
        _pad_cols(sb_lo[:, lora_w + lora_a:], pg + d - n_lora)], axis=1)

    gw = SCAN_GROUP_HEADS * RWKV_HEAD
    hb = (lax.broadcasted_iota(jnp.int32, (gw, gw), 0) // RWKV_HEAD
          == lax.broadcasted_iota(jnp.int32, (gw, gw), 1) // RWKV_HEAD).astype(BF16)
    vec_prep = jnp.stack([w0, a0, k_k, k_a])
    lo16 = lambda w, rows: _pad_rows(w, rows).astype(BF16)
    zu, vn, p_main, lw, kk, aa, g = _in_proj(
        x2, g_mix.reshape(1, d), w_all, sb_all, sgu_ln_w.reshape(1, d), sgu_ln_b.reshape(1, d),
        lo16(w_lora_w, pw), lo16(a_lora_w, pa), lo16(g_lora_w, pg), vec_prep, hb, seq,
        tm=IN_PROJ_ROWS, n_mix=c_rkv // d, key_tile=1, n_lora=n_lora)

    bias_full = jnp.repeat(sgu_b.T, d // SGU_GROUPS, axis=1)

    o = _rwkv_scan(p_main, lw, kk, aa, batch, seq, d, bb=SCAN_SEQS, tblk=SCAN_ROWS)

    vec_post = jnp.stack([ln_x_w, ln_x_b, r_k, g_ffn])
    h1, f = _mix(zu, vn, o, g, p_main, x2, sgu_w, bias_full, vec_post, hb,
                 w_proj_a.astype(BF16), w_proj_b.astype(BF16), w_out.astype(BF16),
                 tm=MIX_ROWS, d=d)
    return _ffn(h1, f, w_ffn1.astype(BF16), w_ffn2.astype(BF16), g_out.reshape(1, d),
                tm=FFN_ROWS, tf=FFN_COLS)


def kernel(x, g_mix, w_in, sgu_ln_w, sgu_ln_b, sgu_w, sgu_b, w_proj_a, shift_b, w_lora_w, w0,
           a_lora_w, a0, g_lora_w, k_k, k_a, r_k, ln_x_w, ln_x_b, w_proj_b, w_out, g_ffn,
           w_ffn1, w_ffn2, g_final):
    batch, seq, d = x.shape
    depth = w_in.shape[0]
    assert depth == 1, "the final RMSNorm is fused into the single layer's ffn call"
    h = x.reshape(batch * seq, d)
    l = 0
    h = _layer(h, batch, seq, g_mix[l], w_in[l], sgu_ln_w[l], sgu_ln_b[l], sgu_w[l], sgu_b[l],
               w_proj_a[l], shift_b[l], w_lora_w[l], w0[l], a_lora_w[l], a0[l], g_lora_w[l],
               k_k[l], k_a[l], r_k[l], ln_x_w[l], ln_x_b[l], w_proj_b[l], w_out[l], g_ffn[l],
               w_ffn1[l], w_ffn2[l], g_final)
    return h.reshape(batch, seq, d)
```

```python
import functools
import math

import jax
import jax.numpy as jnp
from jax import lax
from jax.experimental import pallas as pl
from jax.experimental.pallas import tpu as pltpu

F32 = jnp.float32
BF16 = jnp.bfloat16

SGU_CHUNK = 128
SGU_GROUPS = 8
RWKV_HEAD = 64
NORM_EPS = 1e-6
LN_EPS = 1e-5
GN_EPS = 64e-5

LANES = 128
MXU_WIDTH = 256
PREV_ROWS = 16
LN_ROW_CHUNKS = 2
SCAN_CHUNK = 64
SCAN_GROUP_HEADS = MXU_WIDTH // RWKV_HEAD
VMEM_LIMIT = 56 * 1024 * 1024

IN_PROJ_ROWS = 512
SCAN_SEQS = 4
SCAN_ROWS = 256
MIX_ROWS = 512
FFN_ROWS = 1024
FFN_COLS = 1024


def _cparams(sem):
    return pltpu.CompilerParams(dimension_semantics=sem, vmem_limit_bytes=VMEM_LIMIT)


def _dot(a, b, dims=(((1,), (0,)), ((), ()))):
    return lax.dot_general(a, b, dims, preferred_element_type=F32)


_NT = (((1,), (1,)), ((), ()))
_TN = (((0,), (0,)), ((), ()))


def _split2(x):
    hi = x.astype(BF16)
    lo = (x - hi.astype(F32)).astype(BF16)
    return hi, lo


def _head_sum(x, hb_ref):
    w = hb_ref.shape[0]
    hb = hb_ref[...]
    return jnp.concatenate(
        [_dot(x[:, c * w:(c + 1) * w].astype(BF16), hb) for c in range(x.shape[1] // w)], axis=1)


def _rms_rows(x, g):
    ms = jnp.mean(x * x, axis=-1, keepdims=True)
    return (x * lax.rsqrt(ms + NORM_EPS)) * g


def _gelu(x):
    return 0.5 * x * (1.0 + lax.erf(x * (1.0 / math.sqrt(2.0))))


def _shift_mix_tile(a_scr, w_ref, sb_ref, first, cols):
    pa = _dot(a_scr[...], w_ref[:, cols])
    p = pa[PREV_ROWS:]
    prev_row = jnp.where(first, 0.0, pa[PREV_ROWS - 1:PREV_ROWS, :])
    row = lax.broadcasted_iota(jnp.int32, p.shape, 0)
    shifted = jnp.where(row == 0, prev_row, pltpu.roll(p, 1, 0))
    return p * sb_ref[0:1, cols] + shifted * sb_ref[1:2, cols]


def _in_proj_kernel(x_ref, xp_ref, g_ref, w_ref, sb_ref, lnw_ref, lnb_ref,
                    ww_ref, wa_ref, wg_ref, vec_ref, hb_ref,
                    zu_ref, vn_ref, p_ref, lw_ref, kk_ref, aa_ref, gg_ref, a_scr, *,
                    tiles_per_seq, n_mix, key_tile, n_lora):
    d = x_ref.shape[1]
    n_main = p_ref.shape[1] // d
    first = (pl.program_id(0) % tiles_per_seq) == 0
    tile = lambda t: slice(t * d, (t + 1) * d)
    w0, a0, k_k, k_a = (vec_ref[r:r + 1, :] for r in range(4))

    tm = x_ref.shape[0]
    a_scr[0:PREV_ROWS, :] = _rms_rows(xp_ref[...], g_ref[...]).astype(BF16)
    a_scr[PREV_ROWS:, :] = _rms_rows(x_ref[...], g_ref[...]).astype(BF16)
    a_tile = a_scr.at[pl.ds(PREV_ROWS, tm), :]

    def pieces(t):
        return [(slice(s, s + MXU_WIDTH), slice(t * d + s, t * d + s + MXU_WIDTH))
                for s in range(0, d, MXU_WIDTH)]

    lo0 = (2 + n_main) * d
    lo = _shift_mix_tile(a_scr, w_ref, sb_ref, first, slice(lo0, lo0 + n_lora))
    xw = jnp.tanh(lo[:, 0:LANES]).astype(BF16)
    xa = lo[:, LANES:2 * LANES].astype(BF16)
    xg = jax.nn.sigmoid(lo[:, 2 * LANES:]).astype(BF16)
    for c, _ in pieces(0):
        zw = w0[:, c] + _dot(xw, ww_ref[:, c])
        lw_ref[:, c] = (-math.exp(-0.5)) * jax.nn.sigmoid(zw)
        gg_ref[:, c] = _dot(xg, wg_ref[:, c]).astype(gg_ref.dtype)

    def gelu_tile():
        for dst, src in pieces(0):
            zu_ref[:, dst] = _gelu(_dot(a_tile[...], w_ref[:, src])).astype(BF16)

    def layernorm_tile():
        rc = tm // LN_ROW_CHUNKS
        for c in range(LN_ROW_CHUNKS):
            zv = _gelu(_dot(a_scr[pl.ds(PREV_ROWS + c * rc, rc), :], w_ref[:, tile(1)]))
            mu = jnp.mean(zv, axis=-1, keepdims=True)
            zc = zv - mu
            var = jnp.mean(zc * zc, axis=-1, keepdims=True)
            vn_ref[pl.ds(c * rc, rc), :] = ((zc * lax.rsqrt(var + LN_EPS)) * lnw_ref[...]
                                            + lnb_ref[...]).astype(BF16)

    def main_tile(t):
        for dst, src in pieces(2 + t):
            if t < n_mix:
                p = _shift_mix_tile(a_scr, w_ref, sb_ref, first, src)
            else:
                p = _dot(a_tile[...], w_ref[:, src])
            if t == key_tile:
                aa = jax.nn.sigmoid(a0[:, dst] + _dot(xa, wa_ref[:, dst]))
                aa_ref[:, dst] = aa.astype(aa_ref.dtype)
                kraw = p * k_k[:, dst]
                ss = _dot((kraw * kraw).astype(BF16), hb_ref[...])
                kk_ref[:, dst] = (kraw * lax.rsqrt(jnp.maximum(ss, 1e-24))).astype(kk_ref.dtype)
                p = p * ((1.0 - k_a[:, dst]) + aa * k_a[:, dst])
            p_ref[:, slice(t * d + dst.start, t * d + dst.stop)] = p.astype(p_ref.dtype)

    plain = list(range(n_mix, n_main))
    heavy = [gelu_tile, layernorm_tile] + [functools.partial(main_tile, t) for t in range(n_mix)]
    for step in range(max(len(plain), len(heavy))):
        if step < len(plain):
            main_tile(plain[step])
        if step < len(heavy):
            heavy[step]()


def _in_proj(x2, g, w16, sb, ln_w, ln_b, ww, wa, wg, vecs, hb, seq, tm, n_mix, key_tile, n_lora):
    m, d = x2.shape
    n = w16.shape[1]
    n_main = n // d - 3
    assert seq % tm == 0 and n % d == 0 and hb.shape[0] == MXU_WIDTH
    const = lambda a: pl.BlockSpec(a.shape, lambda i: (0, 0), pipeline_mode=pl.Buffered(1))
    rows = lambda w: pl.BlockSpec((tm, w), lambda i: (i, 0))
    act = lambda dt: jax.ShapeDtypeStruct((m, d), dt)
    return pl.pallas_call(
        functools.partial(_in_proj_kernel, tiles_per_seq=seq // tm, n_mix=n_mix,
                          key_tile=key_tile, n_lora=n_lora),
        grid=(m // tm,),
        in_specs=[
            rows(d),
            pl.BlockSpec((PREV_ROWS, d), lambda i: (jnp.maximum(i * (tm // PREV_ROWS) - 1, 0), 0)),
            const(g), const(w16), const(sb), const(ln_w), const(ln_b),
            const(ww), const(wa), const(wg), const(vecs), const(hb),
        ],
        out_specs=[rows(d), rows(d), rows(n_main * d), rows(d), rows(d), rows(d), rows(d)],
        out_shape=[act(BF16), act(BF16), jax.ShapeDtypeStruct((m, n_main * d), BF16),
                   act(F32), act(BF16), act(BF16), act(BF16)],
        scratch_shapes=[pltpu.VMEM((PREV_ROWS + tm, d), BF16)],
        compiler_params=_cparams(("parallel",)),
        name="in_proj",
    )(x2, x2, g, w16, sb, ln_w, ln_b, ww, wa, wg, vecs, hb)


def _sgu_tile(zu_ref, vn_ref, ga_ref, bias_ref, wpa_ref, wm_scr, s_scr):
    ch = SGU_CHUNK
    dg = LANES
    for c in range(zu_ref.shape[0] // ch):
        rows = pl.ds(c * ch, ch)
        for g in range(SGU_GROUPS):
            cols = slice(g * dg, (g + 1) * dg)
            sv = _dot(wm_scr[g], vn_ref[rows, cols]) + bias_ref[:, cols]
            s_scr[rows, cols] = (zu_ref[rows, cols] * sv).astype(BF16)
    return jax.nn.sigmoid(ga_ref[...].astype(F32)) * _dot(s_scr[...], wpa_ref[...])


def _scan_kernel(r_ref, k_ref, v_ref, lw_ref, kk_ref, aa_ref, o_ref, s_scr, *, chunk, hd, gw):
    c_len = chunk
    assert c_len == hd
    bb, tblk, d = r_ref.shape
    n_groups = d // gw
    hpg = gw // hd
    n_double = int(math.log2(c_len)) - 1

    rowi = lax.broadcasted_iota(jnp.int32, (c_len, gw), 0)
    sub = lax.broadcasted_iota(jnp.int32, (c_len, gw), 1) % c_len
    strict = sub < rowi
    incl = sub <= rowi
    eye = jnp.where(sub == rowi, 1.0, 0.0).astype(F32)
    bd_mask = (lax.broadcasted_iota(jnp.int32, (gw, gw), 0) // c_len
               == lax.broadcasted_iota(jnp.int32, (gw, gw), 1) // hd)
    ltri = jnp.where(lax.broadcasted_iota(jnp.int32, (c_len, c_len), 1)
                     <= lax.broadcasted_iota(jnp.int32, (c_len, c_len), 0), 1.0, 0.0).astype(BF16)

    @pl.when(pl.program_id(1) == 0)
    def _():
        s_scr[...] = jnp.zeros_like(s_scr)

    def bd(x16):
        t = jnp.concatenate([x16] * hpg, axis=0)
        return jnp.where(bd_mask, t, jnp.zeros_like(t))

    chains = [(b, g) for b in range(bb) for g in range(n_groups)]
    n = range(len(chains))

    def body(c, carry):
        rows = pl.ds(pl.multiple_of(c * c_len, c_len), c_len)

        def at(ref, i):
            b, g = chains[i]
            return ref.at[b, rows, g * gw:(g + 1) * gw]

        ar, bq, kq, v16, bk, g_tot = [], [], [], [], [], []
        for i in n:
            k = at(k_ref, i)[...].astype(F32)
            lw = at(lw_ref, i)[...]
            kk = at(kk_ref, i)[...].astype(F32)
            b = kk * at(aa_ref, i)[...].astype(F32)
            lh, ll = _split2(lw)
            cum = _dot(ltri, lh) + _dot(ltri, ll)
            cum_last = cum[c_len - 1:c_len, :]
            g_inv = jnp.exp(-cum)
            g_end = jnp.exp(cum_last - cum)
            aq = -(kk * jnp.exp(cum - lw))
            rq = at(r_ref, i)[...].astype(F32) * jnp.exp(cum)
            ar.append(jnp.concatenate([aq, rq], axis=0).astype(BF16))
            bq.append((b * g_inv).astype(BF16))
            kq.append((k * g_inv).astype(BF16))
            v16.append(at(v_ref, i)[...].astype(BF16))
            bk.append(jnp.concatenate([b * g_end, k * g_end], axis=0).astype(BF16))
            g_tot.append(jnp.exp(cum_last))

        pb = [_dot(ar[i], bd(bq[i]), _NT) for i in n]
        pk = [_dot(ar[i], bd(kq[i]), _NT) for i in n]
        s0 = [s_scr[chains[i]] for i in n]
        ar_s = [_dot(ar[i], s0[i].astype(BF16), _NT) for i in n]
        a_ab = [jnp.where(strict, pb[i][:c_len], 0.0) for i in n]
        a_rb = [jnp.where(incl, pb[i][c_len:], 0.0).astype(BF16) for i in n]
        akv = [_dot(jnp.concatenate([jnp.where(strict, pk[i][:c_len], 0.0),
                                     jnp.where(incl, pk[i][c_len:], 0.0)], axis=0).astype(BF16),
                    bd(v16[i])) for i in n]
        rhs = [ar_s[i][:c_len] + akv[i][:c_len] for i in n]

        p = [_dot(a_ab[i].astype(BF16), bd(a_ab[i].astype(BF16))) for i in n]
        x = [eye + a_ab[i] for i in n]
        for _ in range(n_double - 1):
            xp = [_dot(jnp.concatenate([x[i], p[i]], axis=0).astype(BF16), bd(p[i].astype(BF16)))
                  for i in n]
            x = [x[i] + xp[i][:c_len] for i in n]
            p = [xp[i][c_len:] for i in n]
        x = [x[i] + _dot(x[i].astype(BF16), bd(p[i].astype(BF16))) for i in n]

        sa16 = [_dot(x[i].astype(BF16), bd(rhs[i].astype(BF16))).astype(BF16) for i in n]
        for i in n:
            at(o_ref, i)[...] = ((ar_s[i][c_len:] + akv[i][c_len:])
                                 + _dot(a_rb[i], bd(sa16[i]))).astype(o_ref.dtype)
        for i in n:
            upd = _dot(jnp.concatenate([sa16[i], v16[i]], axis=0), bk[i], _TN)
            s_scr[chains[i]] = jnp.where(bd_mask, s0[i] * g_tot[i] + upd, 0.0)
        return carry

    lax.fori_loop(0, tblk // c_len, body, 0)


def _rwkv_scan(p_main, lw, kk, aa, batch, seq, d, bb, tblk):
    gw = SCAN_GROUP_HEADS * RWKV_HEAD
    spec = pl.BlockSpec((bb, tblk, d), lambda b, t: (b, t, 0))
    col = lambda cb: pl.BlockSpec((bb, tblk, d), lambda b, t: (b, t, cb))
    as3 = lambda a: a.reshape(batch, seq, a.shape[-1])
    out = pl.pallas_call(
        functools.partial(_scan_kernel, chunk=SCAN_CHUNK, hd=RWKV_HEAD, gw=gw),
        grid=(batch // bb, seq // tblk),
        in_specs=[col(0), col(1), col(2), spec, spec, spec],
        out_specs=spec,
        out_shape=jax.ShapeDtypeStruct((batch, seq, d), BF16),
        scratch_shapes=[pltpu.VMEM((bb, d // gw, gw, gw), F32)],
        compiler_params=_cparams(("parallel", "arbitrary")),
        name="rwkv_scan",
    )(as3(p_main), as3(p_main), as3(p_main), as3(lw), as3(kk), as3(aa))
    return out.reshape(batch * seq, d)


def _mix_kernel(zu_ref, vn_ref, ga_ref, o_ref, r_ref, k_ref, v_ref, g_ref, gb_ref, x_ref,
                ws_ref, bias_ref, vec_ref, hb_ref, wpa_ref, wpb_ref, wout_ref,
                h_ref, f_ref, wm_scr, s_scr, *, hd):
    @pl.when(pl.program_id(0) == 0)
    def _():
        ch = SGU_CHUNK
        row = lax.broadcasted_iota(jnp.int32, (ch, ch), 0)
        col = lax.broadcasted_iota(jnp.int32, (ch, ch), 1)
        for g in range(SGU_GROUPS):
            wm_scr[g] = jnp.where(col <= row, ws_ref[g], 0.0).astype(BF16)

    ya = _sgu_tile(zu_ref, vn_ref, ga_ref, bias_ref, wpa_ref, wm_scr, s_scr)

    lnx_w = vec_ref[0:1, :]
    lnx_b = vec_ref[1:2, :]
    r_k = vec_ref[2:3, :]
    g_ffn = vec_ref[3:4, :]
    o = o_ref[...].astype(F32)
    inv_n = 1.0 / hd
    mu = _head_sum(o, hb_ref) * inv_n
    oc = o - mu
    var = _head_sum(oc * oc, hb_ref) * inv_n
    on = (oc * lax.rsqrt(var + GN_EPS)) * lnx_w + lnx_b
    v = v_ref[...].astype(F32)
    rk = r_ref[...].astype(F32) * k_ref[...].astype(F32)
    bonus = _head_sum(rk * r_k, hb_ref) * v
    yb = _dot(((on + bonus) * g_ref[...]).astype(BF16), wpb_ref[...])
    mixed = ya + jax.nn.sigmoid(gb_ref[...].astype(F32)) * yb
    h = x_ref[...] + _dot(mixed.astype(BF16), wout_ref[...])
    h_ref[...] = h
    f_ref[...] = _rms_rows(h, g_ffn).astype(BF16)


def _mix(zu, vn, o, g, p_main, x2, sgu_w, bias_full, vecs, hb, wpa16, wpb16, wout16, tm, d):
    m = x2.shape[0]
    ch = SGU_CHUNK
    tile = pl.BlockSpec((tm, d), lambda i: (i, 0))
    col = lambda cb: pl.BlockSpec((tm, d), lambda i: (i, cb))
    const = lambda a: pl.BlockSpec(a.shape, lambda i: (0,) * a.ndim,
                                   pipeline_mode=pl.Buffered(1))
    return pl.pallas_call(
        functools.partial(_mix_kernel, hd=RWKV_HEAD),
        grid=(m // tm,),
        in_specs=[tile, tile, col(3), tile, col(0), col(1), col(2), tile, col(4), tile,
                  const(sgu_w), const(bias_full), const(vecs), const(hb),
                  const(wpa16), const(wpb16), const(wout16)],
        out_specs=[tile, tile],
        out_shape=[jax.ShapeDtypeStruct((m, d), F32), jax.ShapeDtypeStruct((m, d), BF16)],
        scratch_shapes=[pltpu.VMEM((SGU_GROUPS, ch, ch), BF16), pltpu.VMEM((tm, d), BF16)],
        compiler_params=_cparams(("arbitrary",)),
        name="mix",
    )(zu, vn, p_main, o, p_main, p_main, p_main, g, p_main, x2,
      sgu_w, bias_full, vecs, hb, wpa16, wpb16, wout16)


def _ffn_kernel(h_ref, f_ref, w1_ref, w2_ref, gf_ref, o_ref, *, tf):
    acc = h_ref[...]
    for j in range(w1_ref.shape[1] // tf):
        cols = slice(j * tf, (j + 1) * tf)
        t = jnp.maximum(_dot(f_ref[...], w1_ref[:, cols]), 0.0)
        acc = acc + _dot((t * t).astype(BF16), w2_ref[cols, :])
    o_ref[...] = _rms_rows(acc, gf_ref[...])


def _ffn(h, f, w1_16, w2_16, g_final, tm, tf):
    m, d = h.shape
    dff = w1_16.shape[1]
    assert dff % tf == 0
    tile = pl.BlockSpec((tm, d), lambda i: (i, 0))
    const = lambda shape: pl.BlockSpec(shape, lambda i: (0, 0), pipeline_mode=pl.Buffered(1))
    return pl.pallas_call(
        functools.partial(_ffn_kernel, tf=tf),
        grid=(m // tm,),
        in_specs=[tile, tile, const((d, dff)), const((dff, d)), const((1, d))],
        out_specs=tile,
        out_shape=jax.ShapeDtypeStruct((m, d), F32),
        compiler_params=_cparams(("parallel",)),
        name="ffn",
    )(h, f, w1_16, w2_16, g_final)


def _pad_cols(a, n):
    return jnp.pad(a, ((0, 0), (0, n - a.shape[1])))


def _pad_rows(a, n):
    return jnp.pad(a, ((0, n - a.shape[0]), (0, 0)))


def _layer(x2, batch, seq, g_mix, w_in, sgu_ln_w, sgu_ln_b, sgu_w, sgu_b, w_proj_a, shift_b,
           w_lora_w, w0, a_lora_w, a0, g_lora_w, k_k, k_a, r_k, ln_x_w, ln_x_b, w_proj_b,
           w_out, g_ffn, w_ffn1, w_ffn2, g_out):
    d = x2.shape[1]
    lora_w, lora_a, lora_g = w_lora_w.shape[0], a_lora_w.shape[0], g_lora_w.shape[0]
    c_sgu = 2 * d
    c_rkv = 3 * d
    c_lora = lora_w + lora_a + lora_g
    o_lora = c_sgu + c_rkv
    o_gate = o_lora + c_lora
    pw, pa = LANES, LANES
    pg = -(-lora_g // LANES) * LANES

    n_lora = pw + pa + pg
    w16 = w_in.astype(BF16)
    w_all = jnp.concatenate([
        w16[:, :o_lora], w16[:, o_gate:],
        _pad_cols(w16[:, o_lora:o_lora + lora_w], pw),
        _pad_cols(w16[:, o_lora + lora_w:o_lora + lora_w + lora_a], pa),
        _pad_cols(w16[:, o_lora + lora_w + lora_a:o_gate], pg + d - n_lora)], axis=1)
    sb_lo = shift_b[:, c_rkv:]
    sb_all = jnp.concatenate([
        jnp.zeros((2, c_sgu), F32), shift_b[:, :c_rkv], jnp.zeros((2, 2 * d), F32),
        _pad_cols(sb_lo[:, :lora_w], pw),
        _pad_cols(sb_lo[:, lora_w:lora_w + lora_a], pa),
        _pad_cols(sb_lo[:, lora_w + lora_a:], pg + d - n_lora)], axis=1)

    gw = SCAN_GROUP_HEADS * RWKV_HEAD
    hb = (lax.broadcasted_iota(jnp.int32, (gw, gw), 0) // RWKV_HEAD
          == lax.broadcasted_iota(jnp.int32, (gw, gw), 1) // RWKV_HEAD).astype(BF16)
    vec_prep = jnp.stack([w0, a0, k_k, k_a])
    lo16 = lambda w, rows: _pad_rows(w, rows).astype(BF16)
    zu, vn, p_main, lw, kk, aa, g = _in_proj(
        x2, g_mix.reshape(1, d), w_all, sb_all, sgu_ln_w.reshape(1, d), sgu_ln_b.reshape(1, d),
        lo16(w_lora_w, pw), lo16(a_lora_w, pa), lo16(g_lora_w, pg), vec_prep, hb, seq,
        tm=IN_PROJ_ROWS, n_mix=c_rkv // d, key_tile=1, n_lora=n_lora)

    bias_full = jnp.repeat(sgu_b.T, d // SGU_GROUPS, axis=1)

    o = _rwkv_scan(p_main, lw, kk, aa, batch, seq, d, bb=SCAN_SEQS, tblk=SCAN_ROWS)

    vec_post = jnp.stack([ln_x_w, ln_x_b, r_k, g_ffn])
    h1, f = _mix(zu, vn, o, g, p_main, x2, sgu_w, bias_full, vec_post, hb,
                 w_proj_a.astype(BF16), w_proj_b.astype(BF16), w_out.astype(BF16),
                 tm=MIX_ROWS, d=d)
    return _ffn(h1, f, w_ffn1.astype(BF16), w_ffn2.astype(BF16), g_out.reshape(1, d),
                tm=FFN_ROWS, tf=FFN_COLS)


def kernel(x, g_mix, w_in, sgu_ln_w, sgu_ln_b, sgu_w, sgu_b, w_proj_a, shift_b, w_lora_w, w0,
           a_lora_w, a0, g_lora_w, k_k, k_a, r_k, ln_x_w, ln_x_b, w_proj_b, w_out, g_ffn,
           w_ffn1, w_ffn2, g_final):
    batch, seq, d = x.shape
    depth = w_in.shape[0]
    assert depth == 1, "the final RMSNorm is fused into the single layer's ffn call"
    h = x.reshape(batch * seq, d)
    l = 0
    h = _layer(h, batch, seq, g_mix[l], w_in[l], sgu_ln_w[l], sgu_ln_b[l], sgu_w[l], sgu_b[l],
               w_proj_a[l], shift_b[l], w_lora_w[l], w0[l], a_lora_w[l], a0[l], g_lora_w[l],
               k_k[l], k_a[l], r_k[l], ln_x_w[l], ln_x_b[l], w_proj_b[l], w_out[l], g_ffn[l],
               w_ffn1[l], w_ffn2[l], g_final)
    return h.reshape(batch, seq, d)
```

```python
import functools
import math

import jax
import jax.numpy as jnp
from jax import lax
from jax.experimental import pallas as pl
from jax.experimental.pallas import tpu as pltpu

F32 = jnp.float32
BF16 = jnp.bfloat16

SGU_CHUNK = 128
SGU_GROUPS = 8
RWKV_HEAD = 64
NORM_EPS = 1e-6
LN_EPS = 1e-5
GN_EPS = 64e-5

LANES = 128
MXU_WIDTH = 256
PREV_ROWS = 16
LN_ROW_CHUNKS = 2
SCAN_CHUNK = 64
SCAN_GROUP_HEADS = MXU_WIDTH // RWKV_HEAD
VMEM_LIMIT = 56 * 1024 * 1024

IN_PROJ_ROWS = 512
SCAN_SEQS = 4
SCAN_ROWS = 256
MIX_ROWS = 512
FFN_ROWS = 1024
FFN_COLS = 1024


def _cparams(sem):
    return pltpu.CompilerParams(dimension_semantics=sem, vmem_limit_bytes=VMEM_LIMIT)


def _dot(a, b, dims=(((1,), (0,)), ((), ()))):
    return lax.dot_general(a, b, dims, preferred_element_type=F32)


_NT = (((1,), (1,)), ((), ()))
_TN = (((0,), (0,)), ((), ()))


def _split2(x):
    hi = x.astype(BF16)
    lo = (x - hi.astype(F32)).astype(BF16)
    return hi, lo


def _head_sum(x, hb_ref):
    w = hb_ref.shape[0]
    hb = hb_ref[...]
    return jnp.concatenate(
        [_dot(x[:, c * w:(c + 1) * w].astype(BF16), hb) for c in range(x.shape[1] // w)], axis=1)


def _rms_rows(x, g):
    ms = jnp.mean(x * x, axis=-1, keepdims=True)
    return (x * lax.rsqrt(ms + NORM_EPS)) * g


def _gelu(x):
    return 0.5 * x * (1.0 + lax.erf(x * (1.0 / math.sqrt(2.0))))


def _shift_mix_tile(a_scr, w_ref, sb_ref, first, cols):
    pa = _dot(a_scr[...], w_ref[:, cols])
    p = pa[PREV_ROWS:]
    prev_row = jnp.where(first, 0.0, pa[PREV_ROWS - 1:PREV_ROWS, :])
    row = lax.broadcasted_iota(jnp.int32, p.shape, 0)
    shifted = jnp.where(row == 0, prev_row, pltpu.roll(p, 1, 0))
    return p * sb_ref[0:1, cols] + shifted * sb_ref[1:2, cols]


def _in_proj_kernel(x_ref, xp_ref, g_ref, w_ref, sb_ref, lnw_ref, lnb_ref,
                    ww_ref, wa_ref, wg_ref, vec_ref, hb_ref,
                    zu_ref, vn_ref, p_ref, lw_ref, kk_ref, aa_ref, gg_ref, a_scr, *,
                    tiles_per_seq, n_mix, key_tile, n_lora):
    d = x_ref.shape[1]
    n_main = p_ref.shape[1] // d
    first = (pl.program_id(0) % tiles_per_seq) == 0
    tile = lambda t: slice(t * d, (t + 1) * d)
    w0, a0, k_k, k_a = (vec_ref[r:r + 1, :] for r in range(4))

    tm = x_ref.shape[0]
    a_scr[0:PREV_ROWS, :] = _rms_rows(xp_ref[...], g_ref[...]).astype(BF16)
    a_scr[PREV_ROWS:, :] = _rms_rows(x_ref[...], g_ref[...]).astype(BF16)
    a_tile = a_scr.at[pl.ds(PREV_ROWS, tm), :]

    def pieces(t):
        return [(slice(s, s + MXU_WIDTH), slice(t * d + s, t * d + s + MXU_WIDTH))
                for s in range(0, d, MXU_WIDTH)]

    lo0 = (2 + n_main) * d
    lo = _shift_mix_tile(a_scr, w_ref, sb_ref, first, slice(lo0, lo0 + n_lora))
    xw = jnp.tanh(lo[:, 0:LANES]).astype(BF16)
    xa = lo[:, LANES:2 * LANES].astype(BF16)
    xg = jax.nn.sigmoid(lo[:, 2 * LANES:]).astype(BF16)
    for c, _ in pieces(0):
        zw = w0[:, c] + _dot(xw, ww_ref[:, c])
        lw_ref[:, c] = (-math.exp(-0.5)) * jax.nn.sigmoid(zw)
        gg_ref[:, c] = _dot(xg, wg_ref[:, c]).astype(gg_ref.dtype)

    def gelu_tile():
        for dst, src in pieces(0):
            zu_ref[:, dst] = _gelu(_dot(a_tile[...], w_ref[:, src])).astype(BF16)

    def layernorm_tile():
        rc = tm // LN_ROW_CHUNKS
        for c in range(LN_ROW_CHUNKS):
            zv = _gelu(_dot(a_scr[pl.ds(PREV_ROWS + c * rc, rc), :], w_ref[:, tile(1)]))
            mu = jnp.mean(zv, axis=-1, keepdims=True)
            zc = zv - mu
            var = jnp.mean(zc * zc, axis=-1, keepdims=True)
            vn_ref[pl.ds(c * rc, rc), :] = ((zc * lax.rsqrt(var + LN_EPS)) * lnw_ref[...]
                                            + lnb_ref[...]).astype(BF16)

    def main_tile(t):
        for dst, src in pieces(2 + t):
            if t < n_mix:
                p = _shift_mix_tile(a_scr, w_ref, sb_ref, first, src)
            else:
                p = _dot(a_tile[...], w_ref[:, src])
            if t == key_tile:
                aa = jax.nn.sigmoid(a0[:, dst] + _dot(xa, wa_ref[:, dst]))
                aa_ref[:, dst] = aa.astype(aa_ref.dtype)
                kraw = p * k_k[:, dst]
                ss = _dot((kraw * kraw).astype(BF16), hb_ref[...])
                kk_ref[:, dst] = (kraw * lax.rsqrt(jnp.maximum(ss, 1e-24))).astype(kk_ref.dtype)
                p = p * ((1.0 - k_a[:, dst]) + aa * k_a[:, dst])
            p_ref[:, slice(t * d + dst.start, t * d + dst.stop)] = p.astype(p_ref.dtype)

    plain = list(range(n_mix, n_main))
    heavy = [gelu_tile, layernorm_tile] + [functools.partial(main_tile, t) for t in range(n_mix)]
    for step in range(max(len(plain), len(heavy))):
        if step < len(plain):
            main_tile(plain[step])
        if step < len(heavy):
            heavy[step]()


def _in_proj(x2, g, w16, sb, ln_w, ln_b, ww, wa, wg, vecs, hb, seq, tm, n_mix, key_tile, n_lora):
    m, d = x2.shape
    n = w16.shape[1]
    n_main = n // d - 3
    assert seq % tm == 0 and n % d == 0 and hb.shape[0] == MXU_WIDTH
    const = lambda a: pl.BlockSpec(a.shape, lambda i: (0, 0), pipeline_mode=pl.Buffered(1))
    rows = lambda w: pl.BlockSpec((tm, w), lambda i: (i, 0))
    act = lambda dt: jax.ShapeDtypeStruct((m, d), dt)
    return pl.pallas_call(
        functools.partial(_in_proj_kernel, tiles_per_seq=seq // tm, n_mix=n_mix,
                          key_tile=key_tile, n_lora=n_lora),
        grid=(m // tm,),
        in_specs=[
            rows(d),
            pl.BlockSpec((PREV_ROWS, d), lambda i: (jnp.maximum(i * (tm // PREV_ROWS) - 1, 0), 0)),
            const(g), const(w16), const(sb), const(ln_w), const(ln_b),
            const(ww), const(wa), const(wg), const(vecs), const(hb),
        ],
        out_specs=[rows(d), rows(d), rows(n_main * d), rows(d), rows(d), rows(d), rows(d)],
        out_shape=[act(BF16), act(BF16), jax.ShapeDtypeStruct((m, n_main * d), BF16),
                   act(F32), act(BF16), act(BF16), act(BF16)],
        scratch_shapes=[pltpu.VMEM((PREV_ROWS + tm, d), BF16)],
        compiler_params=_cparams(("parallel",)),
        name="in_proj",
    )(x2, x2, g, w16, sb, ln_w, ln_b, ww, wa, wg, vecs, hb)


def _sgu_tile(zu_ref, vn_ref, ga_ref, bias_ref, wpa_ref, wm_scr, s_scr):
    ch = SGU_CHUNK
    dg = LANES
    for c in range(zu_ref.shape[0] // ch):
        rows = pl.ds(c * ch, ch)
        for g in range(SGU_GROUPS):
            cols = slice(g * dg, (g + 1) * dg)
            sv = _dot(wm_scr[g], vn_ref[rows, cols]) + bias_ref[:, cols]
            s_scr[rows, cols] = (zu_ref[rows, cols] * sv).astype(BF16)
    return jax.nn.sigmoid(ga_ref[...].astype(F32)) * _dot(s_scr[...], wpa_ref[...])


def _scan_kernel(r_ref, k_ref, v_ref, lw_ref, kk_ref, aa_ref, o_ref, s_scr, *, chunk, hd, gw):
    c_len = chunk
    assert c_len == hd
    bb, tblk, d = r_ref.shape
    n_groups = d // gw
    hpg = gw // hd
    n_double = int(math.log2(c_len)) - 1

    rowi = lax.broadcasted_iota(jnp.int32, (c_len, gw), 0)
    sub = lax.broadcasted_iota(jnp.int32, (c_len, gw), 1) % c_len
    strict = sub < rowi
    incl = sub <= rowi
    eye = jnp.where(sub == rowi, 1.0, 0.0).astype(F32)
    bd_mask = (lax.broadcasted_iota(jnp.int32, (gw, gw), 0) // c_len
               == lax.broadcasted_iota(jnp.int32, (gw, gw), 1) // hd)
    ltri = jnp.where(lax.broadcasted_iota(jnp.int32, (c_len, c_len), 1)
                     <= lax.broadcasted_iota(jnp.int32, (c_len, c_len), 0), 1.0, 0.0).astype(BF16)

    @pl.when(pl.program_id(1) == 0)
    def _():
        s_scr[...] = jnp.zeros_like(s_scr)

    def bd(x16):
        t = jnp.concatenate([x16] * hpg, axis=0)
        return jnp.where(bd_mask, t, jnp.zeros_like(t))

    chains = [(b, g) for b in range(bb) for g in range(n_groups)]
    n = range(len(chains))

    def body(c, carry):
        rows = pl.ds(pl.multiple_of(c * c_len, c_len), c_len)

        def at(ref, i):
            b, g = chains[i]
            return ref.at[b, rows, g * gw:(g + 1) * gw]

        ar, bq, kq, v16, bk, g_tot = [], [], [], [], [], []
        for i in n:
            k = at(k_ref, i)[...].astype(F32)
            lw = at(lw_ref, i)[...]
            kk = at(kk_ref, i)[...].astype(F32)
            b = kk * at(aa_ref, i)[...].astype(F32)
            lh, ll = _split2(lw)
            cum = _dot(ltri, lh) + _dot(ltri, ll)
            cum_last = cum[c_len - 1:c_len, :]
            g_inv = jnp.exp(-cum)
            g_end = jnp.exp(cum_last - cum)
            aq = -(kk * jnp.exp(cum - lw))
            rq = at(r_ref, i)[...].astype(F32) * jnp.exp(cum)
            ar.append(jnp.concatenate([aq, rq], axis=0).astype(BF16))
            bq.append((b * g_inv).astype(BF16))
            kq.append((k * g_inv).astype(BF16))
            v16.append(at(v_ref, i)[...].astype(BF16))
            bk.append(jnp.concatenate([b * g_end, k * g_end], axis=0).astype(BF16))
            g_tot.append(jnp.exp(cum_last))

        pb = [_dot(ar[i], bd(bq[i]), _NT) for i in n]
        pk = [_dot(ar[i], bd(kq[i]), _NT) for i in n]
        s0 = [s_scr[chains[i]] for i in n]
        ar_s = [_dot(ar[i], s0[i].astype(BF16), _NT) for i in n]
        a_ab = [jnp.where(strict, pb[i][:c_len], 0.0) for i in n]
        a_rb = [jnp.where(incl, pb[i][c_len:], 0.0).astype(BF16) for i in n]
        akv = [_dot(jnp.concatenate([jnp.where(strict, pk[i][:c_len], 0.0),
                                     jnp.where(incl, pk[i][c_len:], 0.0)], axis=0).astype(BF16),
                    bd(v16[i])) for i in n]
        rhs = [ar_s[i][:c_len] + akv[i][:c_len] for i in n]

        p = [_dot(a_ab[i].astype(BF16), bd(a_ab[i].astype(BF16))) for i in n]
        x = [eye + a_ab[i] for i in n]
        for _ in range(n_double - 1):
            xp = [_dot(jnp.concatenate([x[i], p[i]], axis=0).astype(BF16), bd(p[i].astype(BF16)))
                  for i in n]
            x = [x[i] + xp[i][:c_len] for i in n]
            p = [xp[i][c_len:] for i in n]
        x = [x[i] + _dot(x[i].astype(BF16), bd(p[i].astype(BF16))) for i in n]

        sa16 = [_dot(x[i].astype(BF16), bd(rhs[i].astype(BF16))).astype(BF16) for i in n]
        for i in n:
            at(o_ref, i)[...] = ((ar_s[i][c_len:] + akv[i][c_len:])
                                 + _dot(a_rb[i], bd(sa16[i]))).astype(o_ref.dtype)
        for i in n:
            upd = _dot(jnp.concatenate([sa16[i], v16[i]], axis=0), bk[i], _TN)
            s_scr[chains[i]] = jnp.where(bd_mask, s0[i] * g_tot[i] + upd, 0.0)
        return carry

    lax.fori_loop(0, tblk // c_len, body, 0)


def _rwkv_scan(p_main, lw, kk, aa, batch, seq, d, bb, tblk):
    gw = SCAN_GROUP_HEADS * RWKV_HEAD
    spec = pl.BlockSpec((bb, tblk, d), lambda b, t: (b, t, 0))
    col = lambda cb: pl.BlockSpec((bb, tblk, d), lambda b, t: (b, t, cb))
    as3 = lambda a: a.reshape(batch, seq, a.shape[-1])
    out = pl.pallas_call(
        functools.partial(_scan_kernel, chunk=SCAN_CHUNK, hd=RWKV_HEAD, gw=gw),
        grid=(batch // bb, seq // tblk),
        in_specs=[col(0), col(1), col(2), spec, spec, spec],
        out_specs=spec,
        out_shape=jax.ShapeDtypeStruct((batch, seq, d), BF16),
        scratch_shapes=[pltpu.VMEM((bb, d // gw, gw, gw), F32)],
        compiler_params=_cparams(("parallel", "arbitrary")),
        name="rwkv_scan",
    )(as3(p_main), as3(p_main), as3(p_main), as3(lw), as3(kk), as3(aa))
    return out.reshape(batch * seq, d)


def _mix_kernel(zu_ref, vn_ref, ga_ref, o_ref, r_ref, k_ref, v_ref, g_ref, gb_ref, x_ref,
                ws_ref, bias_ref, vec_ref, hb_ref, wpa_ref, wpb_ref, wout_ref,
                h_ref, f_ref, wm_scr, s_scr, *, hd):
    @pl.when(pl.program_id(0) == 0)
    def _():
        ch = SGU_CHUNK
        row = lax.broadcasted_iota(jnp.int32, (ch, ch), 0)
        col = lax.broadcasted_iota(jnp.int32, (ch, ch), 1)
        for g in range(SGU_GROUPS):
            wm_scr[g] = jnp.where(col <= row, ws_ref[g], 0.0).astype(BF16)

    ya = _sgu_tile(zu_ref, vn_ref, ga_ref, bias_ref, wpa_ref, wm_scr, s_scr)

    lnx_w = vec_ref[0:1, :]
    lnx_b = vec_ref[1:2, :]
    r_k = vec_ref[2:3, :]
    g_ffn = vec_ref[3:4, :]
    o = o_ref[...].astype(F32)
    inv_n = 1.0 / hd
    mu = _head_sum(o, hb_ref) * inv_n
    oc = o - mu
    var = _head_sum(oc * oc, hb_ref) * inv_n
    on = (oc * lax.rsqrt(var + GN_EPS)) * lnx_w + lnx_b
    v = v_ref[...].astype(F32)
    rk = r_ref[...].astype(F32) * k_ref[...].astype(F32)
    bonus = _head_sum(rk * r_k, hb_ref) * v
    yb = _dot(((on + bonus) * g_ref[...]).astype(BF16), wpb_ref[...])
    mixed = ya + jax.nn.sigmoid(gb_ref[...].astype(F32)) * yb
    h = x_ref[...] + _dot(mixed.astype(BF16), wout_ref[...])
    h_ref[...] = h
    f_ref[...] = _rms_rows(h, g_ffn).astype(BF16)


def _mix(zu, vn, o, g, p_main, x2, sgu_w, bias_full, vecs, hb, wpa16, wpb16, wout16, tm, d):
    m = x2.shape[0]
    ch = SGU_CHUNK
    tile = pl.BlockSpec((tm, d), lambda i: (i, 0))
    col = lambda cb: pl.BlockSpec((tm, d), lambda i: (i, cb))
    const = lambda a: pl.BlockSpec(a.shape, lambda i: (0,) * a.ndim,
                                   pipeline_mode=pl.Buffered(1))
    return pl.pallas_call(
        functools.partial(_mix_kernel, hd=RWKV_HEAD),
        grid=(m // tm,),
        in_specs=[tile, tile, col(3), tile, col(0), col(1), col(2), tile, col(4), tile,
                  const(sgu_w), const(bias_full), const(vecs), const(hb),
                  const(wpa16), const(wpb16), const(wout16)],
        out_specs=[tile, tile],
        out_shape=[jax.ShapeDtypeStruct((m, d), F32), jax.ShapeDtypeStruct((m, d), BF16)],
        scratch_shapes=[pltpu.VMEM((SGU_GROUPS, ch, ch), BF16), pltpu.VMEM((tm, d), BF16)],
        compiler_params=_cparams(("arbitrary",)),
        name="mix",
    )(zu, vn, p_main, o, p_main, p_main, p_main, g, p_main, x2,
      sgu_w, bias_full, vecs, hb, wpa16, wpb16, wout16)


def _ffn_kernel(h_ref, f_ref, w1_ref, w2_ref, gf_ref, o_ref, *, tf):
    acc = h_ref[...]
    for j in range(w1_ref.shape[1] // tf):
        cols = slice(j * tf, (j + 1) * tf)
        t = jnp.maximum(_dot(f_ref[...], w1_ref[:, cols]), 0.0)
        acc = acc + _dot((t * t).astype(BF16), w2_ref[cols, :])
    o_ref[...] = _rms_rows(acc, gf_ref[...])


def _ffn(h, f, w1_16, w2_16, g_final, tm, tf):
    m, d = h.shape
    dff = w1_16.shape[1]
    assert dff % tf == 0
    tile = pl.BlockSpec((tm, d), lambda i: (i, 0))
    const = lambda shape: pl.BlockSpec(shape, lambda i: (0, 0), pipeline_mode=pl.Buffered(1))
    return pl.pallas_call(
        functools.partial(_ffn_kernel, tf=tf),
        grid=(m // tm,),
        in_specs=[tile, tile, const((d, dff)), const((dff, d)), const((1, d))],
        out_specs=tile,
        out_shape=jax.ShapeDtypeStruct((m, d), F32),
        compiler_params=pltpu.CompilerParams(
            dimension_semantics=("parallel",), vmem_limit_bytes=VMEM_LIMIT,
            allow_input_fusion=[False, False, True, True, False]),
        name="ffn",
    )(h, f, w1_16, w2_16, g_final)


def _pad_cols(a, n):
    return jnp.pad(a, ((0, 0), (0, n - a.shape[1])))


def _pad_rows(a, n):
    return jnp.pad(a, ((0, n - a.shape[0]), (0, 0)))


def _layer(x2, batch, seq, g_mix, w_in, sgu_ln_w, sgu_ln_b, sgu_w, sgu_b, w_proj_a, shift_b,
           w_lora_w, w0, a_lora_w, a0, g_lora_w, k_k, k_a, r_k, ln_x_w, ln_x_b, w_proj_b,
           w_out, g_ffn, w_ffn1, w_ffn2, g_out):
    d = x2.shape[1]
    lora_w, lora_a, lora_g = w_lora_w.shape[0], a_lora_w.shape[0], g_lora_w.shape[0]
    c_sgu = 2 * d
    c_rkv = 3 * d
    c_lora = lora_w + lora_a + lora_g
    o_lora = c_sgu + c_rkv
    o_gate = o_lora + c_lora
    pw, pa = LANES, LANES
    pg = -(-lora_g // LANES) * LANES

    n_lora = pw + pa + pg
    w16 = w_in.astype(BF16)
    w_all = jnp.concatenate([
        w16[:, :o_lora], w16[:, o_gate:],
        _pad_cols(w16[:, o_lora:o_lora + lora_w], pw),
        _pad_cols(w16[:, o_lora + lora_w:o_lora + lora_w + lora_a], pa),
        _pad_cols(w16[:, o_lora + lora_w + lora_a:o_gate], pg + d - n_lora)], axis=1)
    sb_lo = shift_b[:, c_rkv:]
    sb_all = jnp.concatenate([
        jnp.zeros((2, c_sgu), F32), shift_b[:, :c_rkv], jnp.zeros((2, 2 * d), F32),
        _pad_cols(sb_lo[:, :lora_w], pw),
        _pad_cols(sb_lo[:, lora_w:lora_w + lora_a], pa),
        _pad_cols(sb_lo[:, lora_w + lora_a:], pg + d - n_lora)], axis=1)

    gw = SCAN_GROUP_HEADS * RWKV_HEAD
    hb = (lax.broadcasted_iota(jnp.int32, (gw, gw), 0) // RWKV_HEAD
          == lax.broadcasted_iota(jnp.int32, (gw, gw), 1) // RWKV_HEAD).astype(BF16)
    vec_prep = jnp.stack([w0, a0, k_k, k_a])
    lo16 = lambda w, rows: _pad_rows(w, rows).astype(BF16)
    zu, vn, p_main, lw, kk, aa, g = _in_proj(
        x2, g_mix.reshape(1, d), w_all, sb_all, sgu_ln_w.reshape(1, d), sgu_ln_b.reshape(1, d),
        lo16(w_lora_w, pw), lo16(a_lora_w, pa), lo16(g_lora_w, pg), vec_prep, hb, seq,
        tm=IN_PROJ_ROWS, n_mix=c_rkv // d, key_tile=1, n_lora=n_lora)

    bias_full = jnp.repeat(sgu_b.T, d // SGU_GROUPS, axis=1)

    o = _rwkv_scan(p_main, lw, kk, aa, batch, seq, d, bb=SCAN_SEQS, tblk=SCAN_ROWS)

    vec_post = jnp.stack([ln_x_w, ln_x_b, r_k, g_ffn])
    h1, f = _mix(zu, vn, o, g, p_main, x2, sgu_w, bias_full, vec_post, hb,
                 w_proj_a.astype(BF16), w_proj_b.astype(BF16), w_out.astype(BF16),
                 tm=MIX_ROWS, d=d)
    return _ffn(h1, f, w_ffn1.astype(BF16), w_ffn2.astype(BF16), g_out.reshape(1, d),
                tm=FFN_ROWS, tf=FFN_COLS)


def kernel(x, g_mix, w_in, sgu_ln_w, sgu_ln_b, sgu_w, sgu_b, w_proj_a, shift_b, w_lora_w, w0,
           a_lora_w, a0, g_lora_w, k_k, k_a, r_k, ln_x_w, ln_x_b, w_proj_b, w_out, g_ffn,
           w_ffn1, w_ffn2, g_final):
    batch, seq, d = x.shape
    depth = w_in.shape[0]
    assert depth == 1, "the final RMSNorm is fused into the single layer's ffn call"
    h = x.reshape(batch * seq, d)
    l = 0
    h = _layer(h, batch, seq, g_mix[l], w_in[l], sgu_ln_w[l], sgu_ln_b[l], sgu_w[l], sgu_b[l],
               w_proj_a[l], shift_b[l], w_lora_w[l], w0[l], a_lora_w[l], a0[l], g_lora_w[l],
               k_k[l], k_a[l], r_k[l], ln_x_w[l], ln_x_b[l], w_proj_b[l], w_out[l], g_ffn[l],
               w_ffn1[l], w_ffn2[l], g_final)
    return h.reshape(batch, seq, d)
```
